```python
import math
import jax
import jax.numpy as jnp
from jax import lax
import numpy as np

D_MODEL = 1024
BATCH = 2
SEQ = 16384
DEPTH = 2
DEC_BATCH = 32
DEC_SEQ = 32
PAST_LEN = 4096

CHUNK = 64
N_META = 16
QBLK = 128
HEAD_DIM = 64
A_HEADS = (D_MODEL // 2) // HEAD_DIM
A_KV_HEADS = 2
IDX_HEADS = 4
IDX_DIM = 64
TOPK_MAX = 256
B_HEADS = (D_MODEL // 2) // HEAD_DIM
C_HEADS = D_MODEL // HEAD_DIM
C_KV_HEADS = 2
WINDOW = 128
WIN_CHUNKS = WINDOW // CHUNK
NUM_BUCKETS = 32
MAX_DISTANCE = 128
BIAS_HEADS = C_HEADS
D_FF = 2816
N_EXPERTS = 8
TOP_K = 2
D_EXPERT = 3584
MOE_BLOCK = 256
LN_EPS = 1e-5
DN_ALPHA = (2.0 * DEPTH) ** 0.25
DN_BETA = (8.0 * DEPTH) ** -0.25
L0_SPLITS = (A_HEADS * HEAD_DIM, A_KV_HEADS * HEAD_DIM, A_KV_HEADS * HEAD_DIM,
             IDX_HEADS * IDX_DIM, IDX_DIM, IDX_HEADS,
             B_HEADS * HEAD_DIM, B_HEADS * HEAD_DIM, B_HEADS * HEAD_DIM)
L1_SPLITS = (C_HEADS * HEAD_DIM, C_KV_HEADS * HEAD_DIM, C_KV_HEADS * HEAD_DIM)

kernel_name = 'chunk_causal_dsa_stickbreak_swa_sink_moe_step'

F32 = jnp.float32


def layer_norm(x, g, b):
    xf = x.astype(F32)
    mu = xf.mean(-1, keepdims=True)
    var = jnp.square(xf - mu).mean(-1, keepdims=True)
    return ((xf - mu) * lax.rsqrt(var + LN_EPS) * g + b).astype(x.dtype)


def t5_bucket(rel):
    half = NUM_BUCKETS // 2
    exact = half // 2
    n = jnp.abs(rel)
    far = exact + (jnp.log(jnp.maximum(n, 1).astype(F32) / exact)
                   / math.log(MAX_DISTANCE / exact) * (half - exact)).astype(jnp.int32)
    far = jnp.minimum(far, half - 1)
    return jnp.where(rel > 0, half, 0) + jnp.where(n < exact, n, far)


def prompt_chunk_id(pos):
    return jnp.where(pos < N_META, 0, 1 + (pos - N_META) // CHUNK)


def split_cols(h, sizes):
    return jnp.split(h, np.cumsum(sizes)[:-1].tolist(), axis=-1)


def sweep_blocks(fn, n_blocks):
    out = lax.map(fn, jnp.arange(n_blocks, dtype=jnp.int32))
    out = jnp.moveaxis(out, 0, 1)
    return out.reshape(out.shape[0], n_blocks * QBLK, out.shape[-1])


def swiglu(x, w_gate, w_up, w_down):
    return (jax.nn.silu(x @ w_gate) * (x @ w_up)) @ w_down


def moe_swiglu(x, router, w_gate, w_up, w_down):
    shp = x.shape
    xt = x.reshape(-1, shp[-1])
    n = xt.shape[0]
    logits = xt.astype(F32) @ router.astype(F32)
    top_val, top_idx = lax.top_k(logits, TOP_K)
    gates = jax.nn.softmax(top_val, axis=-1)
    e_flat = top_idx.reshape(-1)
    t_flat = jnp.repeat(jnp.arange(n, dtype=jnp.int32), TOP_K)
    g_flat = gates.reshape(-1)
    n_assign = n * TOP_K
    counts = jnp.bincount(e_flat, length=N_EXPERTS)
    padded = (counts + MOE_BLOCK - 1) // MOE_BLOCK * MOE_BLOCK
    pad_end = jnp.cumsum(padded)
    order = jnp.argsort(e_flat)
    e_sorted = e_flat[order]
    rank = jnp.arange(n_assign, dtype=jnp.int32) - (jnp.cumsum(counts) - counts)[e_sorted]
    slot = (pad_end - padded)[e_sorted] + rank
    n_blocks = -(-n_assign // MOE_BLOCK) + N_EXPERTS
    n_slots = n_blocks * MOE_BLOCK
    slot_tok = jnp.full((n_slots,), n, jnp.int32).at[slot].set(t_flat[order])
    slot_gate = jnp.zeros((n_slots,), F32).at[slot].set(g_flat[order])
    block_expert = jnp.minimum(
        jnp.searchsorted(pad_end, jnp.arange(n_blocks, dtype=jnp.int32) * MOE_BLOCK, side='right'),
        N_EXPERTS - 1)
    x_pad = jnp.concatenate([xt, jnp.zeros((1, shp[-1]), xt.dtype)], axis=0)

    def expert_block(args):
        toks, gate, e = args
        xb = x_pad[toks]
        hb = jax.nn.silu(xb @ w_gate[e]) * (xb @ w_up[e])
        return (hb @ w_down[e]).astype(F32) * gate[:, None]

    outs = lax.map(expert_block, (slot_tok.reshape(n_blocks, MOE_BLOCK),
                                  slot_gate.reshape(n_blocks, MOE_BLOCK), block_expert))
    y = jax.ops.segment_sum(outs.reshape(n_slots, shp[-1]), slot_tok, num_segments=n + 1)[:n]
    return y.astype(x.dtype).reshape(shp)


def dsa_attend(qa, qi, wi, q_pos, q_cid, ka, va, ki, k_pos, k_cid, k_ok, rel_bias, n_sel):
    bsz, nq = qa.shape[:2]
    s = jnp.einsum('bqhe,bke->bqhk', qi.astype(F32), ki.astype(F32)) * IDX_DIM ** -0.5
    score = jnp.einsum('bqh,bqhk->bqk', wi.astype(F32), jax.nn.relu(s))
    allowed = (k_cid[None, :] <= q_cid[:, None]) & k_ok[None, :]
    score = jnp.where(allowed[None], score, -jnp.inf)
    _, idx = lax.top_k(score, n_sel)
    sel_ok = allowed[jnp.arange(nq)[:, None], idx]
    bidx = jnp.arange(bsz)[:, None, None]
    k_sel = ka[bidx, idx].astype(F32)
    v_sel = va[bidx, idx].astype(F32)
    g, r = A_KV_HEADS, A_HEADS // A_KV_HEADS
    bias = rel_bias[:, :A_HEADS][t5_bucket(k_pos[idx] - q_pos[None, :, None])].astype(F32)
    bias = bias.reshape(bsz, nq, n_sel, g, r).transpose(0, 1, 3, 4, 2)
    q = qa.astype(F32).reshape(bsz, nq, g, r, HEAD_DIM)
    logits = jnp.einsum('bqgrd,bqngd->bqgrn', q, k_sel) * HEAD_DIM ** -0.5 + bias
    logits = jnp.where(sel_ok[:, :, None, None, :], logits, -jnp.inf)
    p = jax.nn.softmax(logits, axis=-1)
    out = jnp.einsum('bqgrn,bqngd->bqgrd', p, v_sel)
    return out.reshape(bsz, nq, A_HEADS * HEAD_DIM).astype(qa.dtype)


def stick_breaking(qb, q_pos, kb, vb, k_pos, k_ok):
    bsz, nq = qb.shape[:2]
    z = jnp.einsum('bqhd,bkhd->bhqk', qb.astype(F32), kb.astype(F32)) * HEAD_DIM ** -0.5
    before = (k_pos[None, :] < q_pos[:, None]) & k_ok[None, :]
    log_keep = jnp.where(before, jax.nn.log_sigmoid(-z), 0.0)
    between = lax.cumsum(log_keep, axis=3, reverse=True) - log_keep
    w = jnp.where(before, jnp.exp(jax.nn.log_sigmoid(z) + between), 0.0)
    out = jnp.einsum('bhqk,bkhd->bqhd', w, vb.astype(F32))
    return out.reshape(bsz, nq, B_HEADS * HEAD_DIM).astype(qb.dtype)


def attend_l1(qc, kc, vc, q_pos, q_cid, k_pos, k_cid, k_ok, rel_bias, sinks):
    bsz, nq = qc.shape[:2]
    nk = kc.shape[1]
    g, r = C_KV_HEADS, C_HEADS // C_KV_HEADS
    mask = (k_ok[None, :] & (k_cid[None, :] <= q_cid[:, None])
            & (k_cid[None, :] >= q_cid[:, None] - WIN_CHUNKS))
    bias = rel_bias[t5_bucket(k_pos[None, :] - q_pos[:, None])].astype(F32)
    bias = bias.reshape(nq, nk, g, r).transpose(2, 3, 0, 1)
    q = qc.astype(F32).reshape(bsz, nq, g, r, HEAD_DIM)
    logits = jnp.einsum('bqgrd,bkgd->bgrqk', q, kc.astype(F32)) * HEAD_DIM ** -0.5 + bias
    logits = jnp.where(mask, logits, -jnp.inf)
    sink = sinks.astype(F32).reshape(1, g, r, 1, 1)
    m = jnp.maximum(logits.max(-1, keepdims=True), sink)
    e = jnp.exp(logits - m)
    p = e / (e.sum(-1, keepdims=True) + jnp.exp(sink - m))
    out = jnp.einsum('bgrqk,bkgd->bqgrd', p, vc.astype(F32))
    return out.reshape(bsz, nq, C_HEADS * HEAD_DIM).astype(qc.dtype)


def project_l0(x, w_in):
    bsz, n, _ = x.shape
    qa, ka, va, qi, ki, wi, qb, kb, vb = split_cols(x @ w_in, L0_SPLITS)
    return (qa.reshape(bsz, n, A_HEADS, HEAD_DIM), ka.reshape(bsz, n, A_KV_HEADS, HEAD_DIM),
            va.reshape(bsz, n, A_KV_HEADS, HEAD_DIM), qi.reshape(bsz, n, IDX_HEADS, IDX_DIM),
            ki, wi * IDX_HEADS ** -0.5, qb.reshape(bsz, n, B_HEADS, HEAD_DIM),
            kb.reshape(bsz, n, B_HEADS, HEAD_DIM), vb.reshape(bsz, n, B_HEADS, HEAD_DIM))


def project_l1(x, w_in):
    bsz, n, _ = x.shape
    qc, kc, vc = split_cols(x @ w_in, L1_SPLITS)
    return (qc.reshape(bsz, n, C_HEADS, HEAD_DIM), kc.reshape(bsz, n, C_KV_HEADS, HEAD_DIM),
            vc.reshape(bsz, n, C_KV_HEADS, HEAD_DIM))


def attend_l0(qs, ks, q_pos, q_cid, k_pos, k_cid, k_ok, rel_bias, n_sel):
    qa, qi, wi, qb = qs
    ka, va, ki, kb, vb = ks
    oa = dsa_attend(qa, qi, wi, q_pos, q_cid, ka, va, ki, k_pos, k_cid, k_ok, rel_bias, n_sel)
    ob = stick_breaking(qb, q_pos, kb, vb, k_pos, k_ok)
    return jnp.concatenate([oa, ob], axis=-1)


def mix_residual(x, o, w_out, g, b):
    return layer_norm(DN_ALPHA * x + o @ w_out, g, b)


def prompt_layer0(x, rel_bias, w_in, w_out, ln1_g, ln1_b, w_gate, w_up, w_down, ln2_g, ln2_b):
    bsz, t, _ = x.shape
    nb = -(-t // QBLK)
    t_pad = nb * QBLK
    qa, ka, va, qi, ki, wi, qb, kb, vb = project_l0(x, w_in)

    def padt(a):
        return jnp.pad(a, [(0, 0), (0, t_pad - t)] + [(0, 0)] * (a.ndim - 2))

    qs = (padt(qa), padt(qi), padt(wi), padt(qb))
    ks = (padt(ka), padt(va), padt(ki), padt(kb), padt(vb))
    pos = jnp.arange(t_pad, dtype=jnp.int32)
    cid = prompt_chunk_id(pos)
    ok = pos < t
    n_sel = min(TOPK_MAX, (t - N_META) // 4)

    def block(b):
        s0 = b * QBLK
        qblk = tuple(lax.dynamic_slice_in_dim(a, s0, QBLK, axis=1) for a in qs)
        q_pos = lax.dynamic_slice_in_dim(pos, s0, QBLK)
        return attend_l0(qblk, ks, q_pos, prompt_chunk_id(q_pos), pos, cid, ok, rel_bias, n_sel)

    o = sweep_blocks(block, nb)[:, :t]
    y = mix_residual(x, o, w_out, ln1_g, ln1_b)
    y = layer_norm(DN_ALPHA * y + swiglu(y, w_gate, w_up, w_down), ln2_g, ln2_b)
    return y, (ka, va, ki, kb, vb)


def sample_layer0(x, c_ak, c_av, c_ai, c_bk, c_bv, rel_bias, w_in, w_out, ln1_g, ln1_b,
                  w_gate, w_up, w_down, ln2_g, ln2_b):
    n = x.shape[1]
    past = c_ak.shape[1]
    qa, ka, va, qi, ki, wi, qb, kb, vb = project_l0(x, w_in)
    ks = (jnp.concatenate([c_ak, ka], axis=1), jnp.concatenate([c_av, va], axis=1),
          jnp.concatenate([c_ai, ki], axis=1), jnp.concatenate([c_bk, kb], axis=1),
          jnp.concatenate([c_bv, vb], axis=1))
    k_pos = jnp.arange(past + n, dtype=jnp.int32)
    q_pos = past + jnp.arange(n, dtype=jnp.int32)
    o = attend_l0((qa, qi, wi, qb), ks, q_pos, q_pos // CHUNK, k_pos, k_pos // CHUNK,
                  jnp.ones((past + n,), bool), rel_bias, min(TOPK_MAX, (past + n) // 4))
    y = mix_residual(x, o, w_out, ln1_g, ln1_b)
    y = layer_norm(DN_ALPHA * y + swiglu(y, w_gate, w_up, w_down), ln2_g, ln2_b)
    return y, (ka, va, ki, kb, vb)


def prompt_layer1(x, rel_bias, w_in, sinks, w_out, ln1_g, ln1_b, router, w_gate, w_up, w_down,
                  ln2_g, ln2_b):
    bsz, t, _ = x.shape
    nb = -(-t // QBLK)
    t_pad = nb * QBLK
    qc, kc, vc = project_l1(x, w_in)
    qc_p = jnp.pad(qc, ((0, 0), (0, t_pad - t), (0, 0), (0, 0)))
    band_pad = ((0, 0), (2 * QBLK, t_pad - t + QBLK), (0, 0), (0, 0))
    kc_p = jnp.pad(kc, band_pad)
    vc_p = jnp.pad(vc, band_pad)

    def block(b):
        s0 = b * QBLK
        q_pos = s0 + jnp.arange(QBLK, dtype=jnp.int32)
        k_pos = s0 - 2 * QBLK + jnp.arange(4 * QBLK, dtype=jnp.int32)
        k_ok = (k_pos >= 0) & (k_pos < t)
        return attend_l1(lax.dynamic_slice_in_dim(qc_p, s0, QBLK, axis=1),
                         lax.dynamic_slice_in_dim(kc_p, s0, 4 * QBLK, axis=1),
                         lax.dynamic_slice_in_dim(vc_p, s0, 4 * QBLK, axis=1),
                         q_pos, prompt_chunk_id(q_pos), k_pos, prompt_chunk_id(k_pos), k_ok,
                         rel_bias, sinks)

    o = sweep_blocks(block, nb)[:, :t]
    y = mix_residual(x, o, w_out, ln1_g, ln1_b)
    y = layer_norm(DN_ALPHA * y + moe_swiglu(y, router, w_gate, w_up, w_down), ln2_g, ln2_b)
    buf = min(WINDOW, t)
    return y, (kc[:, t - buf:], vc[:, t - buf:])


def sample_layer1(x, c_ck, c_cv, rel_bias, w_in, sinks, w_out, ln1_g, ln1_b, router, w_gate,
                  w_up, w_down, ln2_g, ln2_b):
    n = x.shape[1]
    buf = c_ck.shape[1]
    qc, kc, vc = project_l1(x, w_in)
    k_all = jnp.concatenate([c_ck, kc], axis=1)
    v_all = jnp.concatenate([c_cv, vc], axis=1)
    k_pos = PAST_LEN - buf + jnp.arange(buf + n, dtype=jnp.int32)
    q_pos = PAST_LEN + jnp.arange(n, dtype=jnp.int32)
    o = attend_l1(qc, k_all, v_all, q_pos, q_pos // CHUNK, k_pos, k_pos // CHUNK,
                  jnp.ones((buf + n,), bool), rel_bias, sinks)
    y = mix_residual(x, o, w_out, ln1_g, ln1_b)
    y = layer_norm(DN_ALPHA * y + moe_swiglu(y, router, w_gate, w_up, w_down), ln2_g, ln2_b)
    return y, (k_all[:, n:], v_all[:, n:])


def setup_inputs(seed: int = 0) -> dict:
    key = jax.random.key(seed)
    keys = iter(jax.random.split(key, 48))

    def nrm(shape, scale):
        return jax.random.normal(next(keys), shape, jnp.float32) * scale

    def gain():
        return 1.0 + nrm((D_MODEL,), 0.01)

    def shift():
        return nrm((D_MODEL,), 0.01)

    c_buf = min(WINDOW, PAST_LEN)
    d_in = D_MODEL ** -0.5
    mix0 = (A_HEADS + B_HEADS) * HEAD_DIM
    mix1 = C_HEADS * HEAD_DIM
    return {
        'x_prompt': nrm((BATCH, SEQ, D_MODEL), 1.0),
        'x_sample': nrm((DEC_BATCH, DEC_SEQ, D_MODEL), 1.0),
        'cache_a_k': nrm((DEC_BATCH, PAST_LEN, A_KV_HEADS, HEAD_DIM), 1.0),
        'cache_a_v': nrm((DEC_BATCH, PAST_LEN, A_KV_HEADS, HEAD_DIM), 1.0),
        'cache_a_idx_k': nrm((DEC_BATCH, PAST_LEN, IDX_DIM), 1.0),
        'cache_b_k': nrm((DEC_BATCH, PAST_LEN, B_HEADS, HEAD_DIM), 1.0),
        'cache_b_v': nrm((DEC_BATCH, PAST_LEN, B_HEADS, HEAD_DIM), 1.0),
        'cache_c_k': nrm((DEC_BATCH, c_buf, C_KV_HEADS, HEAD_DIM), 1.0),
        'cache_c_v': nrm((DEC_BATCH, c_buf, C_KV_HEADS, HEAD_DIM), 1.0),
        'meta_tokens': nrm((N_META, D_MODEL), 1.0),
        'rel_bias': nrm((NUM_BUCKETS, BIAS_HEADS), 0.5),
        'l0_w_in': nrm((D_MODEL, sum(L0_SPLITS)), d_in),
        'l0_w_out': nrm((mix0, D_MODEL), mix0 ** -0.5 * DN_BETA),
        'l0_ln1_g': gain(),
        'l0_ln1_b': shift(),
        'l0_w_gate': nrm((D_MODEL, D_FF), d_in),
        'l0_w_up': nrm((D_MODEL, D_FF), d_in),
        'l0_w_down': nrm((D_FF, D_MODEL), D_FF ** -0.5 * DN_BETA),
        'l0_ln2_g': gain(),
        'l0_ln2_b': shift(),
        'l1_w_in': nrm((D_MODEL, sum(L1_SPLITS)), d_in),
        'l1_sinks': nrm((C_HEADS,), 1.0),
        'l1_w_out': nrm((mix1, D_MODEL), mix1 ** -0.5 * DN_BETA),
        'l1_ln1_g': gain(),
        'l1_ln1_b': shift(),
        'l1_router': nrm((D_MODEL, N_EXPERTS), d_in),
        'l1_w_gate': nrm((N_EXPERTS, D_MODEL, D_EXPERT), d_in),
        'l1_w_up': nrm((N_EXPERTS, D_MODEL, D_EXPERT), d_in),
        'l1_w_down': nrm((N_EXPERTS, D_EXPERT, D_MODEL), D_EXPERT ** -0.5 * DN_BETA),
        'l1_ln2_g': gain(),
        'l1_ln2_b': shift(),
    }


def reference(x_prompt, x_sample, cache_a_k, cache_a_v, cache_a_idx_k, cache_b_k, cache_b_v,
              cache_c_k, cache_c_v, meta_tokens, rel_bias, l0_w_in, l0_w_out, l0_ln1_g, l0_ln1_b,
              l0_w_gate, l0_w_up, l0_w_down, l0_ln2_g, l0_ln2_b, l1_w_in, l1_sinks, l1_w_out,
              l1_ln1_g, l1_ln1_b, l1_router, l1_w_gate, l1_w_up, l1_w_down, l1_ln2_g, l1_ln2_b):
    meta = jnp.broadcast_to(meta_tokens[None].astype(x_prompt.dtype),
                            (x_prompt.shape[0], N_META, D_MODEL))
    hp = jnp.concatenate([meta, x_prompt], axis=1)
    hs = x_sample
    for layer in range(DEPTH):
        if layer % 2 == 0:
            hp, (p_ak, p_av, p_ai, p_bk, p_bv) = prompt_layer0(
                hp, rel_bias, l0_w_in, l0_w_out, l0_ln1_g, l0_ln1_b, l0_w_gate, l0_w_up,
                l0_w_down, l0_ln2_g, l0_ln2_b)
            hs, (s_ak, s_av, s_ai, s_bk, s_bv) = sample_layer0(
                hs, cache_a_k, cache_a_v, cache_a_idx_k, cache_b_k, cache_b_v, rel_bias,
                l0_w_in, l0_w_out, l0_ln1_g, l0_ln1_b, l0_w_gate, l0_w_up, l0_w_down,
                l0_ln2_g, l0_ln2_b)
        else:
            hp, (p_ck, p_cv) = prompt_layer1(
                hp, rel_bias, l1_w_in, l1_sinks, l1_w_out, l1_ln1_g, l1_ln1_b, l1_router,
                l1_w_gate, l1_w_up, l1_w_down, l1_ln2_g, l1_ln2_b)
            hs, (s_ck, s_cv) = sample_layer1(
                hs, cache_c_k, cache_c_v, rel_bias, l1_w_in, l1_sinks, l1_w_out, l1_ln1_g,
                l1_ln1_b, l1_router, l1_w_gate, l1_w_up, l1_w_down, l1_ln2_g, l1_ln2_b)
    return (hp[:, N_META:], hs, p_ak, p_av, p_ai, p_bk, p_bv, p_ck, p_cv,
            s_ak, s_av, s_ai, s_bk, s_bv, s_ck, s_cv)
```

```python
import functools

import jax
import jax.numpy as jnp
from jax import lax
from jax.experimental import pallas as pl
from jax.experimental.pallas import tpu as pltpu

F32 = jnp.float32
BF16 = jnp.bfloat16
I32 = jnp.int32

D_MODEL = 1024
CHUNK_SHIFT = 6
N_META = 16
HEAD_DIM = 64
A_HEADS = 8
A_KV_HEADS = 2
IDX_HEADS = 4
TOPK_MAX = 256
B_HEADS = 8
C_HEADS = 16
C_KV_HEADS = 2
WINDOW = 128
WIN_CHUNKS = 2
NUM_BUCKETS = 32
N_EXPERTS = 8
LN_EPS = 1e-5
DEPTH = 2
DN_ALPHA = (2.0 * DEPTH) ** 0.25

LANE = 128
QBLK = 128
DSA_CK = 512
DSA_PADF = 384
SB_TK = 128
SB_STOP = -120.0
NEG = -1e30
PROMPT_COFF = 64 - N_META
INT_MIN = -2 ** 31
VMEM_LIMIT = 56 * 1024 * 1024


def _cparams(sem):
    return pltpu.CompilerParams(dimension_semantics=sem, vmem_limit_bytes=VMEM_LIMIT)


def _nt(a, b):
    return lax.dot_general(a, b, (((1,), (1,)), ((), ())), preferred_element_type=F32)


def _layer_norm(v, g, b):
    mu = jnp.mean(v, axis=-1, keepdims=True)
    c = v - mu
    var = jnp.mean(c * c, axis=-1, keepdims=True)
    return c * lax.rsqrt(var + LN_EPS) * g + b


def _proj_kernel(x_ref, w_ref, *out_refs, specs):
    xb = x_ref[...].astype(BF16)
    k = 0
    for c0, width, kinds in specs:
        acc = jnp.dot(xb, w_ref[:, c0:c0 + width], preferred_element_type=F32)
        for kind in kinds:
            if kind == "f32":
                val = acc
            elif kind == "bf16":
                val = acc.astype(BF16)
            elif kind == "bf16s":
                val = (acc * 0.125).astype(BF16)
            elif kind == "hilo":
                hi = acc.astype(BF16)
                lo = (acc - hi.astype(F32)).astype(BF16)
                lane = lax.broadcasted_iota(I32, acc.shape, 1) & (LANE - 1)
                val = jnp.where(lane < HEAD_DIM, hi, lo)
            elif kind == "wi":
                val = acc * 0.0625
            else:
                raise ValueError(kind)
            out_refs[k][...] = val
            k += 1


def _project(x, w16, specs, tm):
    n = x.shape[0]
    out_shape, out_specs = [], []
    for _, width, kinds in specs:
        for kind in kinds:
            dt = F32 if kind in ("f32", "wi") else BF16
            out_shape.append(jax.ShapeDtypeStruct((n, width), dt))
            out_specs.append(pl.BlockSpec((tm, width), lambda i: (i, 0)))
    return pl.pallas_call(
        functools.partial(_proj_kernel, specs=specs),
        grid=(pl.cdiv(n, tm),),
        in_specs=[pl.BlockSpec((tm, x.shape[1]), lambda i: (i, 0)),
                  pl.BlockSpec(w16.shape, lambda i: (0, 0))],
        out_specs=out_specs,
        out_shape=out_shape,
        compiler_params=_cparams(("parallel",)),
        name="proj",
    )(x, w16)


def _mix_ln_kernel(x_ref, *refs, n_pairs):
    o_refs = refs[:n_pairs]
    w_refs = refs[n_pairs:2 * n_pairs]
    g_ref, b_ref, y_ref = refs[2 * n_pairs:]
    acc = DN_ALPHA * x_ref[...]
    for o_ref, w_ref in zip(o_refs, w_refs):
        acc = acc + jnp.dot(o_ref[...], w_ref[...], preferred_element_type=F32)
    y_ref[...] = _layer_norm(acc, g_ref[...], b_ref[...])


def _mix_ln(x, os_, ws, g, b, tm):
    n, d = x.shape
    in_specs = [pl.BlockSpec((tm, d), lambda i: (i, 0))]
    in_specs += [pl.BlockSpec((tm, o.shape[1]), lambda i: (i, 0)) for o in os_]
    in_specs += [pl.BlockSpec(w.shape, lambda i: (0, 0)) for w in ws]
    in_specs += [pl.BlockSpec((1, d), lambda i: (0, 0))] * 2
    return pl.pallas_call(
        functools.partial(_mix_ln_kernel, n_pairs=len(os_)),
        grid=(pl.cdiv(n, tm),),
        in_specs=in_specs,
        out_specs=pl.BlockSpec((tm, d), lambda i: (i, 0)),
        out_shape=jax.ShapeDtypeStruct((n, d), F32),
        compiler_params=_cparams(("parallel",)),
        name="mix_ln",
    )(x, *os_, *ws, g.reshape(1, d), b.reshape(1, d))


def _ffn_ln_kernel(y_ref, wg_ref, wu_ref, wd_ref, g_ref, b_ref, o_ref, *, nf):
    y = y_ref[...]
    yb = y.astype(BF16)
    acc = DN_ALPHA * y
    for f in range(nf):
        hg = jnp.dot(yb, wg_ref[f], preferred_element_type=F32)
        hu = jnp.dot(yb, wu_ref[f], preferred_element_type=F32)
        h = (hg * jax.nn.sigmoid(hg) * hu).astype(BF16)
        acc = acc + jnp.dot(h, wd_ref[f], preferred_element_type=F32)
    o_ref[...] = _layer_norm(acc, g_ref[...], b_ref[...])


def _ffn_ln(y, wg, wu, wd, g, b, tm, tf):
    n, d = y.shape
    dff = wg.shape[1]
    nf = dff // tf
    wg3 = wg.astype(BF16).reshape(d, nf, tf).transpose(1, 0, 2)
    wu3 = wu.astype(BF16).reshape(d, nf, tf).transpose(1, 0, 2)
    wd3 = wd.astype(BF16).reshape(nf, tf, d)
    full3 = lambda i: (0, 0, 0)
    return pl.pallas_call(
        functools.partial(_ffn_ln_kernel, nf=nf),
        grid=(pl.cdiv(n, tm),),
        in_specs=[pl.BlockSpec((tm, d), lambda i: (i, 0)),
                  pl.BlockSpec(wg3.shape, full3), pl.BlockSpec(wu3.shape, full3),
                  pl.BlockSpec(wd3.shape, full3),
                  pl.BlockSpec((1, d), lambda i: (0, 0)), pl.BlockSpec((1, d), lambda i: (0, 0))],
        out_specs=pl.BlockSpec((tm, d), lambda i: (i, 0)),
        out_shape=jax.ShapeDtypeStruct((n, d), F32),
        compiler_params=_cparams(("parallel",)),
        name="ffn_ln",
    )(y, wg3, wu3, wd3, g.reshape(1, d), b.reshape(1, d))


def _dsa_kernel(rb_ref, qi_ref, wi_ref, qa_ref, ki_ref, ka_ref, va_ref, bn_ref, o_ref,
                keys_ref, cut_ref, *, tq, n_sel, e_base, e_step, q_base, q_step, coff, n_valid, padf):
    i = pl.program_id(1)
    ck = DSA_CK
    e_end = e_base + e_step * i
    n_chunks = (e_end + ck - 1) // ck
    qpos = q_base + q_step * i + lax.broadcasted_iota(I32, (tq, 1), 0)
    bound = jnp.minimum(((((qpos + coff) >> CHUNK_SHIFT) + 1) << CHUNK_SHIFT) - coff, n_valid)
    lane_ck = lax.broadcasted_iota(I32, (tq, ck), 1)
    lane128 = lax.broadcasted_iota(I32, (tq, LANE), 1)

    def key_rows(c):
        kpos0 = e_end - ck * (c + 1)
        return kpos0, pl.multiple_of(kpos0 + padf, LANE)

    qstack = jnp.concatenate([qi_ref[:, h * LANE:(h + 1) * LANE] for h in range(IDX_HEADS)], axis=0)
    wcols = [wi_ref[:, h:h + 1] for h in range(IDX_HEADS)]

    def p1(c, carry):
        kpos0, r = key_rows(c)
        s = jnp.maximum(_nt(qstack, ki_ref[pl.ds(r, ck), :]), 0.0)
        sc = wcols[0] * s[0:tq]
        for h in range(1, IDX_HEADS):
            sc = sc + wcols[h] * s[h * tq:(h + 1) * tq]
        kpos = kpos0 + lane_ck
        allowed = (kpos >= 0) & (kpos < bound)
        bits = lax.bitcast_convert_type(sc, I32)
        u = bits ^ ((bits >> 31) & 0x7FFFFFFF)
        keys_ref[c] = jnp.where(allowed, u, INT_MIN)
        return carry

    lax.fori_loop(0, n_chunks, p1, 0)

    def count_ge(cand):
        def body(c, acc):
            u = keys_ref[c]
            for t in range(ck // LANE):
                acc = acc + jnp.where(u[:, t * LANE:(t + 1) * LANE] >= cand, 1.0, 0.0)
            return acc
        acc = lax.fori_loop(0, n_chunks, body, jnp.zeros((tq, LANE), F32))
        return jnp.sum(acc, axis=1, keepdims=True)

    def bis(it, t):
        cand = t + (jnp.int32(1) << (31 - it))
        return jnp.where(count_ge(cand) >= n_sel, cand, t)

    thr = lax.fori_loop(0, 32, bis, jnp.full((tq, 1), INT_MIN, I32))
    thr = jnp.maximum(thr, INT_MIN + 1)
    n_gt = count_ge(thr + 1)
    n_eq = count_ge(thr) - n_gt
    need = n_sel - n_gt
    excess = n_eq > need

    cut_ref[...] = jnp.full((tq, LANE), 2 ** 30, I32)

    @pl.when(jnp.max(jnp.where(excess, 1.0, 0.0)) > 0.0)
    def _():
        def count_eq_below(p):
            def body(c, acc):
                kpos0, _ = key_rows(c)
                u = keys_ref[c]
                for t in range(ck // LANE):
                    kp = kpos0 + t * LANE + lane128
                    hit = (u[:, t * LANE:(t + 1) * LANE] == thr) & (kp < p)
                    acc = acc + jnp.where(hit, 1.0, 0.0)
                return acc
            acc = lax.fori_loop(0, n_chunks, body, jnp.zeros((tq, LANE), F32))
            return jnp.sum(acc, axis=1, keepdims=True)

        def pb(it, p):
            cand = p + (jnp.int32(1) << (20 - it))
            return jnp.where(count_eq_below(cand) < need, cand, p)

        p = lax.fori_loop(0, 21, pb, jnp.zeros((tq, 1), I32))
        cut = jnp.where(excess, p + 1, 2 ** 30)
        cut_ref[...] = jnp.broadcast_to(cut, (tq, LANE))

    cut = cut_ref[:, 0:1]

    r4 = A_HEADS // A_KV_HEADS
    qg = []
    for g in range(A_KV_HEADS):
        rows = []
        for j in range(r4):
            slot = qa_ref[:, j * LANE:(j + 1) * LANE]
            half = (lane128 < HEAD_DIM) if g == 0 else (lane128 >= HEAD_DIM)
            rows.append(jnp.where(half, slot, jnp.zeros_like(slot)))
        qg.append(jnp.concatenate(rows, axis=0))
    farb = [jnp.concatenate([jnp.full((tq, 1), rb_ref[g * r4 + j], F32) for j in range(r4)], axis=0)
            for g in range(A_KV_HEADS)]

    def attend(c, state, near):
        kpos0, r = key_rows(c)
        u = keys_ref[c]
        kpos = kpos0 + lane_ck
        sel = (u > thr) | ((u == thr) & (kpos < cut))
        sel4 = jnp.concatenate([sel] * r4, axis=0)
        kt = ka_ref[pl.ds(r, ck), :]
        vt = va_ref[pl.ds(r, ck), :]
        new = []
        for g in range(A_KV_HEADS):
            m_p, l_p, a_p = state[g]
            s = _nt(qg[g], kt)
            s = s + (bn_ref[g] if near else farb[g])
            s = jnp.where(sel4, s, NEG)
            m_n = jnp.maximum(m_p, jnp.max(s, axis=1, keepdims=True))
            alpha = jnp.exp(m_p - m_n)
            p = jnp.exp(s - m_n)
            l_n = alpha * l_p + jnp.sum(p, axis=1, keepdims=True)
            a_n = alpha * a_p + jnp.dot(p.astype(BF16), vt, preferred_element_type=F32)
            new.append((m_n, l_n, a_n))
        return tuple(new)

    init = tuple((jnp.full((r4 * tq, 1), NEG, F32), jnp.zeros((r4 * tq, 1), F32),
                  jnp.zeros((r4 * tq, LANE), F32)) for _ in range(A_KV_HEADS))
    state = attend(0, init, True)
    state = lax.fori_loop(1, n_chunks, lambda c, st: attend(c, st, False), state)
    outs = [a / l for (_, l, a) in state]
    for j in range(r4):
        lo = outs[0][j * tq:(j + 1) * tq]
        hi = outs[1][j * tq:(j + 1) * tq]
        o_ref[:, j * LANE:(j + 1) * LANE] = jnp.where(lane128 < HEAD_DIM, lo, hi).astype(BF16)


def _dsa(rb_far, qi, wi, qa, ki, ka, va, bn, *, nb, nq, tq, row0, n_sel, e_base, e_step,
         q_base, q_step, coff, n_valid, padf, n_chunks_max):
    rb0 = row0 // tq
    qmap = lambda b, i: (rb0 + b * nq + i, 0)
    kmap = lambda b, i: (b, 0, 0)
    kern = functools.partial(_dsa_kernel, tq=tq, n_sel=n_sel, e_base=e_base, e_step=e_step,
                             q_base=q_base, q_step=q_step, coff=coff, n_valid=n_valid, padf=padf)
    return pl.pallas_call(
        kern,
        grid=(nb, nq),
        in_specs=[pl.BlockSpec(memory_space=pltpu.SMEM),
                  pl.BlockSpec((tq, qi.shape[1]), qmap),
                  pl.BlockSpec((tq, LANE), qmap),
                  pl.BlockSpec((tq, qa.shape[1]), qmap),
                  pl.BlockSpec((None,) + ki.shape[1:], kmap),
                  pl.BlockSpec((None,) + ka.shape[1:], kmap),
                  pl.BlockSpec((None,) + va.shape[1:], kmap),
                  pl.BlockSpec(bn.shape, lambda b, i: (0, 0, 0))],
        out_specs=pl.BlockSpec((tq, qa.shape[1]), lambda b, i: (b * nq + i, 0)),
        out_shape=jax.ShapeDtypeStruct((nb * nq * tq, qa.shape[1]), BF16),
        scratch_shapes=[pltpu.VMEM((n_chunks_max, tq, DSA_CK), I32),
                        pltpu.VMEM((tq, LANE), I32)],
        compiler_params=_cparams(("parallel", "arbitrary")),
        name="dsa",
    )(rb_far, qi, wi, qa, ki, ka, va, bn)


def _sb_kernel(q_ref, k_ref, v_ref, o_ref, *, tq, e_base, e_step, q_base, q_step, padf, n_valid):
    i = pl.program_id(2)
    tk = SB_TK
    e_end = e_base + e_step * i
    n_tiles = e_end // tk
    qpos = q_base + q_step * i + lax.broadcasted_iota(I32, (tq, 1), 0)
    lane = lax.broadcasted_iota(I32, (tq, LANE), 1)
    q = q_ref[...]
    zero = jnp.zeros_like(q)
    qh = [jnp.where(lane < HEAD_DIM, q, zero), jnp.where(lane >= HEAD_DIM, q, zero)]
    rr = lax.broadcasted_iota(I32, (tk, tk), 0)
    cc = lax.broadcasted_iota(I32, (tk, tk), 1)
    tri = jnp.where(rr > cc, 1.0, 0.0).astype(BF16)
    lane_tk = lax.broadcasted_iota(I32, (tq, tk), 1)

    def cond(st):
        m, worst = st[0], st[1]
        return (m < n_tiles) & (worst > SB_STOP)

    def body(st):
        m, _, carry, acc = st
        kpos0 = e_end - tk * (m + 1)
        r = pl.multiple_of(kpos0 + padf, tk)
        kt = k_ref[pl.ds(r, tk), :]
        vt = v_ref[pl.ds(r, tk), :]
        kpos = kpos0 + lane_tk
        before = (kpos < qpos) & (kpos >= 0) & (kpos < n_valid)
        new_c, new_a = [], []
        worst = jnp.float32(-jnp.inf)
        for h in range(2):
            z = _nt(qh[h], kt)
            ls = -(jnp.maximum(z, 0.0) + jnp.log1p(jnp.exp(-jnp.abs(z))))
            lk = jnp.where(before, ls, 0.0)
            hi = lk.astype(BF16)
            lo = (lk - hi.astype(F32)).astype(BF16)
            bl = (jnp.dot(hi, tri, preferred_element_type=F32)
                  + jnp.dot(lo, tri, preferred_element_type=F32))
            w = jnp.where(before, jnp.exp(ls + z + bl + carry[h]), 0.0)
            new_a.append(acc[h] + jnp.dot(w.astype(BF16), vt, preferred_element_type=F32))
            c_n = carry[h] + bl[:, 0:1] + lk[:, 0:1]
            new_c.append(c_n)
            worst = jnp.maximum(worst, jnp.max(c_n))
        return (m + 1, worst, tuple(new_c), tuple(new_a))

    init = (jnp.int32(0), jnp.float32(0.0),
            (jnp.zeros((tq, 1), F32), jnp.zeros((tq, 1), F32)),
            (jnp.zeros((tq, LANE), F32), jnp.zeros((tq, LANE), F32)))
    _, _, _, acc = lax.while_loop(cond, body, init)
    o_ref[...] = jnp.where(lane < HEAD_DIM, acc[0], acc[1]).astype(BF16)


def _stick_break(q, k, v, *, nb, nq, tq, row0, e_base, e_step, q_base, q_step, padf, n_valid):
    rb0 = row0 // tq
    npair = q.shape[1] // LANE
    kern = functools.partial(_sb_kernel, tq=tq, e_base=e_base, e_step=e_step, q_base=q_base,
                             q_step=q_step, padf=padf, n_valid=n_valid)
    kspec = pl.BlockSpec((None, k.shape[1], LANE), lambda b, p, i: (b, 0, p))
    return pl.pallas_call(
        kern,
        grid=(nb, npair, nq),
        in_specs=[pl.BlockSpec((tq, LANE), lambda b, p, i: (rb0 + b * nq + i, p)), kspec, kspec],
        out_specs=pl.BlockSpec((tq, LANE), lambda b, p, i: (b * nq + i, p)),
        out_shape=jax.ShapeDtypeStruct((nb * nq * tq, q.shape[1]), BF16),
        compiler_params=_cparams(("parallel", "parallel", "arbitrary")),
        name="stick_break",
    )(q, k, v)


def _swa_kernel(sink_ref, q_ref, k_ref, v_ref, bias_ref, o_ref, *, tq, ck, r_base, r_step,
                p_base, p_step, q_base, q_step, coff, n_valid):
    i = pl.program_id(1)
    r0 = pl.multiple_of(r_base + r_step * i, 16)
    kpos0 = p_base + p_step * i
    kt = k_ref[pl.ds(r0, ck), :]
    vt = v_ref[pl.ds(r0, ck), :]
    qpos = q_base + q_step * i + lax.broadcasted_iota(I32, (tq, 1), 0)
    kpos = kpos0 + lax.broadcasted_iota(I32, (tq, ck), 1)
    qc = (qpos + coff) >> CHUNK_SHIFT
    hi_b = jnp.minimum(((qc + 1) << CHUNK_SHIFT) - coff, n_valid)
    lo_b = jnp.maximum(((qc - WIN_CHUNKS) << CHUNK_SHIFT) - coff, 0)
    allowed = (kpos >= lo_b) & (kpos < hi_b)
    r8 = C_HEADS // C_KV_HEADS
    allowed8 = jnp.concatenate([allowed] * r8, axis=0)
    lane = lax.broadcasted_iota(I32, (tq, LANE), 1)
    outs = []
    for g in range(C_KV_HEADS):
        rows = []
        for j in range(r8):
            slot = q_ref[:, j * LANE:(j + 1) * LANE]
            half = (lane < HEAD_DIM) if g == 0 else (lane >= HEAD_DIM)
            rows.append(jnp.where(half, slot, jnp.zeros_like(slot)))
        qs = jnp.concatenate(rows, axis=0)
        sink = jnp.concatenate([jnp.full((tq, 1), sink_ref[g * r8 + j], F32) for j in range(r8)], axis=0)
        s = _nt(qs, kt) + bias_ref[g]
        s = jnp.where(allowed8, s, NEG)
        m = jnp.maximum(jnp.max(s, axis=1, keepdims=True), sink)
        e = jnp.exp(s - m)
        den = jnp.sum(e, axis=1, keepdims=True) + jnp.exp(sink - m)
        p = (e / den).astype(BF16)
        outs.append(jnp.dot(p, vt, preferred_element_type=F32))
    for j in range(r8):
        lo = outs[0][j * tq:(j + 1) * tq]
        hi = outs[1][j * tq:(j + 1) * tq]
        o_ref[:, j * LANE:(j + 1) * LANE] = jnp.where(lane < HEAD_DIM, lo, hi).astype(BF16)


def _swa(sinks, q, k, v, bias, *, nb, nq, tq, row0, ck, r_base, r_step, p_base, p_step,
         q_base, q_step, coff, n_valid):
    rb0 = row0 // tq
    kern = functools.partial(_swa_kernel, tq=tq, ck=ck, r_base=r_base, r_step=r_step, p_base=p_base,
                             p_step=p_step, q_base=q_base, q_step=q_step, coff=coff, n_valid=n_valid)
    kmap = lambda b, i: (b, 0, 0)
    return pl.pallas_call(
        kern,
        grid=(nb, nq),
        in_specs=[pl.BlockSpec(memory_space=pltpu.SMEM),
                  pl.BlockSpec((tq, q.shape[1]), lambda b, i: (rb0 + b * nq + i, 0)),
                  pl.BlockSpec((None,) + k.shape[1:], kmap),
                  pl.BlockSpec((None,) + v.shape[1:], kmap),
                  pl.BlockSpec(bias.shape, lambda b, i: (0, 0, 0))],
        out_specs=pl.BlockSpec((tq, q.shape[1]), lambda b, i: (b * nq + i, 0)),
        out_shape=jax.ShapeDtypeStruct((nb * nq * tq, q.shape[1]), BF16),
        compiler_params=_cparams(("parallel", "arbitrary")),
        name="swa",
    )(sinks, q, k, v, bias)


def _router_kernel(y_ref, rh_ref, rl_ref, idx_ref, gate_ref):
    y = y_ref[...]
    yh = y.astype(BF16)
    yl = (y - yh.astype(F32)).astype(BF16)
    rh = rh_ref[...]
    logits = (jnp.dot(yh, rh, preferred_element_type=F32) + jnp.dot(yl, rh, preferred_element_type=F32)
              + jnp.dot(yh, rl_ref[...], preferred_element_type=F32))
    lane = lax.broadcasted_iota(I32, logits.shape, 1)
    logits = jnp.where(lane < N_EXPERTS, logits, -jnp.inf)
    m1 = jnp.max(logits, axis=1, keepdims=True)
    i1 = jnp.min(jnp.where(logits == m1, lane, LANE), axis=1, keepdims=True)
    rest = jnp.where(lane == i1, -jnp.inf, logits)
    m2 = jnp.max(rest, axis=1, keepdims=True)
    i2 = jnp.min(jnp.where(rest == m2, lane, LANE), axis=1, keepdims=True)
    e2 = jnp.exp(m2 - m1)
    den = 1.0 + e2
    idx_ref[...] = jnp.where(lane == 0, i1, jnp.where(lane == 1, i2, 0))
    gate_ref[...] = jnp.where(lane == 0, 1.0 / den, jnp.where(lane == 1, e2 / den, 0.0))


def _router(y, router, tm):
    n, d = y.shape
    rpad = jnp.pad(router.astype(F32), ((0, 0), (0, LANE - router.shape[1])))
    rh = rpad.astype(BF16)
    rl = (rpad - rh.astype(F32)).astype(BF16)
    return pl.pallas_call(
        _router_kernel,
        grid=(pl.cdiv(n, tm),),
        in_specs=[pl.BlockSpec((tm, d), lambda i: (i, 0)),
                  pl.BlockSpec((d, LANE), lambda i: (0, 0)), pl.BlockSpec((d, LANE), lambda i: (0, 0))],
        out_specs=[pl.BlockSpec((tm, LANE), lambda i: (i, 0)), pl.BlockSpec((tm, LANE), lambda i: (i, 0))],
        out_shape=[jax.ShapeDtypeStruct((n, LANE), I32), jax.ShapeDtypeStruct((n, LANE), F32)],
        compiler_params=_cparams(("parallel",)),
        name="router",
    )(y, rh, rl)


def _moe_kernel(be_ref, nu_ref, x_ref, wg_ref, wu_ref, wd_ref, o_ref, acc_ref):
    t = pl.program_id(0)
    f = pl.program_id(1)

    @pl.when(f == 0)
    def _():
        acc_ref[...] = jnp.zeros_like(acc_ref)

    @pl.when(t < nu_ref[0])
    def _():
        xb = x_ref[...]
        hg = jnp.dot(xb, wg_ref[...], preferred_element_type=F32)
        hu = jnp.dot(xb, wu_ref[...], preferred_element_type=F32)
        h = (hg * jax.nn.sigmoid(hg) * hu).astype(BF16)
        acc_ref[...] += jnp.dot(h, wd_ref[...], preferred_element_type=F32)

    @pl.when(f == pl.num_programs(1) - 1)
    def _():
        o_ref[...] = acc_ref[...]


def _moe_experts(block_expert, n_used, xs, wg, wu, wd, tmb, tf):
    n_slots, d = xs.shape
    de = wg.shape[2]
    grid_spec = pltpu.PrefetchScalarGridSpec(
        num_scalar_prefetch=2,
        grid=(n_slots // tmb, de // tf),
        in_specs=[pl.BlockSpec((tmb, d), lambda t, f, be, nu: (t, 0)),
                  pl.BlockSpec((None, d, tf), lambda t, f, be, nu: (be[t], 0, f)),
                  pl.BlockSpec((None, d, tf), lambda t, f, be, nu: (be[t], 0, f)),
                  pl.BlockSpec((None, tf, d), lambda t, f, be, nu: (be[t], f, 0))],
        out_specs=pl.BlockSpec((tmb, d), lambda t, f, be, nu: (t, 0)),
        scratch_shapes=[pltpu.VMEM((tmb, d), F32)],
    )
    return pl.pallas_call(
        _moe_kernel,
        grid_spec=grid_spec,
        out_shape=jax.ShapeDtypeStruct((n_slots, d), F32),
        compiler_params=_cparams(("parallel", "arbitrary")),
        name="moe_experts",
    )(block_expert, n_used, xs, wg, wu, wd)


def _res_ln_kernel(y_ref, m_ref, g_ref, b_ref, o_ref):
    o_ref[...] = _layer_norm(DN_ALPHA * y_ref[...] + m_ref[...], g_ref[...], b_ref[...])


def _res_ln(y, m, g, b, tm):
    n, d = y.shape
    row = pl.BlockSpec((tm, d), lambda i: (i, 0))
    vec = pl.BlockSpec((1, d), lambda i: (0, 0))
    return pl.pallas_call(
        _res_ln_kernel,
        grid=(pl.cdiv(n, tm),),
        in_specs=[row, row, vec, vec],
        out_specs=row,
        out_shape=jax.ShapeDtypeStruct((n, d), F32),
        compiler_params=_cparams(("parallel",)),
        name="res_ln",
    )(y, m, g.reshape(1, d), b.reshape(1, d))


def _moe(y, router, w_gate, w_up, w_down, g, b, tm, tmb, tf):
    n, d = y.shape
    idx_p, gate_p = _router(y, router, tm)
    top_idx = idx_p[:, :2]
    gates = gate_p[:, :2]
    e_flat = top_idx.reshape(-1)
    t_flat = jnp.repeat(jnp.arange(n, dtype=I32), 2)
    g_flat = gates.reshape(-1)
    n_assign = 2 * n
    counts = jnp.bincount(e_flat, length=N_EXPERTS)
    padded = (counts + tmb - 1) // tmb * tmb
    pad_end = jnp.cumsum(padded)
    order = jnp.argsort(e_flat)
    e_sorted = e_flat[order]
    rank = jnp.arange(n_assign, dtype=I32) - (jnp.cumsum(counts) - counts)[e_sorted]
    slot = (pad_end - padded)[e_sorted] + rank
    n_blocks = -(-n_assign // tmb) + N_EXPERTS
    n_slots = n_blocks * tmb
    slot_tok = jnp.full((n_slots,), n, I32).at[slot].set(t_flat[order])
    slot_gate = jnp.zeros((n_slots,), F32).at[slot].set(g_flat[order])
    block_expert = jnp.minimum(
        jnp.searchsorted(pad_end, jnp.arange(n_blocks, dtype=I32) * tmb, side="right"),
        N_EXPERTS - 1).astype(I32)
    n_used = (pad_end[-1] // tmb).astype(I32).reshape(1)
    x_pad = jnp.concatenate([y, jnp.zeros((1, d), y.dtype)], axis=0)
    xs = x_pad[slot_tok].astype(BF16)
    outs = _moe_experts(block_expert, n_used, xs, w_gate.astype(BF16), w_up.astype(BF16),
                        w_down.astype(BF16), tmb, tf)
    mo = jax.ops.segment_sum(outs * slot_gate[:, None], slot_tok, num_segments=n + 1)[:n]
    return _res_ln(y, mo, g, b, tm)


def _t5_bucket(rel):
    half = NUM_BUCKETS // 2
    exact = half // 2
    n = jnp.abs(rel)
    far = exact + sum((n >= t).astype(I32) for t in (12, 16, 23, 32, 46, 64, 91))
    return jnp.where(rel > 0, half, 0) + jnp.where(n < exact, n, far)


def _bias_tile(rel_bias, n_groups, per_group, tq, ck, d0):
    d = d0 + jnp.arange(ck, dtype=I32)[None, :] - jnp.arange(tq, dtype=I32)[:, None]
    tile = rel_bias.astype(F32)[_t5_bucket(d)]
    tile = jnp.transpose(tile[:, :, :n_groups * per_group], (2, 0, 1))
    return tile.reshape(n_groups, per_group * tq, ck)


def _cols(w, ranges):
    parts = []
    for r in ranges:
        if isinstance(r, int):
            parts.append(jnp.zeros((w.shape[0], r), w.dtype))
        else:
            parts.append(w[:, r[0]:r[1]])
    return jnp.concatenate(parts, axis=1)


def _head_pair_order(n_heads):
    half = n_heads // 2
    order = []
    for j in range(half):
        order += [j, half + j]
    return order


L0_SPECS = ((0, 512, ("bf16s",)), (512, 128, ("f32", "bf16")), (640, 128, ("f32", "bf16")),
            (768, 512, ("hilo",)), (1280, 128, ("f32", "bf16")), (1408, 128, ("wi",)),
            (1536, 512, ("bf16s",)), (2048, 512, ("f32", "bf16")), (2560, 512, ("f32", "bf16")))
L1_SPECS = ((0, 1024, ("bf16s",)), (1024, 128, ("f32", "bf16")), (1152, 128, ("f32", "bf16")))


def _l0_weight(w_in):
    hd = HEAD_DIM
    rng = [(hd * h, hd * h + hd) for h in _head_pair_order(A_HEADS)]
    rng += [(512, 640), (640, 768)]
    for h in range(IDX_HEADS):
        rng += [(768 + hd * h, 768 + hd * h + hd)] * 2
    rng += [(1024, 1088)] * 2
    rng += [(1088, 1092), LANE - IDX_HEADS]
    rng += [(1092, 1604), (1604, 2116), (2116, 2628)]
    return _cols(w_in, rng).astype(BF16)


def _l1_weight(w_in):
    hd = HEAD_DIM
    rng = [(hd * h, hd * h + hd) for h in _head_pair_order(C_HEADS)]
    rng += [(1024, 1152), (1152, 1280)]
    return _cols(w_in, rng).astype(BF16)


def _perm_rows(w, n_heads):
    return jnp.concatenate([w[HEAD_DIM * h:HEAD_DIM * (h + 1)] for h in _head_pair_order(n_heads)], axis=0)


def _front_pad(a, nb, rows_in, front, rows_out):
    a = a.reshape(nb, rows_in, a.shape[-1])
    return jnp.pad(a, ((0, 0), (front, rows_out - front - rows_in), (0, 0)))


def _layer0_attention(x_prompt, x_sample, cache_a_k, cache_a_v, cache_a_idx_k, cache_b_k, cache_b_v,
                      meta_tokens, rel_bias, l0_w_in):
    nb, seq, d = x_prompt.shape
    t = N_META + seq
    nq = -(-t // QBLK)
    tp = nq * QBLK
    db, ds, _ = x_sample.shape
    past = cache_a_k.shape[1]
    assert d == D_MODEL and past % SB_TK == 0 and ds % 16 == 0 and ds <= 64
    np_, ns = nb * tp, db * ds
    tm = 512

    meta = jnp.broadcast_to(meta_tokens[None].astype(x_prompt.dtype), (nb, N_META, d))
    hp = jnp.concatenate([meta, x_prompt, jnp.zeros((nb, tp - t, d), x_prompt.dtype)], axis=1)
    x_all = jnp.concatenate([hp.reshape(np_, d), x_sample.reshape(ns, d)], axis=0)
    rel_bias = rel_bias.astype(F32)
    rb_far = rel_bias[NUM_BUCKETS // 2 - 1]

    (qa16, ka32, ka16, va32, va16, qi16, ki32, ki16, wi32, qb16, kb32, kb16, vb32, vb16) = _project(
        x_all, _l0_weight(l0_w_in), L0_SPECS, tm)

    n_sel_p = min(TOPK_MAX, (t - N_META) // 4)
    kr = QBLK * (nq + 1) + DSA_PADF
    ncm = -(-(QBLK * (nq + 1)) // DSA_CK)
    pk = [_front_pad(a[:np_], nb, tp, DSA_PADF, kr) for a in (ki16, ka16, va16)]
    bn_p = _bias_tile(rel_bias, A_KV_HEADS, A_HEADS // A_KV_HEADS, QBLK, DSA_CK, -(DSA_CK // 2))
    oa_p = _dsa(rb_far, qi16, wi32, qa16, *pk, bn_p, nb=nb, nq=nq, tq=QBLK, row0=0, n_sel=n_sel_p,
                e_base=2 * QBLK, e_step=QBLK, q_base=0, q_step=QBLK, coff=PROMPT_COFF,
                n_valid=t, padf=DSA_PADF, n_chunks_max=ncm)

    n_keys = past + ds
    n_sel_s = min(TOPK_MAX, n_keys // 4)
    e_s = -(-n_keys // LANE) * LANE
    ncs = -(-e_s // DSA_CK)
    krs = ncs * DSA_CK
    fs = krs - e_s

    def cat_keys(cache, new, width_dup):
        c = cache.reshape(db, past, -1).astype(BF16)
        if width_dup:
            c = jnp.concatenate([c, c], axis=-1)
        a = jnp.concatenate([c, new[np_:].reshape(db, ds, -1)], axis=1)
        return jnp.pad(a, ((0, 0), (fs, krs - fs - n_keys), (0, 0)))

    sk = [cat_keys(cache_a_idx_k, ki16, True), cat_keys(cache_a_k, ka16, False),
          cat_keys(cache_a_v, va16, False)]
    bn_s = _bias_tile(rel_bias, A_KV_HEADS, A_HEADS // A_KV_HEADS, ds, DSA_CK, e_s - DSA_CK - past)
    oa_s = _dsa(rb_far, qi16, wi32, qa16, *sk, bn_s, nb=db, nq=1, tq=ds, row0=np_, n_sel=n_sel_s,
                e_base=e_s, e_step=0, q_base=past, q_step=0, coff=0, n_valid=n_keys, padf=fs,
                n_chunks_max=ncs)

    kb_p = kb16[:np_].reshape(nb, tp, -1)
    vb_p = vb16[:np_].reshape(nb, tp, -1)
    ob_p = _stick_break(qb16, kb_p, vb_p, nb=nb, nq=nq, tq=QBLK, row0=0, e_base=QBLK, e_step=QBLK,
                        q_base=0, q_step=QBLK, padf=0, n_valid=t)
    rows_s = past + SB_TK

    def cat_b(cache, new):
        a = jnp.concatenate([cache.reshape(db, past, -1).astype(BF16), new[np_:].reshape(db, ds, -1)], axis=1)
        return jnp.pad(a, ((0, 0), (0, rows_s - n_keys), (0, 0)))

    ob_s = _stick_break(qb16, cat_b(cache_b_k, kb16), cat_b(cache_b_v, vb16), nb=db, nq=1, tq=ds,
                        row0=np_, e_base=rows_s, e_step=0, q_base=past, q_step=0, padf=0, n_valid=n_keys)

    oa = jnp.concatenate([oa_p, oa_s], axis=0)
    ob = jnp.concatenate([ob_p, ob_s], axis=0)
    return dict(x_all=x_all, oa=oa, ob=ob, np=np_, tp=tp, t=t, nq=nq, rel_bias=rel_bias,
                ka32=ka32, va32=va32, ki32=ki32, kb32=kb32, vb32=vb32)


def kernel(x_prompt, x_sample, cache_a_k, cache_a_v, cache_a_idx_k, cache_b_k, cache_b_v, cache_c_k, cache_c_v, meta_tokens, rel_bias, l0_w_in, l0_w_out, l0_ln1_g, l0_ln1_b, l0_w_gate, l0_w_up, l0_w_down, l0_ln2_g, l0_ln2_b, l1_w_in, l1_sinks, l1_w_out, l1_ln1_g, l1_ln1_b, l1_router, l1_w_gate, l1_w_up, l1_w_down, l1_ln2_g, l1_ln2_b):
    a0 = _layer0_attention(x_prompt, x_sample, cache_a_k, cache_a_v, cache_a_idx_k, cache_b_k, cache_b_v,
                           meta_tokens, rel_bias, l0_w_in)
    x_all, oa, ob, np_, tp, t, nq, rel_bias = (a0[k] for k in ("x_all", "oa", "ob", "np", "tp", "t", "nq", "rel_bias"))
    ka32, va32, ki32, kb32, vb32 = (a0[k] for k in ("ka32", "va32", "ki32", "kb32", "vb32"))
    nb, _, d = x_prompt.shape
    db, ds, _ = x_sample.shape
    past = cache_a_k.shape[1]
    tm = 512
    w_out0 = l0_w_out.astype(BF16)
    half0 = A_HEADS * HEAD_DIM
    y0 = _mix_ln(x_all, [oa, ob], [_perm_rows(w_out0[:half0], A_HEADS), w_out0[half0:]],
                 l0_ln1_g, l0_ln1_b, tm)
    h1 = _ffn_ln(y0, l0_w_gate, l0_w_up, l0_w_down, l0_ln2_g, l0_ln2_b, tm, 256)

    qc16, kc32, kc16, vc32, vc16 = _project(h1, _l1_weight(l1_w_in), L1_SPECS, tm)
    ckp = 4 * QBLK
    krc = QBLK * (nq - 1) + ckp
    kc_p = _front_pad(kc16[:np_], nb, tp, 2 * QBLK, max(krc, tp + 2 * QBLK))
    vc_p = _front_pad(vc16[:np_], nb, tp, 2 * QBLK, max(krc, tp + 2 * QBLK))
    bc_p = _bias_tile(rel_bias, C_KV_HEADS, C_HEADS // C_KV_HEADS, QBLK, ckp, -2 * QBLK)
    sinks = l1_sinks.astype(F32)
    oc_p = _swa(sinks, qc16, kc_p, vc_p, bc_p, nb=nb, nq=nq, tq=QBLK, row0=0, ck=ckp, r_base=0,
                r_step=QBLK, p_base=-2 * QBLK, p_step=QBLK, q_base=0, q_step=QBLK, coff=PROMPT_COFF, n_valid=t)

    buf = cache_c_k.shape[1]
    cks = -(-(buf + ds) // LANE) * LANE

    def cat_c(cache, new):
        a = jnp.concatenate([cache.reshape(db, buf, -1).astype(BF16), new[np_:].reshape(db, ds, -1)], axis=1)
        return jnp.pad(a, ((0, 0), (0, cks - buf - ds), (0, 0)))

    bc_s = _bias_tile(rel_bias, C_KV_HEADS, C_HEADS // C_KV_HEADS, ds, cks, -buf)
    oc_s = _swa(sinks, qc16, cat_c(cache_c_k, kc16), cat_c(cache_c_v, vc16), bc_s, nb=db, nq=1, tq=ds,
                row0=np_, ck=cks, r_base=0, r_step=0, p_base=past - buf, p_step=0, q_base=past,
                q_step=0, coff=0, n_valid=past + ds)
    oc = jnp.concatenate([oc_p, oc_s], axis=0)
    y1 = _mix_ln(h1, [oc], [_perm_rows(l1_w_out.astype(BF16), C_HEADS)], l1_ln1_g, l1_ln1_b, tm)
    h2 = _moe(y1, l1_router, l1_w_gate, l1_w_up, l1_w_down, l1_ln2_g, l1_ln2_b, tm, 512, 512)

    def pr(a, heads):
        a = a[:np_].reshape(nb, tp, -1)[:, :t]
        return a.reshape(nb, t, heads, HEAD_DIM) if heads else a[..., :HEAD_DIM]

    def sm(a, heads):
        a = a[np_:].reshape(db, ds, -1)
        return a.reshape(db, ds, heads, HEAD_DIM) if heads else a[..., :HEAD_DIM]

    y_prompt = h2[:np_].reshape(nb, tp, d)[:, N_META:t]
    y_sample = h2[np_:].reshape(db, ds, d)
    bufp = min(WINDOW, t)
    p_ck = pr(kc32, C_KV_HEADS)[:, t - bufp:]
    p_cv = pr(vc32, C_KV_HEADS)[:, t - bufp:]
    s_ck = jnp.concatenate([cache_c_k, sm(kc32, C_KV_HEADS)], axis=1)[:, ds:]
    s_cv = jnp.concatenate([cache_c_v, sm(vc32, C_KV_HEADS)], axis=1)[:, ds:]
    return (y_prompt, y_sample,
            pr(ka32, A_KV_HEADS), pr(va32, A_KV_HEADS), pr(ki32, 0), pr(kb32, B_HEADS), pr(vb32, B_HEADS),
            p_ck, p_cv,
            sm(ka32, A_KV_HEADS), sm(va32, A_KV_HEADS), sm(ki32, 0), sm(kb32, B_HEADS), sm(vb32, B_HEADS),
            s_ck, s_cv)
```

```python
import functools

import jax
import jax.numpy as jnp
from jax import lax
from jax.experimental import pallas as pl
from jax.experimental.pallas import tpu as pltpu

F32 = jnp.float32
BF16 = jnp.bfloat16
I32 = jnp.int32

D_MODEL = 1024
CHUNK_SHIFT = 6
N_META = 16
HEAD_DIM = 64
A_HEADS = 8
A_KV_HEADS = 2
IDX_HEADS = 4
TOPK_MAX = 256
B_HEADS = 8
C_HEADS = 16
C_KV_HEADS = 2
WINDOW = 128
WIN_CHUNKS = 2
NUM_BUCKETS = 32
N_EXPERTS = 8
LN_EPS = 1e-5
DEPTH = 2
DN_ALPHA = (2.0 * DEPTH) ** 0.25

LANE = 128
QBLK = 128
DSA_CK = 512
DSA_PADF = 384
DSA_RB = 32
LOG2E = 1.4426950408889634
SB_BAND = 512
SB_STOP = -120.0
NEG = -1e30
PROMPT_COFF = 64 - N_META
INT_MIN = -2 ** 31
VMEM_LIMIT = 56 * 1024 * 1024


def _cparams(sem):
    return pltpu.CompilerParams(dimension_semantics=sem, vmem_limit_bytes=VMEM_LIMIT)


def _nt(a, b):
    return lax.dot_general(a, b, (((1,), (1,)), ((), ())), preferred_element_type=F32)


def _layer_norm(v, g, b):
    mu = jnp.mean(v, axis=-1, keepdims=True)
    c = v - mu
    var = jnp.mean(c * c, axis=-1, keepdims=True)
    return c * lax.rsqrt(var + LN_EPS) * g + b


def _proj_kernel(x_ref, w_ref, *out_refs, specs):
    xb = x_ref[...].astype(BF16)
    k = 0
    for c0, width, kinds in specs:
        acc = jnp.dot(xb, w_ref[:, c0:c0 + width], preferred_element_type=F32)
        for kind in kinds:
            if kind == "f32":
                val = acc
            elif kind == "bf16":
                val = acc.astype(BF16)
            elif kind == "bf16s":
                val = (acc * 0.125).astype(BF16)
            elif kind == "bf16e":
                val = (acc * (0.125 * LOG2E)).astype(BF16)
            elif kind == "hilo":
                hi = acc.astype(BF16)
                lo = (acc - hi.astype(F32)).astype(BF16)
                lane = lax.broadcasted_iota(I32, acc.shape, 1) & (LANE - 1)
                val = jnp.where(lane < HEAD_DIM, hi, lo)
            elif kind == "wi":
                val = acc * 0.0625
            else:
                raise ValueError(kind)
            out_refs[k][...] = val
            k += 1


def _project(x, w16, specs, tm):
    n = x.shape[0]
    out_shape, out_specs = [], []
    for _, width, kinds in specs:
        for kind in kinds:
            dt = F32 if kind in ("f32", "wi") else BF16
            out_shape.append(jax.ShapeDtypeStruct((n, width), dt))
            out_specs.append(pl.BlockSpec((tm, width), lambda i: (i, 0)))
    return pl.pallas_call(
        functools.partial(_proj_kernel, specs=specs),
        grid=(pl.cdiv(n, tm),),
        in_specs=[pl.BlockSpec((tm, x.shape[1]), lambda i: (i, 0)),
                  pl.BlockSpec(w16.shape, lambda i: (0, 0))],
        out_specs=out_specs,
        out_shape=out_shape,
        compiler_params=_cparams(("parallel",)),
        name="proj",
    )(x, w16)


def _mix_ln_kernel(x_ref, *refs, n_pairs):
    o_refs = refs[:n_pairs]
    w_refs = refs[n_pairs:2 * n_pairs]
    g_ref, b_ref, y_ref = refs[2 * n_pairs:]
    acc = DN_ALPHA * x_ref[...]
    for o_ref, w_ref in zip(o_refs, w_refs):
        acc = acc + jnp.dot(o_ref[...], w_ref[...], preferred_element_type=F32)
    y_ref[...] = _layer_norm(acc, g_ref[...], b_ref[...])


def _mix_ln(x, os_, ws, g, b, tm):
    n, d = x.shape
    in_specs = [pl.BlockSpec((tm, d), lambda i: (i, 0))]
    in_specs += [pl.BlockSpec((tm, o.shape[1]), lambda i: (i, 0)) for o in os_]
    in_specs += [pl.BlockSpec(w.shape, lambda i: (0, 0)) for w in ws]
    in_specs += [pl.BlockSpec((1, d), lambda i: (0, 0))] * 2
    return pl.pallas_call(
        functools.partial(_mix_ln_kernel, n_pairs=len(os_)),
        grid=(pl.cdiv(n, tm),),
        in_specs=in_specs,
        out_specs=pl.BlockSpec((tm, d), lambda i: (i, 0)),
        out_shape=jax.ShapeDtypeStruct((n, d), F32),
        compiler_params=_cparams(("parallel",)),
        name="mix_ln",
    )(x, *os_, *ws, g.reshape(1, d), b.reshape(1, d))


def _ffn_ln_kernel(y_ref, wg_ref, wu_ref, wd_ref, g_ref, b_ref, o_ref, *, nf):
    y = y_ref[...]
    yb = y.astype(BF16)
    acc = DN_ALPHA * y
    for f in range(nf):
        hg = jnp.dot(yb, wg_ref[f], preferred_element_type=F32)
        hu = jnp.dot(yb, wu_ref[f], preferred_element_type=F32)
        h = (hg * jax.nn.sigmoid(hg) * hu).astype(BF16)
        acc = acc + jnp.dot(h, wd_ref[f], preferred_element_type=F32)
    o_ref[...] = _layer_norm(acc, g_ref[...], b_ref[...])


def _ffn_ln(y, wg, wu, wd, g, b, tm, tf):
    n, d = y.shape
    dff = wg.shape[1]
    nf = dff // tf
    wg3 = wg.astype(BF16).reshape(d, nf, tf).transpose(1, 0, 2)
    wu3 = wu.astype(BF16).reshape(d, nf, tf).transpose(1, 0, 2)
    wd3 = wd.astype(BF16).reshape(nf, tf, d)
    full3 = lambda i: (0, 0, 0)
    return pl.pallas_call(
        functools.partial(_ffn_ln_kernel, nf=nf),
        grid=(pl.cdiv(n, tm),),
        in_specs=[pl.BlockSpec((tm, d), lambda i: (i, 0)),
                  pl.BlockSpec(wg3.shape, full3), pl.BlockSpec(wu3.shape, full3),
                  pl.BlockSpec(wd3.shape, full3),
                  pl.BlockSpec((1, d), lambda i: (0, 0)), pl.BlockSpec((1, d), lambda i: (0, 0))],
        out_specs=pl.BlockSpec((tm, d), lambda i: (i, 0)),
        out_shape=jax.ShapeDtypeStruct((n, d), F32),
        compiler_params=_cparams(("parallel",)),
        name="ffn_ln",
    )(y, wg3, wu3, wd3, g.reshape(1, d), b.reshape(1, d))


def _sort_key(x):
    bits = lax.bitcast_convert_type(x, I32)
    return bits ^ ((bits >> 31) & 0x7FFFFFFF)


def _dsa_kernel(rb_ref, qi_ref, wi_ref, qa_ref, ki_ref, ka_ref, va_ref, bn_ref, o_ref,
                keys_ref, cut_ref, s_ref, p_ref, madd_ref, wbc_ref, cmax_ref, mpart_ref, alpha_ref,
                l_ref, acc_ref, *, tq, n_sel, e_base, e_step, q_base, q_step, coff, n_valid, padf):
    i = pl.program_id(1)
    ck = DSA_CK
    rbk = DSA_RB
    nt = ck // LANE
    e_end = e_base + e_step * i
    n_chunks = (e_end + ck - 1) // ck
    qpos = q_base + q_step * i + lax.broadcasted_iota(I32, (tq, 1), 0)
    bound = jnp.minimum(((((qpos + coff) >> CHUNK_SHIFT) + 1) << CHUNK_SHIFT) - coff, n_valid)
    lane_rb = lax.broadcasted_iota(I32, (rbk, ck), 1)
    lane128 = lax.broadcasted_iota(I32, (tq, LANE), 1)

    def key_rows(c):
        kpos0 = e_end - ck * (c + 1)
        return kpos0, pl.multiple_of(jnp.maximum(kpos0 + padf, 0), LANE)

    def tiles(x):
        return [x[:, t * LANE:(t + 1) * LANE] for t in range(nt)]

    qstack = jnp.concatenate([qi_ref[:, h * LANE:(h + 1) * LANE] for h in range(IDX_HEADS)], axis=0)
    for h in range(IDX_HEADS):
        wbc_ref[h] = jnp.broadcast_to(wi_ref[:, h:h + 1], (tq, LANE))
    cmax_ref[...] = jnp.full((tq, ck), -jnp.inf, F32)

    def idx_dot(c, dst):
        _, r = key_rows(c)
        s_ref[dst] = _nt(qstack, ki_ref[pl.ds(r, ck), :])

    def idx_keys(c, src):
        kpos0, _ = key_rows(c)
        for b in range(tq // rbk):
            r0 = b * rbk
            sc = None
            for h in range(IDX_HEADS):
                sh = jnp.maximum(s_ref[src, h * tq + r0:h * tq + r0 + rbk, :], 0.0)
                term = jnp.concatenate([wbc_ref[h, r0:r0 + rbk, :]] * nt, axis=1) * sh
                sc = term if sc is None else sc + term
            kpos = kpos0 + lane_rb
            allowed = (kpos >= 0) & (kpos < bound[r0:r0 + rbk])
            keys_ref[c, r0:r0 + rbk, :] = jnp.where(allowed, _sort_key(sc), INT_MIN)
            cmax_ref[r0:r0 + rbk, :] = jnp.maximum(cmax_ref[r0:r0 + rbk, :], jnp.where(allowed, sc, -jnp.inf))

    idx_dot(0, 0)

    def p1(j, carry):
        c = 2 * j
        idx_dot(c + 1, 1)
        idx_keys(c, 0)
        idx_dot(c + 2, 0)
        idx_keys(c + 1, 1)
        return carry

    lax.fori_loop(0, (n_chunks + 1) // 2, p1, 0)

    def count_ge(cand):
        def body(c, acc):
            u = keys_ref[c]
            for t in range(nt):
                acc = acc + jnp.where(u[:, t * LANE:(t + 1) * LANE] >= cand, 1.0, 0.0)
            return acc
        acc = lax.fori_loop(0, n_chunks, body, jnp.zeros((tq, LANE), F32))
        return jnp.sum(acc, axis=1, keepdims=True)

    cm = tiles(cmax_ref[...])
    if n_sel <= LANE:
        fold = [functools.reduce(jnp.maximum, cm)]
    elif n_sel <= 2 * LANE:
        fold = [jnp.maximum(cm[0], cm[2]), jnp.maximum(cm[1], cm[3])]
    else:
        fold = cm
    fmin = jnp.min(functools.reduce(jnp.minimum, fold), axis=1, keepdims=True)
    fmax = jnp.max(functools.reduce(jnp.maximum, cm), axis=1, keepdims=True)
    lo0 = _sort_key(fmin) - 1
    hi0 = _sort_key(fmax) + 2
    small = bound <= n_sel
    log_n = jnp.log(jnp.float32(n_sel))
    total = (n_chunks * ck).astype(F32)

    def s_cond(st):
        return (st[0] < 80) & (st[1] > 0)

    def s_body(st):
        k, _, lo, hi, c_lo, c_hi, done, thr, exc, need = st
        d = hi - lo
        mid = (lo & hi) + ((lo ^ hi) >> 1)
        frac = (jnp.log(c_lo) - log_n) / jnp.maximum(jnp.log(c_lo) - jnp.log(jnp.maximum(c_hi, 0.5)), 1e-6)
        step = (d.astype(F32) * jnp.clip(frac, 0.0, 1.0)).astype(I32)
        interp = lo + jnp.clip(step, 1, jnp.maximum(d - 1, 1))
        use_mid = (d < 0) | ((k & 1) == 0)
        cand = jnp.where(k == 0, lo0, jnp.where(use_mid, mid, interp))
        cnt = count_ge(cand)
        ge = cnt >= n_sel
        lo_n = jnp.where(ge, cand, lo)
        hi_n = jnp.where(ge, hi, cand)
        c_lo_n = jnp.where(ge, cnt, c_lo)
        c_hi_n = jnp.where(ge, c_hi, cnt)
        exact = cnt == n_sel
        conv = hi_n == lo_n + 1
        act = done == 0
        thr = jnp.where(act & exact, cand, jnp.where(act & conv, lo_n, thr))
        tie = act & conv & jnp.logical_not(exact)
        exc = jnp.where(tie & (c_lo_n > n_sel), 1, exc)
        need = jnp.where(tie, n_sel - c_hi_n, need)
        done_n = jnp.where(act & (exact | conv), 1, done)
        n_act = jnp.max(jnp.where(done_n == 0, 1.0, 0.0))
        return (k + 1, n_act, lo_n, hi_n, c_lo_n, c_hi_n, done_n, thr, exc, need)

    col_i = lambda v: jnp.full((tq, 1), v, I32)
    done0 = jnp.where(small, 1, 0).astype(I32)
    st = lax.while_loop(s_cond, s_body, (
        jnp.int32(0), jnp.max(jnp.where(small, 0.0, 1.0)), col_i(INT_MIN), hi0,
        jnp.broadcast_to(total, (tq, 1)), jnp.zeros((tq, 1), F32), done0,
        col_i(INT_MIN + 1), col_i(0), jnp.zeros((tq, 1), F32)))
    thr, exc, need = st[7], st[8], st[9]
    excess = exc > 0

    cut_ref[...] = jnp.full((tq, LANE), 2 ** 30, I32)

    @pl.when(jnp.max(jnp.where(excess, 1.0, 0.0)) > 0.0)
    def _():
        def count_eq_below(p):
            def body(c, acc):
                kpos0, _ = key_rows(c)
                u = keys_ref[c]
                for t in range(ck // LANE):
                    kp = kpos0 + t * LANE + lane128
                    hit = (u[:, t * LANE:(t + 1) * LANE] == thr) & (kp < p)
                    acc = acc + jnp.where(hit, 1.0, 0.0)
                return acc
            acc = lax.fori_loop(0, n_chunks, body, jnp.zeros((tq, LANE), F32))
            return jnp.sum(acc, axis=1, keepdims=True)

        def pb(it, p):
            cand = p + (jnp.int32(1) << (20 - it))
            return jnp.where(count_eq_below(cand) < need, cand, p)

        p = lax.fori_loop(0, 21, pb, jnp.zeros((tq, 1), I32))
        cut = jnp.where(excess, p + 1, 2 ** 30)
        cut_ref[...] = jnp.broadcast_to(cut, (tq, LANE))

    cut = cut_ref[:, 0:1]

    r4 = A_HEADS // A_KV_HEADS
    qg = []
    for g in range(A_KV_HEADS):
        rows = []
        for j in range(r4):
            slot = qa_ref[:, j * LANE:(j + 1) * LANE]
            half = (lane128 < HEAD_DIM) if g == 0 else (lane128 >= HEAD_DIM)
            rows.append(jnp.where(half, slot, jnp.zeros_like(slot)))
        qg.append(jnp.concatenate(rows, axis=0))
    farb = [jnp.concatenate([jnp.full((tq, 1), rb_ref[g * r4 + j], F32) for j in range(r4)], axis=0)
            for g in range(A_KV_HEADS)]
    l_ref[...] = jnp.zeros(l_ref.shape, F32)
    acc_ref[...] = jnp.zeros(acc_ref.shape, F32)

    def logits_dot(c, g):
        _, r = key_rows(c)
        s_ref[g] = _nt(qg[g], ka_ref[pl.ds(r, ck), :])

    def select_mask(c):
        kpos0, _ = key_rows(c)
        for b in range(tq // rbk):
            r0 = b * rbk
            u = keys_ref[c, r0:r0 + rbk, :]
            t_b = thr[r0:r0 + rbk]
            sel = (u > t_b) | ((u == t_b) & (kpos0 + lane_rb < cut[r0:r0 + rbk]))
            madd_ref[r0:r0 + rbk, :] = jnp.where(sel, 0.0, NEG)

    def softmax_passes(g, m_old, near):
        for b in range(r4 * tq // rbk):
            r0 = b * rbk
            q0 = r0 % tq
            sm = s_ref[g, r0:r0 + rbk, :] + madd_ref[q0:q0 + rbk, :]
            if near:
                sm = sm + bn_ref[g, r0:r0 + rbk, :]
            s_ref[g, r0:r0 + rbk, :] = sm
            mpart_ref[g, r0:r0 + rbk, :] = functools.reduce(jnp.maximum, tiles(sm))
        m_blk = jnp.max(mpart_ref[g], axis=1, keepdims=True)
        if not near:
            m_blk = m_blk + farb[g]
        m_new = jnp.maximum(m_old, m_blk)
        alpha = jnp.exp2(m_old - m_new)
        alpha_ref[g] = jnp.broadcast_to(alpha, (r4 * tq, LANE))
        shift = m_new if near else m_new - farb[g]
        for b in range(r4 * tq // rbk):
            r0 = b * rbk
            p = jnp.exp2(s_ref[g, r0:r0 + rbk, :] - shift[r0:r0 + rbk])
            l_ref[g, r0:r0 + rbk, :] = (alpha_ref[g, r0:r0 + rbk, :] * l_ref[g, r0:r0 + rbk, :]
                                        + functools.reduce(jnp.add, tiles(p)))
            p_ref[g, r0:r0 + rbk, :] = p.astype(BF16)
        return m_new

    def value_dot(c, g):
        _, r = key_rows(c)
        acc_ref[g] = alpha_ref[g] * acc_ref[g] + jnp.dot(p_ref[g], va_ref[pl.ds(r, ck), :],
                                                        preferred_element_type=F32)

    m0 = jnp.full((r4 * tq, 1), NEG, F32)
    logits_dot(0, 0)
    logits_dot(0, 1)
    select_mask(0)
    m0n = softmax_passes(0, m0, True)
    value_dot(0, 0)
    logits_dot(1, 0)
    m1n = softmax_passes(1, m0, True)

    def p3(c, ms):
        value_dot(c - 1, 1)
        logits_dot(c, 1)
        select_mask(c)
        m_a = softmax_passes(0, ms[0], False)
        value_dot(c, 0)
        logits_dot(c + 1, 0)
        m_b = softmax_passes(1, ms[1], False)
        return (m_a, m_b)

    lax.fori_loop(1, n_chunks, p3, (m0n, m1n))
    value_dot(n_chunks - 1, 1)
    outs = [acc_ref[g] / jnp.sum(l_ref[g], axis=1, keepdims=True) for g in range(A_KV_HEADS)]
    for j in range(r4):
        lo = outs[0][j * tq:(j + 1) * tq]
        hi = outs[1][j * tq:(j + 1) * tq]
        o_ref[:, j * LANE:(j + 1) * LANE] = jnp.where(lane128 < HEAD_DIM, lo, hi).astype(BF16)


def _dsa(rb_far, qi, wi, qa, ki, ka, va, bn, *, nb, nq, tq, row0, n_sel, e_base, e_step,
         q_base, q_step, coff, n_valid, padf, n_chunks_max):
    rb0 = row0 // tq
    r4 = A_HEADS // A_KV_HEADS
    assert IDX_HEADS == r4 and tq % DSA_RB == 0
    qmap = lambda b, i: (rb0 + b * nq + i, 0)
    kmap = lambda b, i: (b, 0, 0)
    kern = functools.partial(_dsa_kernel, tq=tq, n_sel=n_sel, e_base=e_base, e_step=e_step,
                             q_base=q_base, q_step=q_step, coff=coff, n_valid=n_valid, padf=padf)
    return pl.pallas_call(
        kern,
        grid=(nb, nq),
        in_specs=[pl.BlockSpec(memory_space=pltpu.SMEM),
                  pl.BlockSpec((tq, qi.shape[1]), qmap),
                  pl.BlockSpec((tq, LANE), qmap),
                  pl.BlockSpec((tq, qa.shape[1]), qmap),
                  pl.BlockSpec((None,) + ki.shape[1:], kmap),
                  pl.BlockSpec((None,) + ka.shape[1:], kmap),
                  pl.BlockSpec((None,) + va.shape[1:], kmap),
                  pl.BlockSpec(bn.shape, lambda b, i: (0, 0, 0))],
        out_specs=pl.BlockSpec((tq, qa.shape[1]), lambda b, i: (b * nq + i, 0)),
        out_shape=jax.ShapeDtypeStruct((nb * nq * tq, qa.shape[1]), BF16),
        scratch_shapes=[pltpu.VMEM((n_chunks_max + 1, tq, DSA_CK), I32),
                        pltpu.VMEM((tq, LANE), I32),
                        pltpu.VMEM((2, r4 * tq, DSA_CK), F32),
                        pltpu.VMEM((A_KV_HEADS, r4 * tq, DSA_CK), BF16),
                        pltpu.VMEM((tq, DSA_CK), F32),
                        pltpu.VMEM((IDX_HEADS, tq, LANE), F32),
                        pltpu.VMEM((tq, DSA_CK), F32),
                        pltpu.VMEM((A_KV_HEADS, r4 * tq, LANE), F32),
                        pltpu.VMEM((A_KV_HEADS, r4 * tq, LANE), F32),
                        pltpu.VMEM((A_KV_HEADS, r4 * tq, LANE), F32),
                        pltpu.VMEM((A_KV_HEADS, r4 * tq, LANE), F32)],
        compiler_params=_cparams(("parallel", "arbitrary")),
        name="dsa",
    )(rb_far, qi, wi, qa, ki, ka, va, bn)


def _sb_kernel(tri_ref, q_ref, k_ref, v_ref, o_ref, *, tq, e_base, e_step, q_base, q_step, padf, n_valid):
    i = pl.program_id(2)
    bw = SB_BAND
    e_end = e_base + e_step * i
    n_bands = (e_end + bw - 1) // bw
    qpos = q_base + q_step * i + lax.broadcasted_iota(I32, (tq, 1), 0)
    lane = lax.broadcasted_iota(I32, (tq, LANE), 1)
    q = q_ref[...]
    zero = jnp.zeros_like(q)
    qh = [jnp.where(lane < HEAD_DIM, q, zero), jnp.where(lane >= HEAD_DIM, q, zero)]
    lane_bw = lax.broadcasted_iota(I32, (tq, bw), 1)

    def cond(st):
        return (st[0] < n_bands) & (st[1] > SB_STOP)

    def body(st):
        m, _, carry, acc = st
        kpos0 = e_end - bw * (m + 1)
        r = pl.multiple_of(kpos0 + padf, LANE)
        kt = k_ref[pl.ds(r, bw), :]
        vt = v_ref[pl.ds(r, bw), :]
        kpos = kpos0 + lane_bw
        before = (kpos < qpos) & (kpos >= 0) & (kpos < n_valid)
        zs, lss, lks, parts = [], [], [], []
        for h in range(2):
            z = _nt(qh[h], kt)
            ls = -(jnp.maximum(z, 0.0) + jnp.log1p(jnp.exp(-jnp.abs(z))))
            lk = jnp.where(before, ls, 0.0)
            hi = lk.astype(BF16)
            parts += [hi, (lk - hi.astype(F32)).astype(BF16)]
            zs.append(z)
            lss.append(ls)
            lks.append(lk)
        sums = jnp.dot(jnp.concatenate(parts, axis=0), tri_ref[...], preferred_element_type=F32)
        new_c, new_a = [], []
        worst = jnp.float32(-jnp.inf)
        for h in range(2):
            bl = sums[2 * h * tq:(2 * h + 1) * tq] + sums[(2 * h + 1) * tq:(2 * h + 2) * tq]
            w = jnp.where(before, jnp.exp(lss[h] + zs[h] + bl + carry[h]), 0.0)
            new_a.append(acc[h] + jnp.dot(w.astype(BF16), vt, preferred_element_type=F32))
            c_n = carry[h] + bl[:, 0:1] + lks[h][:, 0:1]
            new_c.append(c_n)
            worst = jnp.maximum(worst, jnp.max(c_n))
        return (m + 1, worst, tuple(new_c), tuple(new_a))

    init = (jnp.int32(0), jnp.float32(0.0),
            (jnp.zeros((tq, 1), F32), jnp.zeros((tq, 1), F32)),
            (jnp.zeros((tq, LANE), F32), jnp.zeros((tq, LANE), F32)))
    _, _, _, acc = lax.while_loop(cond, body, init)
    o_ref[...] = jnp.where(lane < HEAD_DIM, acc[0], acc[1]).astype(BF16)


def _stick_break(q, k, v, *, nb, nq, tq, row0, e_base, e_step, q_base, q_step, padf, n_valid):
    rb0 = row0 // tq
    npair = q.shape[1] // LANE
    kern = functools.partial(_sb_kernel, tq=tq, e_base=e_base, e_step=e_step, q_base=q_base,
                             q_step=q_step, padf=padf, n_valid=n_valid)
    rr = lax.broadcasted_iota(I32, (SB_BAND, SB_BAND), 0)
    cc = lax.broadcasted_iota(I32, (SB_BAND, SB_BAND), 1)
    tri = jnp.where(rr > cc, 1.0, 0.0).astype(BF16)
    kspec = pl.BlockSpec((None, k.shape[1], LANE), lambda b, p, i: (b, 0, p))
    return pl.pallas_call(
        kern,
        grid=(nb, npair, nq),
        in_specs=[pl.BlockSpec(tri.shape, lambda b, p, i: (0, 0)),
                  pl.BlockSpec((tq, LANE), lambda b, p, i: (rb0 + b * nq + i, p)), kspec, kspec],
        out_specs=pl.BlockSpec((tq, LANE), lambda b, p, i: (b * nq + i, p)),
        out_shape=jax.ShapeDtypeStruct((nb * nq * tq, q.shape[1]), BF16),
        compiler_params=_cparams(("parallel", "parallel", "arbitrary")),
        name="stick_break",
    )(tri, q, k, v)


def _swa_kernel(sink_ref, q_ref, k_ref, v_ref, bias_ref, o_ref, *, tq, ck, r_base, r_step,
                p_base, p_step, q_base, q_step, coff, n_valid):
    i = pl.program_id(1)
    r0 = pl.multiple_of(r_base + r_step * i, 16)
    kpos0 = p_base + p_step * i
    kt = k_ref[pl.ds(r0, ck), :]
    vt = v_ref[pl.ds(r0, ck), :]
    qpos = q_base + q_step * i + lax.broadcasted_iota(I32, (tq, 1), 0)
    kpos = kpos0 + lax.broadcasted_iota(I32, (tq, ck), 1)
    qc = (qpos + coff) >> CHUNK_SHIFT
    hi_b = jnp.minimum(((qc + 1) << CHUNK_SHIFT) - coff, n_valid)
    lo_b = jnp.maximum(((qc - WIN_CHUNKS) << CHUNK_SHIFT) - coff, 0)
    allowed = (kpos >= lo_b) & (kpos < hi_b)
    r8 = C_HEADS // C_KV_HEADS
    allowed8 = jnp.concatenate([allowed] * r8, axis=0)
    lane = lax.broadcasted_iota(I32, (tq, LANE), 1)
    outs = []
    for g in range(C_KV_HEADS):
        rows = []
        for j in range(r8):
            slot = q_ref[:, j * LANE:(j + 1) * LANE]
            half = (lane < HEAD_DIM) if g == 0 else (lane >= HEAD_DIM)
            rows.append(jnp.where(half, slot, jnp.zeros_like(slot)))
        qs = jnp.concatenate(rows, axis=0)
        sink = jnp.concatenate([jnp.full((tq, 1), sink_ref[g * r8 + j], F32) for j in range(r8)], axis=0)
        s = _nt(qs, kt) + bias_ref[g]
        s = jnp.where(allowed8, s, NEG)
        m = jnp.maximum(jnp.max(s, axis=1, keepdims=True), sink)
        e = jnp.exp(s - m)
        den = jnp.sum(e, axis=1, keepdims=True) + jnp.exp(sink - m)
        p = (e / den).astype(BF16)
        outs.append(jnp.dot(p, vt, preferred_element_type=F32))
    for j in range(r8):
        lo = outs[0][j * tq:(j + 1) * tq]
        hi = outs[1][j * tq:(j + 1) * tq]
        o_ref[:, j * LANE:(j + 1) * LANE] = jnp.where(lane < HEAD_DIM, lo, hi).astype(BF16)


def _swa(sinks, q, k, v, bias, *, nb, nq, tq, row0, ck, r_base, r_step, p_base, p_step,
         q_base, q_step, coff, n_valid):
    rb0 = row0 // tq
    kern = functools.partial(_swa_kernel, tq=tq, ck=ck, r_base=r_base, r_step=r_step, p_base=p_base,
                             p_step=p_step, q_base=q_base, q_step=q_step, coff=coff, n_valid=n_valid)
    kmap = lambda b, i: (b, 0, 0)
    return pl.pallas_call(
        kern,
        grid=(nb, nq),
        in_specs=[pl.BlockSpec(memory_space=pltpu.SMEM),
                  pl.BlockSpec((tq, q.shape[1]), lambda b, i: (rb0 + b * nq + i, 0)),
                  pl.BlockSpec((None,) + k.shape[1:], kmap),
                  pl.BlockSpec((None,) + v.shape[1:], kmap),
                  pl.BlockSpec(bias.shape, lambda b, i: (0, 0, 0))],
        out_specs=pl.BlockSpec((tq, q.shape[1]), lambda b, i: (b * nq + i, 0)),
        out_shape=jax.ShapeDtypeStruct((nb * nq * tq, q.shape[1]), BF16),
        compiler_params=_cparams(("parallel", "arbitrary")),
        name="swa",
    )(sinks, q, k, v, bias)


def _router_kernel(y_ref, rh_ref, rl_ref, idx_ref, gate_ref):
    y = y_ref[...]
    yh = y.astype(BF16)
    yl = (y - yh.astype(F32)).astype(BF16)
    rh = rh_ref[...]
    logits = (jnp.dot(yh, rh, preferred_element_type=F32) + jnp.dot(yl, rh, preferred_element_type=F32)
              + jnp.dot(yh, rl_ref[...], preferred_element_type=F32))
    lane = lax.broadcasted_iota(I32, logits.shape, 1)
    logits = jnp.where(lane < N_EXPERTS, logits, -jnp.inf)
    m1 = jnp.max(logits, axis=1, keepdims=True)
    i1 = jnp.min(jnp.where(logits == m1, lane, LANE), axis=1, keepdims=True)
    rest = jnp.where(lane == i1, -jnp.inf, logits)
    m2 = jnp.max(rest, axis=1, keepdims=True)
    i2 = jnp.min(jnp.where(rest == m2, lane, LANE), axis=1, keepdims=True)
    e2 = jnp.exp(m2 - m1)
    den = 1.0 + e2
    idx_ref[...] = jnp.where(lane == 0, i1, jnp.where(lane == 1, i2, 0))
    gate_ref[...] = jnp.where(lane == 0, 1.0 / den, jnp.where(lane == 1, e2 / den, 0.0))


def _router(y, router, tm):
    n, d = y.shape
    rpad = jnp.pad(router.astype(F32), ((0, 0), (0, LANE - router.shape[1])))
    rh = rpad.astype(BF16)
    rl = (rpad - rh.astype(F32)).astype(BF16)
    return pl.pallas_call(
        _router_kernel,
        grid=(pl.cdiv(n, tm),),
        in_specs=[pl.BlockSpec((tm, d), lambda i: (i, 0)),
                  pl.BlockSpec((d, LANE), lambda i: (0, 0)), pl.BlockSpec((d, LANE), lambda i: (0, 0))],
        out_specs=[pl.BlockSpec((tm, LANE), lambda i: (i, 0)), pl.BlockSpec((tm, LANE), lambda i: (i, 0))],
        out_shape=[jax.ShapeDtypeStruct((n, LANE), I32), jax.ShapeDtypeStruct((n, LANE), F32)],
        compiler_params=_cparams(("parallel",)),
        name="router",
    )(y, rh, rl)


def _moe_kernel(be_ref, nu_ref, x_ref, wg_ref, wu_ref, wd_ref, o_ref, acc_ref):
    t = pl.program_id(0)
    f = pl.program_id(1)

    @pl.when(f == 0)
    def _():
        acc_ref[...] = jnp.zeros_like(acc_ref)

    @pl.when(t < nu_ref[0])
    def _():
        xb = x_ref[...]
        hg = jnp.dot(xb, wg_ref[...], preferred_element_type=F32)
        hu = jnp.dot(xb, wu_ref[...], preferred_element_type=F32)
        h = (hg * jax.nn.sigmoid(hg) * hu).astype(BF16)
        acc_ref[...] += jnp.dot(h, wd_ref[...], preferred_element_type=F32)

    @pl.when(f == pl.num_programs(1) - 1)
    def _():
        o_ref[...] = acc_ref[...]


def _moe_experts(block_expert, n_used, xs, wg, wu, wd, tmb, tf):
    n_slots, d = xs.shape
    de = wg.shape[2]
    grid_spec = pltpu.PrefetchScalarGridSpec(
        num_scalar_prefetch=2,
        grid=(n_slots // tmb, de // tf),
        in_specs=[pl.BlockSpec((tmb, d), lambda t, f, be, nu: (t, 0)),
                  pl.BlockSpec((None, d, tf), lambda t, f, be, nu: (be[t], 0, f)),
                  pl.BlockSpec((None, d, tf), lambda t, f, be, nu: (be[t], 0, f)),
                  pl.BlockSpec((None, tf, d), lambda t, f, be, nu: (be[t], f, 0))],
        out_specs=pl.BlockSpec((tmb, d), lambda t, f, be, nu: (t, 0)),
        scratch_shapes=[pltpu.VMEM((tmb, d), F32)],
    )
    return pl.pallas_call(
        _moe_kernel,
        grid_spec=grid_spec,
        out_shape=jax.ShapeDtypeStruct((n_slots, d), F32),
        compiler_params=_cparams(("parallel", "arbitrary")),
        name="moe_experts",
    )(block_expert, n_used, xs, wg, wu, wd)


def _res_ln_kernel(y_ref, m_ref, g_ref, b_ref, o_ref):
    o_ref[...] = _layer_norm(DN_ALPHA * y_ref[...] + m_ref[...], g_ref[...], b_ref[...])


def _res_ln(y, m, g, b, tm):
    n, d = y.shape
    row = pl.BlockSpec((tm, d), lambda i: (i, 0))
    vec = pl.BlockSpec((1, d), lambda i: (0, 0))
    return pl.pallas_call(
        _res_ln_kernel,
        grid=(pl.cdiv(n, tm),),
        in_specs=[row, row, vec, vec],
        out_specs=row,
        out_shape=jax.ShapeDtypeStruct((n, d), F32),
        compiler_params=_cparams(("parallel",)),
        name="res_ln",
    )(y, m, g.reshape(1, d), b.reshape(1, d))


def _moe(y, router, w_gate, w_up, w_down, g, b, tm, tmb, tf):
    n, d = y.shape
    idx_p, gate_p = _router(y, router, tm)
    top_idx = idx_p[:, :2]
    gates = gate_p[:, :2]
    e_flat = top_idx.reshape(-1)
    t_flat = jnp.repeat(jnp.arange(n, dtype=I32), 2)
    g_flat = gates.reshape(-1)
    n_assign = 2 * n
    counts = jnp.bincount(e_flat, length=N_EXPERTS)
    padded = (counts + tmb - 1) // tmb * tmb
    pad_end = jnp.cumsum(padded)
    order = jnp.argsort(e_flat)
    e_sorted = e_flat[order]
    rank = jnp.arange(n_assign, dtype=I32) - (jnp.cumsum(counts) - counts)[e_sorted]
    slot = (pad_end - padded)[e_sorted] + rank
    n_blocks = -(-n_assign // tmb) + N_EXPERTS
    n_slots = n_blocks * tmb
    slot_tok = jnp.full((n_slots,), n, I32).at[slot].set(t_flat[order])
    slot_gate = jnp.zeros((n_slots,), F32).at[slot].set(g_flat[order])
    block_expert = jnp.minimum(
        jnp.searchsorted(pad_end, jnp.arange(n_blocks, dtype=I32) * tmb, side="right"),
        N_EXPERTS - 1).astype(I32)
    n_used = (pad_end[-1] // tmb).astype(I32).reshape(1)
    x_pad = jnp.concatenate([y, jnp.zeros((1, d), y.dtype)], axis=0)
    xs = x_pad[slot_tok].astype(BF16)
    outs = _moe_experts(block_expert, n_used, xs, w_gate.astype(BF16), w_up.astype(BF16),
                        w_down.astype(BF16), tmb, tf)
    mo = jax.ops.segment_sum(outs * slot_gate[:, None], slot_tok, num_segments=n + 1)[:n]
    return _res_ln(y, mo, g, b, tm)


def _t5_bucket(rel):
    half = NUM_BUCKETS // 2
    exact = half // 2
    n = jnp.abs(rel)
    far = exact + sum((n >= t).astype(I32) for t in (12, 16, 23, 32, 46, 64, 91))
    return jnp.where(rel > 0, half, 0) + jnp.where(n < exact, n, far)


def _bias_tile(rel_bias, n_groups, per_group, tq, ck, d0):
    d = d0 + jnp.arange(ck, dtype=I32)[None, :] - jnp.arange(tq, dtype=I32)[:, None]
    tile = rel_bias.astype(F32)[_t5_bucket(d)]
    tile = jnp.transpose(tile[:, :, :n_groups * per_group], (2, 0, 1))
    return tile.reshape(n_groups, per_group * tq, ck)


def _cols(w, ranges):
    parts = []
    for r in ranges:
        if isinstance(r, int):
            parts.append(jnp.zeros((w.shape[0], r), w.dtype))
        else:
            parts.append(w[:, r[0]:r[1]])
    return jnp.concatenate(parts, axis=1)


def _head_pair_order(n_heads):
    half = n_heads // 2
    order = []
    for j in range(half):
        order += [j, half + j]
    return order


L0_SPECS = ((0, 512, ("bf16e",)), (512, 128, ("f32", "bf16")), (640, 128, ("f32", "bf16")),
            (768, 512, ("hilo",)), (1280, 128, ("f32", "bf16")), (1408, 128, ("wi",)),
            (1536, 512, ("bf16s",)), (2048, 512, ("f32", "bf16")), (2560, 512, ("f32", "bf16")))
L1_SPECS = ((0, 1024, ("bf16s",)), (1024, 128, ("f32", "bf16")), (1152, 128, ("f32", "bf16")))


def _l0_weight(w_in):
    hd = HEAD_DIM
    rng = [(hd * h, hd * h + hd) for h in _head_pair_order(A_HEADS)]
    rng += [(512, 640), (640, 768)]
    for h in range(IDX_HEADS):
        rng += [(768 + hd * h, 768 + hd * h + hd)] * 2
    rng += [(1024, 1088)] * 2
    rng += [(1088, 1092), LANE - IDX_HEADS]
    rng += [(1092, 1604), (1604, 2116), (2116, 2628)]
    return _cols(w_in, rng).astype(BF16)


def _l1_weight(w_in):
    hd = HEAD_DIM
    rng = [(hd * h, hd * h + hd) for h in _head_pair_order(C_HEADS)]
    rng += [(1024, 1152), (1152, 1280)]
    return _cols(w_in, rng).astype(BF16)


def _perm_rows(w, n_heads):
    return jnp.concatenate([w[HEAD_DIM * h:HEAD_DIM * (h + 1)] for h in _head_pair_order(n_heads)], axis=0)


def _front_pad(a, nb, rows_in, front, rows_out):
    a = a.reshape(nb, rows_in, a.shape[-1])
    return jnp.pad(a, ((0, 0), (front, rows_out - front - rows_in), (0, 0)))


def _layer0_attention(x_prompt, x_sample, cache_a_k, cache_a_v, cache_a_idx_k, cache_b_k, cache_b_v,
                      meta_tokens, rel_bias, l0_w_in):
    nb, seq, d = x_prompt.shape
    t = N_META + seq
    nq = -(-t // QBLK)
    tp = nq * QBLK
    db, ds, _ = x_sample.shape
    past = cache_a_k.shape[1]
    assert d == D_MODEL and past % LANE == 0 and ds % DSA_RB == 0 and ds <= 64
    np_, ns = nb * tp, db * ds
    tm = 512

    meta = jnp.broadcast_to(meta_tokens[None].astype(x_prompt.dtype), (nb, N_META, d))
    hp = jnp.concatenate([meta, x_prompt, jnp.zeros((nb, tp - t, d), x_prompt.dtype)], axis=1)
    x_all = jnp.concatenate([hp.reshape(np_, d), x_sample.reshape(ns, d)], axis=0)
    rel_bias = rel_bias.astype(F32)
    rb_far = rel_bias[NUM_BUCKETS // 2 - 1] * LOG2E

    (qa16, ka32, ka16, va32, va16, qi16, ki32, ki16, wi32, qb16, kb32, kb16, vb32, vb16) = _project(
        x_all, _l0_weight(l0_w_in), L0_SPECS, tm)

    n_sel_p = min(TOPK_MAX, (t - N_META) // 4)
    kr = QBLK * (nq + 1) + DSA_PADF
    ncm = -(-(QBLK * (nq + 1)) // DSA_CK)
    pk = [_front_pad(a[:np_], nb, tp, DSA_PADF, kr) for a in (ki16, ka16, va16)]
    bn_p = _bias_tile(rel_bias, A_KV_HEADS, A_HEADS // A_KV_HEADS, QBLK, DSA_CK, -(DSA_CK // 2)) * LOG2E
    oa_p = _dsa(rb_far, qi16, wi32, qa16, *pk, bn_p, nb=nb, nq=nq, tq=QBLK, row0=0, n_sel=n_sel_p,
                e_base=2 * QBLK, e_step=QBLK, q_base=0, q_step=QBLK, coff=PROMPT_COFF,
                n_valid=t, padf=DSA_PADF, n_chunks_max=ncm)

    n_keys = past + ds
    n_sel_s = min(TOPK_MAX, n_keys // 4)
    e_s = -(-n_keys // LANE) * LANE
    ncs = -(-e_s // DSA_CK)
    krs = ncs * DSA_CK
    fs = krs - e_s

    def cat_keys(cache, new, width_dup):
        c = cache.reshape(db, past, -1).astype(BF16)
        if width_dup:
            c = jnp.concatenate([c, c], axis=-1)
        a = jnp.concatenate([c, new[np_:].reshape(db, ds, -1)], axis=1)
        return jnp.pad(a, ((0, 0), (fs, krs - fs - n_keys), (0, 0)))

    sk = [cat_keys(cache_a_idx_k, ki16, True), cat_keys(cache_a_k, ka16, False),
          cat_keys(cache_a_v, va16, False)]
    bn_s = _bias_tile(rel_bias, A_KV_HEADS, A_HEADS // A_KV_HEADS, ds, DSA_CK, e_s - DSA_CK - past) * LOG2E
    oa_s = _dsa(rb_far, qi16, wi32, qa16, *sk, bn_s, nb=db, nq=1, tq=ds, row0=np_, n_sel=n_sel_s,
                e_base=e_s, e_step=0, q_base=past, q_step=0, coff=0, n_valid=n_keys, padf=fs,
                n_chunks_max=ncs)

    pfb = SB_BAND - QBLK
    kb_p = _front_pad(kb16[:np_], nb, tp, pfb, tp + pfb)
    vb_p = _front_pad(vb16[:np_], nb, tp, pfb, tp + pfb)
    ob_p = _stick_break(qb16, kb_p, vb_p, nb=nb, nq=nq, tq=QBLK, row0=0, e_base=QBLK, e_step=QBLK,
                        q_base=0, q_step=QBLK, padf=pfb, n_valid=t)
    rows_s = -(-e_s // SB_BAND) * SB_BAND
    pfs = rows_s - e_s

    def cat_b(cache, new):
        a = jnp.concatenate([cache.reshape(db, past, -1).astype(BF16), new[np_:].reshape(db, ds, -1)], axis=1)
        return jnp.pad(a, ((0, 0), (pfs, rows_s - pfs - n_keys), (0, 0)))

    ob_s = _stick_break(qb16, cat_b(cache_b_k, kb16), cat_b(cache_b_v, vb16), nb=db, nq=1, tq=ds,
                        row0=np_, e_base=e_s, e_step=0, q_base=past, q_step=0, padf=pfs, n_valid=n_keys)

    oa = jnp.concatenate([oa_p, oa_s], axis=0)
    ob = jnp.concatenate([ob_p, ob_s], axis=0)
    return dict(x_all=x_all, oa=oa, ob=ob, np=np_, tp=tp, t=t, nq=nq, rel_bias=rel_bias,
                ka32=ka32, va32=va32, ki32=ki32, kb32=kb32, vb32=vb32)


def kernel(x_prompt, x_sample, cache_a_k, cache_a_v, cache_a_idx_k, cache_b_k, cache_b_v, cache_c_k, cache_c_v, meta_tokens, rel_bias, l0_w_in, l0_w_out, l0_ln1_g, l0_ln1_b, l0_w_gate, l0_w_up, l0_w_down, l0_ln2_g, l0_ln2_b, l1_w_in, l1_sinks, l1_w_out, l1_ln1_g, l1_ln1_b, l1_router, l1_w_gate, l1_w_up, l1_w_down, l1_ln2_g, l1_ln2_b):
    a0 = _layer0_attention(x_prompt, x_sample, cache_a_k, cache_a_v, cache_a_idx_k, cache_b_k, cache_b_v,
                           meta_tokens, rel_bias, l0_w_in)
    x_all, oa, ob, np_, tp, t, nq, rel_bias = (a0[k] for k in ("x_all", "oa", "ob", "np", "tp", "t", "nq", "rel_bias"))
    ka32, va32, ki32, kb32, vb32 = (a0[k] for k in ("ka32", "va32", "ki32", "kb32", "vb32"))
    nb, _, d = x_prompt.shape
    db, ds, _ = x_sample.shape
    past = cache_a_k.shape[1]
    tm = 512
    w_out0 = l0_w_out.astype(BF16)
    half0 = A_HEADS * HEAD_DIM
    y0 = _mix_ln(x_all, [oa, ob], [_perm_rows(w_out0[:half0], A_HEADS), w_out0[half0:]],
                 l0_ln1_g, l0_ln1_b, tm)
    h1 = _ffn_ln(y0, l0_w_gate, l0_w_up, l0_w_down, l0_ln2_g, l0_ln2_b, tm, 256)

    qc16, kc32, kc16, vc32, vc16 = _project(h1, _l1_weight(l1_w_in), L1_SPECS, tm)
    ckp = 4 * QBLK
    krc = QBLK * (nq - 1) + ckp
    kc_p = _front_pad(kc16[:np_], nb, tp, 2 * QBLK, max(krc, tp + 2 * QBLK))
    vc_p = _front_pad(vc16[:np_], nb, tp, 2 * QBLK, max(krc, tp + 2 * QBLK))
    bc_p = _bias_tile(rel_bias, C_KV_HEADS, C_HEADS // C_KV_HEADS, QBLK, ckp, -2 * QBLK)
    sinks = l1_sinks.astype(F32)
    oc_p = _swa(sinks, qc16, kc_p, vc_p, bc_p, nb=nb, nq=nq, tq=QBLK, row0=0, ck=ckp, r_base=0,
                r_step=QBLK, p_base=-2 * QBLK, p_step=QBLK, q_base=0, q_step=QBLK, coff=PROMPT_COFF, n_valid=t)

    buf = cache_c_k.shape[1]
    cks = -(-(buf + ds) // LANE) * LANE

    def cat_c(cache, new):
        a = jnp.concatenate([cache.reshape(db, buf, -1).astype(BF16), new[np_:].reshape(db, ds, -1)], axis=1)
        return jnp.pad(a, ((0, 0), (0, cks - buf - ds), (0, 0)))

    bc_s = _bias_tile(rel_bias, C_KV_HEADS, C_HEADS // C_KV_HEADS, ds, cks, -buf)
    oc_s = _swa(sinks, qc16, cat_c(cache_c_k, kc16), cat_c(cache_c_v, vc16), bc_s, nb=db, nq=1, tq=ds,
                row0=np_, ck=cks, r_base=0, r_step=0, p_base=past - buf, p_step=0, q_base=past,
                q_step=0, coff=0, n_valid=past + ds)
    oc = jnp.concatenate([oc_p, oc_s], axis=0)
    y1 = _mix_ln(h1, [oc], [_perm_rows(l1_w_out.astype(BF16), C_HEADS)], l1_ln1_g, l1_ln1_b, tm)
    h2 = _moe(y1, l1_router, l1_w_gate, l1_w_up, l1_w_down, l1_ln2_g, l1_ln2_b, tm, 512, 512)

    def pr(a, heads):
        a = a[:np_].reshape(nb, tp, -1)[:, :t]
        return a.reshape(nb, t, heads, HEAD_DIM) if heads else a[..., :HEAD_DIM]

    def sm(a, heads):
        a = a[np_:].reshape(db, ds, -1)
        return a.reshape(db, ds, heads, HEAD_DIM) if heads else a[..., :HEAD_DIM]

    y_prompt = h2[:np_].reshape(nb, tp, d)[:, N_META:t]
    y_sample = h2[np_:].reshape(db, ds, d)
    bufp = min(WINDOW, t)
    p_ck = pr(kc32, C_KV_HEADS)[:, t - bufp:]
    p_cv = pr(vc32, C_KV_HEADS)[:, t - bufp:]
    s_ck = jnp.concatenate([cache_c_k, sm(kc32, C_KV_HEADS)], axis=1)[:, ds:]
    s_cv = jnp.concatenate([cache_c_v, sm(vc32, C_KV_HEADS)], axis=1)[:, ds:]
    return (y_prompt, y_sample,
            pr(ka32, A_KV_HEADS), pr(va32, A_KV_HEADS), pr(ki32, 0), pr(kb32, B_HEADS), pr(vb32, B_HEADS),
            p_ck, p_cv,
            sm(ka32, A_KV_HEADS), sm(va32, A_KV_HEADS), sm(ki32, 0), sm(kb32, B_HEADS), sm(vb32, B_HEADS),
            s_ck, s_cv)
```

```python
import functools

import jax
import jax.numpy as jnp
from jax import lax
from jax.experimental import pallas as pl
from jax.experimental.pallas import tpu as pltpu

F32 = jnp.float32
BF16 = jnp.bfloat16
I32 = jnp.int32

D_MODEL = 1024
CHUNK_SHIFT = 6
N_META = 16
HEAD_DIM = 64
A_HEADS = 8
A_KV_HEADS = 2
IDX_HEADS = 4
TOPK_MAX = 256
B_HEADS = 8
C_HEADS = 16
C_KV_HEADS = 2
WINDOW = 128
WIN_CHUNKS = 2
NUM_BUCKETS = 32
N_EXPERTS = 8
LN_EPS = 1e-5
DEPTH = 2
DN_ALPHA = (2.0 * DEPTH) ** 0.25

LANE = 128
QBLK = 128
DSA_CK = 512
DSA_PADF = 384
DSA_RB = 32
LOG2E = 1.4426950408889634
SB_BAND = 512
SB_STOP = -120.0
NEG = -1e30
PROMPT_COFF = 64 - N_META
INT_MIN = -2 ** 31
VMEM_LIMIT = 56 * 1024 * 1024


def _cparams(sem):
    return pltpu.CompilerParams(dimension_semantics=sem, vmem_limit_bytes=VMEM_LIMIT)


def _nt(a, b):
    return lax.dot_general(a, b, (((1,), (1,)), ((), ())), preferred_element_type=F32)


def _layer_norm(v, g, b):
    mu = jnp.mean(v, axis=-1, keepdims=True)
    c = v - mu
    var = jnp.mean(c * c, axis=-1, keepdims=True)
    return c * lax.rsqrt(var + LN_EPS) * g + b


def _proj_kernel(x_ref, w_ref, *out_refs, specs):
    xb = x_ref[...].astype(BF16)
    k = 0
    for c0, width, kinds in specs:
        acc = jnp.dot(xb, w_ref[:, c0:c0 + width], preferred_element_type=F32)
        for kind in kinds:
            if kind == "f32":
                val = acc
            elif kind == "bf16":
                val = acc.astype(BF16)
            elif kind == "bf16s":
                val = (acc * 0.125).astype(BF16)
            elif kind == "bf16e":
                val = (acc * (0.125 * LOG2E)).astype(BF16)
            elif kind == "hilo":
                hi = acc.astype(BF16)
                lo = (acc - hi.astype(F32)).astype(BF16)
                lane = lax.broadcasted_iota(I32, acc.shape, 1) & (LANE - 1)
                val = jnp.where(lane < HEAD_DIM, hi, lo)
            elif kind == "wi":
                val = acc * 0.0625
            else:
                raise ValueError(kind)
            out_refs[k][...] = val
            k += 1


def _project(x, w16, specs, tm):
    n = x.shape[0]
    out_shape, out_specs = [], []
    for _, width, kinds in specs:
        for kind in kinds:
            dt = F32 if kind in ("f32", "wi") else BF16
            out_shape.append(jax.ShapeDtypeStruct((n, width), dt))
            out_specs.append(pl.BlockSpec((tm, width), lambda i: (i, 0)))
    return pl.pallas_call(
        functools.partial(_proj_kernel, specs=specs),
        grid=(pl.cdiv(n, tm),),
        in_specs=[pl.BlockSpec((tm, x.shape[1]), lambda i: (i, 0)),
                  pl.BlockSpec(w16.shape, lambda i: (0, 0))],
        out_specs=out_specs,
        out_shape=out_shape,
        compiler_params=_cparams(("parallel",)),
        name="proj",
    )(x, w16)


def _mix_ln_kernel(x_ref, *refs, n_pairs):
    o_refs = refs[:n_pairs]
    w_refs = refs[n_pairs:2 * n_pairs]
    g_ref, b_ref, y_ref = refs[2 * n_pairs:]
    acc = DN_ALPHA * x_ref[...]
    for o_ref, w_ref in zip(o_refs, w_refs):
        acc = acc + jnp.dot(o_ref[...], w_ref[...], preferred_element_type=F32)
    y_ref[...] = _layer_norm(acc, g_ref[...], b_ref[...])


def _mix_ln(x, os_, ws, g, b, tm):
    n, d = x.shape
    in_specs = [pl.BlockSpec((tm, d), lambda i: (i, 0))]
    in_specs += [pl.BlockSpec((tm, o.shape[1]), lambda i: (i, 0)) for o in os_]
    in_specs += [pl.BlockSpec(w.shape, lambda i: (0, 0)) for w in ws]
    in_specs += [pl.BlockSpec((1, d), lambda i: (0, 0))] * 2
    return pl.pallas_call(
        functools.partial(_mix_ln_kernel, n_pairs=len(os_)),
        grid=(pl.cdiv(n, tm),),
        in_specs=in_specs,
        out_specs=pl.BlockSpec((tm, d), lambda i: (i, 0)),
        out_shape=jax.ShapeDtypeStruct((n, d), F32),
        compiler_params=_cparams(("parallel",)),
        name="mix_ln",
    )(x, *os_, *ws, g.reshape(1, d), b.reshape(1, d))


def _ffn_ln_kernel(y_ref, wg_ref, wu_ref, wd_ref, g_ref, b_ref, o_ref, *, nf):
    y = y_ref[...]
    yb = y.astype(BF16)
    acc = DN_ALPHA * y
    for f in range(nf):
        hg = jnp.dot(yb, wg_ref[f], preferred_element_type=F32)
        hu = jnp.dot(yb, wu_ref[f], preferred_element_type=F32)
        h = (hg * jax.nn.sigmoid(hg) * hu).astype(BF16)
        acc = acc + jnp.dot(h, wd_ref[f], preferred_element_type=F32)
    o_ref[...] = _layer_norm(acc, g_ref[...], b_ref[...])


def _ffn_ln(y, wg, wu, wd, g, b, tm, tf):
    n, d = y.shape
    dff = wg.shape[1]
    nf = dff // tf
    wg3 = wg.astype(BF16).reshape(d, nf, tf).transpose(1, 0, 2)
    wu3 = wu.astype(BF16).reshape(d, nf, tf).transpose(1, 0, 2)
    wd3 = wd.astype(BF16).reshape(nf, tf, d)
    full3 = lambda i: (0, 0, 0)
    return pl.pallas_call(
        functools.partial(_ffn_ln_kernel, nf=nf),
        grid=(pl.cdiv(n, tm),),
        in_specs=[pl.BlockSpec((tm, d), lambda i: (i, 0)),
                  pl.BlockSpec(wg3.shape, full3), pl.BlockSpec(wu3.shape, full3),
                  pl.BlockSpec(wd3.shape, full3),
                  pl.BlockSpec((1, d), lambda i: (0, 0)), pl.BlockSpec((1, d), lambda i: (0, 0))],
        out_specs=pl.BlockSpec((tm, d), lambda i: (i, 0)),
        out_shape=jax.ShapeDtypeStruct((n, d), F32),
        compiler_params=_cparams(("parallel",)),
        name="ffn_ln",
    )(y, wg3, wu3, wd3, g.reshape(1, d), b.reshape(1, d))


def _sort_key(x):
    bits = lax.bitcast_convert_type(x, I32)
    return bits ^ ((bits >> 31) & 0x7FFFFFFF)


def _dsa_kernel(rb_ref, qi_ref, wi_ref, qa_ref, ki_ref, ka_ref, va_ref, bn_ref, o_ref,
                keys_ref, cut_ref, s_ref, p_ref, madd_ref, wbc_ref, cmax_ref, mpart_ref, alpha_ref,
                l_ref, acc_ref, *, tq, n_sel, e_base, e_step, q_base, q_step, coff, n_valid, padf):
    i = pl.program_id(1)
    ck = DSA_CK
    rbk = DSA_RB
    nt = ck // LANE
    e_end = e_base + e_step * i
    n_chunks = (e_end + ck - 1) // ck
    qpos = q_base + q_step * i + lax.broadcasted_iota(I32, (tq, 1), 0)
    bound = jnp.minimum(((((qpos + coff) >> CHUNK_SHIFT) + 1) << CHUNK_SHIFT) - coff, n_valid)
    lane_rb = lax.broadcasted_iota(I32, (rbk, ck), 1)
    lane128 = lax.broadcasted_iota(I32, (tq, LANE), 1)

    def key_rows(c):
        kpos0 = e_end - ck * (c + 1)
        return kpos0, pl.multiple_of(jnp.maximum(kpos0 + padf, 0), LANE)

    def tiles(x):
        return [x[:, t * LANE:(t + 1) * LANE] for t in range(nt)]

    qstack = jnp.concatenate([qi_ref[:, h * LANE:(h + 1) * LANE] for h in range(IDX_HEADS)], axis=0)
    for h in range(IDX_HEADS):
        wbc_ref[h] = jnp.broadcast_to(wi_ref[:, h:h + 1], (tq, LANE))
    cmax_ref[...] = jnp.full((tq, ck), -jnp.inf, F32)

    def idx_dot(c, dst):
        _, r = key_rows(c)
        s_ref[dst] = _nt(qstack, ki_ref[pl.ds(r, ck), :])

    def idx_keys(c, src):
        kpos0, _ = key_rows(c)
        for b in range(tq // rbk):
            r0 = b * rbk
            sc = None
            for h in range(IDX_HEADS):
                sh = jnp.maximum(s_ref[src, h * tq + r0:h * tq + r0 + rbk, :], 0.0)
                term = jnp.concatenate([wbc_ref[h, r0:r0 + rbk, :]] * nt, axis=1) * sh
                sc = term if sc is None else sc + term
            kpos = kpos0 + lane_rb
            allowed = (kpos >= 0) & (kpos < bound[r0:r0 + rbk])
            keys_ref[c, r0:r0 + rbk, :] = jnp.where(allowed, _sort_key(sc), INT_MIN)
            cmax_ref[r0:r0 + rbk, :] = jnp.maximum(cmax_ref[r0:r0 + rbk, :], jnp.where(allowed, sc, -jnp.inf))

    idx_dot(0, 0)

    def p1(j, carry):
        c = 2 * j
        idx_dot(c + 1, 1)
        idx_keys(c, 0)
        idx_dot(c + 2, 0)
        idx_keys(c + 1, 1)
        return carry

    lax.fori_loop(0, (n_chunks + 1) // 2, p1, 0)

    def count_ge(cand):
        def body(c, acc):
            u = keys_ref[c]
            for t in range(nt):
                acc = acc + jnp.where(u[:, t * LANE:(t + 1) * LANE] >= cand, 1.0, 0.0)
            return acc
        acc = lax.fori_loop(0, n_chunks, body, jnp.zeros((tq, LANE), F32))
        return jnp.sum(acc, axis=1, keepdims=True)

    cm = tiles(cmax_ref[...])
    if n_sel <= LANE:
        fold = [functools.reduce(jnp.maximum, cm)]
    elif n_sel <= 2 * LANE:
        fold = [jnp.maximum(cm[0], cm[2]), jnp.maximum(cm[1], cm[3])]
    else:
        fold = cm
    fmin = jnp.min(functools.reduce(jnp.minimum, fold), axis=1, keepdims=True)
    fmax = jnp.max(functools.reduce(jnp.maximum, cm), axis=1, keepdims=True)
    lo0 = _sort_key(fmin) - 1
    hi0 = _sort_key(fmax) + 2
    small = bound <= n_sel
    nbits = jnp.where(small, 0, 32 - lax.clz(lo0 ^ hi0))
    low_mask = (jnp.int32(1) << jnp.minimum(nbits, 31)) - 1
    t0 = jnp.where(small, INT_MIN + 1, jnp.where(nbits >= 32, INT_MIN, lo0 & ~low_mask))
    max_bits = jnp.max(nbits.astype(F32)).astype(I32)

    def bis(it, t):
        b = nbits - 1 - it
        cand = t + (jnp.int32(1) << jnp.maximum(b, 0))
        return jnp.where((b >= 0) & (count_ge(cand) >= n_sel), cand, t)

    thr = lax.fori_loop(0, max_bits, bis, t0)
    thr = jnp.maximum(thr, INT_MIN + 1)
    n_gt = count_ge(thr + 1)
    need = n_sel - n_gt
    excess = (count_ge(thr) - n_gt) > need

    cut_ref[...] = jnp.full((tq, LANE), 2 ** 30, I32)

    @pl.when(jnp.max(jnp.where(excess, 1.0, 0.0)) > 0.0)
    def _():
        rr = lax.broadcasted_iota(I32, (ck, ck), 0)
        cc = lax.broadcasted_iota(I32, (ck, ck), 1)
        upto = jnp.where(rr <= cc, 1.0, 0.0).astype(BF16)

        def body(j, st):
            seen, cut = st
            c = n_chunks - 1 - j
            kpos0, _ = key_rows(c)
            eq = jnp.where(keys_ref[c] == thr, 1.0, 0.0).astype(BF16)
            incl = jnp.dot(eq, upto, preferred_element_type=F32)
            below = jnp.sum(jnp.where(seen + incl < need, 1.0, 0.0), axis=1, keepdims=True)
            tot = seen + incl[:, ck - 1:ck]
            here = excess & (seen < need) & (tot >= need)
            cut = jnp.where(here, kpos0 + below.astype(I32) + 1, cut)
            return (tot, cut)

        _, cut = lax.fori_loop(0, n_chunks, body, (jnp.zeros((tq, 1), F32), jnp.full((tq, 1), 2 ** 30, I32)))
        cut_ref[...] = jnp.broadcast_to(cut, (tq, LANE))

    cut = cut_ref[:, 0:1]

    r4 = A_HEADS // A_KV_HEADS
    qg = []
    for g in range(A_KV_HEADS):
        rows = []
        for j in range(r4):
            slot = qa_ref[:, j * LANE:(j + 1) * LANE]
            half = (lane128 < HEAD_DIM) if g == 0 else (lane128 >= HEAD_DIM)
            rows.append(jnp.where(half, slot, jnp.zeros_like(slot)))
        qg.append(jnp.concatenate(rows, axis=0))
    farb = [jnp.concatenate([jnp.full((tq, 1), rb_ref[g * r4 + j], F32) for j in range(r4)], axis=0)
            for g in range(A_KV_HEADS)]
    l_ref[...] = jnp.zeros(l_ref.shape, F32)
    acc_ref[...] = jnp.zeros(acc_ref.shape, F32)

    def logits_dot(c, g):
        _, r = key_rows(c)
        s_ref[g] = _nt(qg[g], ka_ref[pl.ds(r, ck), :])

    def select_mask(c):
        kpos0, _ = key_rows(c)
        for b in range(tq // rbk):
            r0 = b * rbk
            u = keys_ref[c, r0:r0 + rbk, :]
            t_b = thr[r0:r0 + rbk]
            sel = (u > t_b) | ((u == t_b) & (kpos0 + lane_rb < cut[r0:r0 + rbk]))
            madd_ref[r0:r0 + rbk, :] = jnp.where(sel, 0.0, NEG)

    def softmax_passes(g, m_old, near):
        for b in range(r4 * tq // rbk):
            r0 = b * rbk
            q0 = r0 % tq
            sm = s_ref[g, r0:r0 + rbk, :] + madd_ref[q0:q0 + rbk, :]
            if near:
                sm = sm + bn_ref[g, r0:r0 + rbk, :]
            s_ref[g, r0:r0 + rbk, :] = sm
            mpart_ref[g, r0:r0 + rbk, :] = functools.reduce(jnp.maximum, tiles(sm))
        m_blk = jnp.max(mpart_ref[g], axis=1, keepdims=True)
        if not near:
            m_blk = m_blk + farb[g]
        m_new = jnp.maximum(m_old, m_blk)
        alpha = jnp.exp2(m_old - m_new)
        alpha_ref[g] = jnp.broadcast_to(alpha, (r4 * tq, LANE))
        shift = m_new if near else m_new - farb[g]
        for b in range(r4 * tq // rbk):
            r0 = b * rbk
            p = jnp.exp2(s_ref[g, r0:r0 + rbk, :] - shift[r0:r0 + rbk])
            l_ref[g, r0:r0 + rbk, :] = (alpha_ref[g, r0:r0 + rbk, :] * l_ref[g, r0:r0 + rbk, :]
                                        + functools.reduce(jnp.add, tiles(p)))
            p_ref[g, r0:r0 + rbk, :] = p.astype(BF16)
        return m_new

    def value_dot(c, g):
        _, r = key_rows(c)
        acc_ref[g] = alpha_ref[g] * acc_ref[g] + jnp.dot(p_ref[g], va_ref[pl.ds(r, ck), :],
                                                        preferred_element_type=F32)

    m0 = jnp.full((r4 * tq, 1), NEG, F32)
    logits_dot(0, 0)
    logits_dot(0, 1)
    select_mask(0)
    m0n = softmax_passes(0, m0, True)
    value_dot(0, 0)
    logits_dot(1, 0)
    m1n = softmax_passes(1, m0, True)

    def p3(c, ms):
        value_dot(c - 1, 1)
        logits_dot(c, 1)
        select_mask(c)
        m_a = softmax_passes(0, ms[0], False)
        value_dot(c, 0)
        logits_dot(c + 1, 0)
        m_b = softmax_passes(1, ms[1], False)
        return (m_a, m_b)

    lax.fori_loop(1, n_chunks, p3, (m0n, m1n))
    value_dot(n_chunks - 1, 1)
    outs = [acc_ref[g] / jnp.sum(l_ref[g], axis=1, keepdims=True) for g in range(A_KV_HEADS)]
    for j in range(r4):
        lo = outs[0][j * tq:(j + 1) * tq]
        hi = outs[1][j * tq:(j + 1) * tq]
        o_ref[:, j * LANE:(j + 1) * LANE] = jnp.where(lane128 < HEAD_DIM, lo, hi).astype(BF16)


def _dsa(rb_far, qi, wi, qa, ki, ka, va, bn, *, nb, nq, tq, row0, n_sel, e_base, e_step,
         q_base, q_step, coff, n_valid, padf, n_chunks_max):
    rb0 = row0 // tq
    r4 = A_HEADS // A_KV_HEADS
    assert IDX_HEADS == r4 and tq % DSA_RB == 0
    qmap = lambda b, i: (rb0 + b * nq + i, 0)
    kmap = lambda b, i: (b, 0, 0)
    kern = functools.partial(_dsa_kernel, tq=tq, n_sel=n_sel, e_base=e_base, e_step=e_step,
                             q_base=q_base, q_step=q_step, coff=coff, n_valid=n_valid, padf=padf)
    return pl.pallas_call(
        kern,
        grid=(nb, nq),
        in_specs=[pl.BlockSpec(memory_space=pltpu.SMEM),
                  pl.BlockSpec((tq, qi.shape[1]), qmap),
                  pl.BlockSpec((tq, LANE), qmap),
                  pl.BlockSpec((tq, qa.shape[1]), qmap),
                  pl.BlockSpec((None,) + ki.shape[1:], kmap),
                  pl.BlockSpec((None,) + ka.shape[1:], kmap),
                  pl.BlockSpec((None,) + va.shape[1:], kmap),
                  pl.BlockSpec(bn.shape, lambda b, i: (0, 0, 0))],
        out_specs=pl.BlockSpec((tq, qa.shape[1]), lambda b, i: (b * nq + i, 0)),
        out_shape=jax.ShapeDtypeStruct((nb * nq * tq, qa.shape[1]), BF16),
        scratch_shapes=[pltpu.VMEM((n_chunks_max + 1, tq, DSA_CK), I32),
                        pltpu.VMEM((tq, LANE), I32),
                        pltpu.VMEM((2, r4 * tq, DSA_CK), F32),
                        pltpu.VMEM((A_KV_HEADS, r4 * tq, DSA_CK), BF16),
                        pltpu.VMEM((tq, DSA_CK), F32),
                        pltpu.VMEM((IDX_HEADS, tq, LANE), F32),
                        pltpu.VMEM((tq, DSA_CK), F32),
                        pltpu.VMEM((A_KV_HEADS, r4 * tq, LANE), F32),
                        pltpu.VMEM((A_KV_HEADS, r4 * tq, LANE), F32),
                        pltpu.VMEM((A_KV_HEADS, r4 * tq, LANE), F32),
                        pltpu.VMEM((A_KV_HEADS, r4 * tq, LANE), F32)],
        compiler_params=_cparams(("parallel", "arbitrary")),
        name="dsa",
    )(rb_far, qi, wi, qa, ki, ka, va, bn)


def _sb_kernel(tri_ref, q_ref, k_ref, v_ref, o_ref, *, tq, e_base, e_step, q_base, q_step, padf, n_valid):
    i = pl.program_id(2)
    bw = SB_BAND
    e_end = e_base + e_step * i
    n_bands = (e_end + bw - 1) // bw
    qpos = q_base + q_step * i + lax.broadcasted_iota(I32, (tq, 1), 0)
    lane = lax.broadcasted_iota(I32, (tq, LANE), 1)
    q = q_ref[...]
    zero = jnp.zeros_like(q)
    qh = [jnp.where(lane < HEAD_DIM, q, zero), jnp.where(lane >= HEAD_DIM, q, zero)]
    lane_bw = lax.broadcasted_iota(I32, (tq, bw), 1)

    def cond(st):
        return (st[0] < n_bands) & (st[1] > SB_STOP)

    def body(st):
        m, _, carry, acc = st
        kpos0 = e_end - bw * (m + 1)
        r = pl.multiple_of(kpos0 + padf, LANE)
        kt = k_ref[pl.ds(r, bw), :]
        vt = v_ref[pl.ds(r, bw), :]
        kpos = kpos0 + lane_bw
        before = (kpos < qpos) & (kpos >= 0) & (kpos < n_valid)
        zs, lss, lks, parts = [], [], [], []
        for h in range(2):
            z = _nt(qh[h], kt)
            ls = -(jnp.maximum(z, 0.0) + jnp.log1p(jnp.exp(-jnp.abs(z))))
            lk = jnp.where(before, ls, 0.0)
            hi = lk.astype(BF16)
            parts += [hi, (lk - hi.astype(F32)).astype(BF16)]
            zs.append(z)
            lss.append(ls)
            lks.append(lk)
        sums = jnp.dot(jnp.concatenate(parts, axis=0), tri_ref[...], preferred_element_type=F32)
        new_c, new_a = [], []
        worst = jnp.float32(-jnp.inf)
        for h in range(2):
            bl = sums[2 * h * tq:(2 * h + 1) * tq] + sums[(2 * h + 1) * tq:(2 * h + 2) * tq]
            w = jnp.where(before, jnp.exp(lss[h] + zs[h] + bl + carry[h]), 0.0)
            new_a.append(acc[h] + jnp.dot(w.astype(BF16), vt, preferred_element_type=F32))
            c_n = carry[h] + bl[:, 0:1] + lks[h][:, 0:1]
            new_c.append(c_n)
            worst = jnp.maximum(worst, jnp.max(c_n))
        return (m + 1, worst, tuple(new_c), tuple(new_a))

    init = (jnp.int32(0), jnp.float32(0.0),
            (jnp.zeros((tq, 1), F32), jnp.zeros((tq, 1), F32)),
            (jnp.zeros((tq, LANE), F32), jnp.zeros((tq, LANE), F32)))
    _, _, _, acc = lax.while_loop(cond, body, init)
    o_ref[...] = jnp.where(lane < HEAD_DIM, acc[0], acc[1]).astype(BF16)


def _stick_break(q, k, v, *, nb, nq, tq, row0, e_base, e_step, q_base, q_step, padf, n_valid):
    rb0 = row0 // tq
    npair = q.shape[1] // LANE
    kern = functools.partial(_sb_kernel, tq=tq, e_base=e_base, e_step=e_step, q_base=q_base,
                             q_step=q_step, padf=padf, n_valid=n_valid)
    rr = lax.broadcasted_iota(I32, (SB_BAND, SB_BAND), 0)
    cc = lax.broadcasted_iota(I32, (SB_BAND, SB_BAND), 1)
    tri = jnp.where(rr > cc, 1.0, 0.0).astype(BF16)
    kspec = pl.BlockSpec((None, k.shape[1], LANE), lambda b, p, i: (b, 0, p))
    return pl.pallas_call(
        kern,
        grid=(nb, npair, nq),
        in_specs=[pl.BlockSpec(tri.shape, lambda b, p, i: (0, 0)),
                  pl.BlockSpec((tq, LANE), lambda b, p, i: (rb0 + b * nq + i, p)), kspec, kspec],
        out_specs=pl.BlockSpec((tq, LANE), lambda b, p, i: (b * nq + i, p)),
        out_shape=jax.ShapeDtypeStruct((nb * nq * tq, q.shape[1]), BF16),
        compiler_params=_cparams(("parallel", "parallel", "arbitrary")),
        name="stick_break",
    )(tri, q, k, v)


def _swa_kernel(sink_ref, q_ref, k_ref, v_ref, bias_ref, o_ref, *, tq, ck, r_base, r_step,
                p_base, p_step, q_base, q_step, coff, n_valid):
    i = pl.program_id(1)
    r0 = pl.multiple_of(r_base + r_step * i, 16)
    kpos0 = p_base + p_step * i
    kt = k_ref[pl.ds(r0, ck), :]
    vt = v_ref[pl.ds(r0, ck), :]
    qpos = q_base + q_step * i + lax.broadcasted_iota(I32, (tq, 1), 0)
    kpos = kpos0 + lax.broadcasted_iota(I32, (tq, ck), 1)
    qc = (qpos + coff) >> CHUNK_SHIFT
    hi_b = jnp.minimum(((qc + 1) << CHUNK_SHIFT) - coff, n_valid)
    lo_b = jnp.maximum(((qc - WIN_CHUNKS) << CHUNK_SHIFT) - coff, 0)
    allowed = (kpos >= lo_b) & (kpos < hi_b)
    r8 = C_HEADS // C_KV_HEADS
    allowed8 = jnp.concatenate([allowed] * r8, axis=0)
    lane = lax.broadcasted_iota(I32, (tq, LANE), 1)
    outs = []
    for g in range(C_KV_HEADS):
        rows = []
        for j in range(r8):
            slot = q_ref[:, j * LANE:(j + 1) * LANE]
            half = (lane < HEAD_DIM) if g == 0 else (lane >= HEAD_DIM)
            rows.append(jnp.where(half, slot, jnp.zeros_like(slot)))
        qs = jnp.concatenate(rows, axis=0)
        sink = jnp.concatenate([jnp.full((tq, 1), sink_ref[g * r8 + j], F32) for j in range(r8)], axis=0)
        s = _nt(qs, kt) + bias_ref[g]
        s = jnp.where(allowed8, s, NEG)
        m = jnp.maximum(jnp.max(s, axis=1, keepdims=True), sink)
        e = jnp.exp(s - m)
        den = jnp.sum(e, axis=1, keepdims=True) + jnp.exp(sink - m)
        p = (e / den).astype(BF16)
        outs.append(jnp.dot(p, vt, preferred_element_type=F32))
    for j in range(r8):
        lo = outs[0][j * tq:(j + 1) * tq]
        hi = outs[1][j * tq:(j + 1) * tq]
        o_ref[:, j * LANE:(j + 1) * LANE] = jnp.where(lane < HEAD_DIM, lo, hi).astype(BF16)


def _swa(sinks, q, k, v, bias, *, nb, nq, tq, row0, ck, r_base, r_step, p_base, p_step,
         q_base, q_step, coff, n_valid):
    rb0 = row0 // tq
    kern = functools.partial(_swa_kernel, tq=tq, ck=ck, r_base=r_base, r_step=r_step, p_base=p_base,
                             p_step=p_step, q_base=q_base, q_step=q_step, coff=coff, n_valid=n_valid)
    kmap = lambda b, i: (b, 0, 0)
    return pl.pallas_call(
        kern,
        grid=(nb, nq),
        in_specs=[pl.BlockSpec(memory_space=pltpu.SMEM),
                  pl.BlockSpec((tq, q.shape[1]), lambda b, i: (rb0 + b * nq + i, 0)),
                  pl.BlockSpec((None,) + k.shape[1:], kmap),
                  pl.BlockSpec((None,) + v.shape[1:], kmap),
                  pl.BlockSpec(bias.shape, lambda b, i: (0, 0, 0))],
        out_specs=pl.BlockSpec((tq, q.shape[1]), lambda b, i: (b * nq + i, 0)),
        out_shape=jax.ShapeDtypeStruct((nb * nq * tq, q.shape[1]), BF16),
        compiler_params=_cparams(("parallel", "arbitrary")),
        name="swa",
    )(sinks, q, k, v, bias)


def _router_kernel(y_ref, rh_ref, rl_ref, idx_ref, gate_ref):
    y = y_ref[...]
    yh = y.astype(BF16)
    yl = (y - yh.astype(F32)).astype(BF16)
    rh = rh_ref[...]
    logits = (jnp.dot(yh, rh, preferred_element_type=F32) + jnp.dot(yl, rh, preferred_element_type=F32)
              + jnp.dot(yh, rl_ref[...], preferred_element_type=F32))
    lane = lax.broadcasted_iota(I32, logits.shape, 1)
    logits = jnp.where(lane < N_EXPERTS, logits, -jnp.inf)
    m1 = jnp.max(logits, axis=1, keepdims=True)
    i1 = jnp.min(jnp.where(logits == m1, lane, LANE), axis=1, keepdims=True)
    rest = jnp.where(lane == i1, -jnp.inf, logits)
    m2 = jnp.max(rest, axis=1, keepdims=True)
    i2 = jnp.min(jnp.where(rest == m2, lane, LANE), axis=1, keepdims=True)
    e2 = jnp.exp(m2 - m1)
    den = 1.0 + e2
    idx_ref[...] = jnp.where(lane == 0, i1, jnp.where(lane == 1, i2, 0))
    gate_ref[...] = jnp.where(lane == 0, 1.0 / den, jnp.where(lane == 1, e2 / den, 0.0))


def _router(y, router, tm):
    n, d = y.shape
    rpad = jnp.pad(router.astype(F32), ((0, 0), (0, LANE - router.shape[1])))
    rh = rpad.astype(BF16)
    rl = (rpad - rh.astype(F32)).astype(BF16)
    return pl.pallas_call(
        _router_kernel,
        grid=(pl.cdiv(n, tm),),
        in_specs=[pl.BlockSpec((tm, d), lambda i: (i, 0)),
                  pl.BlockSpec((d, LANE), lambda i: (0, 0)), pl.BlockSpec((d, LANE), lambda i: (0, 0))],
        out_specs=[pl.BlockSpec((tm, LANE), lambda i: (i, 0)), pl.BlockSpec((tm, LANE), lambda i: (i, 0))],
        out_shape=[jax.ShapeDtypeStruct((n, LANE), I32), jax.ShapeDtypeStruct((n, LANE), F32)],
        compiler_params=_cparams(("parallel",)),
        name="router",
    )(y, rh, rl)


def _moe_kernel(be_ref, nu_ref, x_ref, wg_ref, wu_ref, wd_ref, o_ref, acc_ref):
    t = pl.program_id(0)
    f = pl.program_id(1)

    @pl.when(f == 0)
    def _():
        acc_ref[...] = jnp.zeros_like(acc_ref)

    @pl.when(t < nu_ref[0])
    def _():
        xb = x_ref[...]
        hg = jnp.dot(xb, wg_ref[...], preferred_element_type=F32)
        hu = jnp.dot(xb, wu_ref[...], preferred_element_type=F32)
        h = (hg * jax.nn.sigmoid(hg) * hu).astype(BF16)
        acc_ref[...] += jnp.dot(h, wd_ref[...], preferred_element_type=F32)

    @pl.when(f == pl.num_programs(1) - 1)
    def _():
        o_ref[...] = acc_ref[...]


def _moe_experts(block_expert, n_used, xs, wg, wu, wd, tmb, tf):
    n_slots, d = xs.shape
    de = wg.shape[2]
    grid_spec = pltpu.PrefetchScalarGridSpec(
        num_scalar_prefetch=2,
        grid=(n_slots // tmb, de // tf),
        in_specs=[pl.BlockSpec((tmb, d), lambda t, f, be, nu: (t, 0)),
                  pl.BlockSpec((None, d, tf), lambda t, f, be, nu: (be[t], 0, f)),
                  pl.BlockSpec((None, d, tf), lambda t, f, be, nu: (be[t], 0, f)),
                  pl.BlockSpec((None, tf, d), lambda t, f, be, nu: (be[t], f, 0))],
        out_specs=pl.BlockSpec((tmb, d), lambda t, f, be, nu: (t, 0)),
        scratch_shapes=[pltpu.VMEM((tmb, d), F32)],
    )
    return pl.pallas_call(
        _moe_kernel,
        grid_spec=grid_spec,
        out_shape=jax.ShapeDtypeStruct((n_slots, d), F32),
        compiler_params=_cparams(("parallel", "arbitrary")),
        name="moe_experts",
    )(block_expert, n_used, xs, wg, wu, wd)


def _res_ln_kernel(y_ref, m_ref, g_ref, b_ref, o_ref):
    o_ref[...] = _layer_norm(DN_ALPHA * y_ref[...] + m_ref[...], g_ref[...], b_ref[...])


def _res_ln(y, m, g, b, tm):
    n, d = y.shape
    row = pl.BlockSpec((tm, d), lambda i: (i, 0))
    vec = pl.BlockSpec((1, d), lambda i: (0, 0))
    return pl.pallas_call(
        _res_ln_kernel,
        grid=(pl.cdiv(n, tm),),
        in_specs=[row, row, vec, vec],
        out_specs=row,
        out_shape=jax.ShapeDtypeStruct((n, d), F32),
        compiler_params=_cparams(("parallel",)),
        name="res_ln",
    )(y, m, g.reshape(1, d), b.reshape(1, d))


def _moe(y, router, w_gate, w_up, w_down, g, b, tm, tmb, tf):
    n, d = y.shape
    idx_p, gate_p = _router(y, router, tm)
    top_idx = idx_p[:, :2]
    gates = gate_p[:, :2]
    e_flat = top_idx.reshape(-1)
    t_flat = jnp.repeat(jnp.arange(n, dtype=I32), 2)
    g_flat = gates.reshape(-1)
    n_assign = 2 * n
    counts = jnp.bincount(e_flat, length=N_EXPERTS)
    padded = (counts + tmb - 1) // tmb * tmb
    pad_end = jnp.cumsum(padded)
    order = jnp.argsort(e_flat)
    e_sorted = e_flat[order]
    rank = jnp.arange(n_assign, dtype=I32) - (jnp.cumsum(counts) - counts)[e_sorted]
    slot = (pad_end - padded)[e_sorted] + rank
    n_blocks = -(-n_assign // tmb) + N_EXPERTS
    n_slots = n_blocks * tmb
    slot_tok = jnp.full((n_slots,), n, I32).at[slot].set(t_flat[order])
    slot_gate = jnp.zeros((n_slots,), F32).at[slot].set(g_flat[order])
    block_expert = jnp.minimum(
        jnp.searchsorted(pad_end, jnp.arange(n_blocks, dtype=I32) * tmb, side="right"),
        N_EXPERTS - 1).astype(I32)
    n_used = (pad_end[-1] // tmb).astype(I32).reshape(1)
    x_pad = jnp.concatenate([y, jnp.zeros((1, d), y.dtype)], axis=0)
    xs = x_pad[slot_tok].astype(BF16)
    outs = _moe_experts(block_expert, n_used, xs, w_gate.astype(BF16), w_up.astype(BF16),
                        w_down.astype(BF16), tmb, tf)
    mo = jax.ops.segment_sum(outs * slot_gate[:, None], slot_tok, num_segments=n + 1)[:n]
    return _res_ln(y, mo, g, b, tm)


def _t5_bucket(rel):
    half = NUM_BUCKETS // 2
    exact = half // 2
    n = jnp.abs(rel)
    far = exact + sum((n >= t).astype(I32) for t in (12, 16, 23, 32, 46, 64, 91))
    return jnp.where(rel > 0, half, 0) + jnp.where(n < exact, n, far)


def _bias_tile(rel_bias, n_groups, per_group, tq, ck, d0):
    d = d0 + jnp.arange(ck, dtype=I32)[None, :] - jnp.arange(tq, dtype=I32)[:, None]
    tile = rel_bias.astype(F32)[_t5_bucket(d)]
    tile = jnp.transpose(tile[:, :, :n_groups * per_group], (2, 0, 1))
    return tile.reshape(n_groups, per_group * tq, ck)


def _cols(w, ranges):
    parts = []
    for r in ranges:
        if isinstance(r, int):
            parts.append(jnp.zeros((w.shape[0], r), w.dtype))
        else:
            parts.append(w[:, r[0]:r[1]])
    return jnp.concatenate(parts, axis=1)


def _head_pair_order(n_heads):
    half = n_heads // 2
    order = []
    for j in range(half):
        order += [j, half + j]
    return order


L0_SPECS = ((0, 512, ("bf16e",)), (512, 128, ("f32", "bf16")), (640, 128, ("f32", "bf16")),
            (768, 512, ("hilo",)), (1280, 128, ("f32", "bf16")), (1408, 128, ("wi",)),
            (1536, 512, ("bf16s",)), (2048, 512, ("f32", "bf16")), (2560, 512, ("f32", "bf16")))
L1_SPECS = ((0, 1024, ("bf16s",)), (1024, 128, ("f32", "bf16")), (1152, 128, ("f32", "bf16")))


def _l0_weight(w_in):
    hd = HEAD_DIM
    rng = [(hd * h, hd * h + hd) for h in _head_pair_order(A_HEADS)]
    rng += [(512, 640), (640, 768)]
    for h in range(IDX_HEADS):
        rng += [(768 + hd * h, 768 + hd * h + hd)] * 2
    rng += [(1024, 1088)] * 2
    rng += [(1088, 1092), LANE - IDX_HEADS]
    rng += [(1092, 1604), (1604, 2116), (2116, 2628)]
    return _cols(w_in, rng).astype(BF16)


def _l1_weight(w_in):
    hd = HEAD_DIM
    rng = [(hd * h, hd * h + hd) for h in _head_pair_order(C_HEADS)]
    rng += [(1024, 1152), (1152, 1280)]
    return _cols(w_in, rng).astype(BF16)


def _perm_rows(w, n_heads):
    return jnp.concatenate([w[HEAD_DIM * h:HEAD_DIM * (h + 1)] for h in _head_pair_order(n_heads)], axis=0)


def _front_pad(a, nb, rows_in, front, rows_out):
    a = a.reshape(nb, rows_in, a.shape[-1])
    return jnp.pad(a, ((0, 0), (front, rows_out - front - rows_in), (0, 0)))


def _layer0_attention(x_prompt, x_sample, cache_a_k, cache_a_v, cache_a_idx_k, cache_b_k, cache_b_v,
                      meta_tokens, rel_bias, l0_w_in):
    nb, seq, d = x_prompt.shape
    t = N_META + seq
    nq = -(-t // QBLK)
    tp = nq * QBLK
    db, ds, _ = x_sample.shape
    past = cache_a_k.shape[1]
    assert d == D_MODEL and past % LANE == 0 and ds % DSA_RB == 0 and ds <= 64
    np_, ns = nb * tp, db * ds
    tm = 512

    meta = jnp.broadcast_to(meta_tokens[None].astype(x_prompt.dtype), (nb, N_META, d))
    hp = jnp.concatenate([meta, x_prompt, jnp.zeros((nb, tp - t, d), x_prompt.dtype)], axis=1)
    x_all = jnp.concatenate([hp.reshape(np_, d), x_sample.reshape(ns, d)], axis=0)
    rel_bias = rel_bias.astype(F32)
    rb_far = rel_bias[NUM_BUCKETS // 2 - 1] * LOG2E

    (qa16, ka32, ka16, va32, va16, qi16, ki32, ki16, wi32, qb16, kb32, kb16, vb32, vb16) = _project(
        x_all, _l0_weight(l0_w_in), L0_SPECS, tm)

    n_sel_p = min(TOPK_MAX, (t - N_META) // 4)
    kr = QBLK * (nq + 1) + DSA_PADF
    ncm = -(-(QBLK * (nq + 1)) // DSA_CK)
    pk = [_front_pad(a[:np_], nb, tp, DSA_PADF, kr) for a in (ki16, ka16, va16)]
    bn_p = _bias_tile(rel_bias, A_KV_HEADS, A_HEADS // A_KV_HEADS, QBLK, DSA_CK, -(DSA_CK // 2)) * LOG2E
    oa_p = _dsa(rb_far, qi16, wi32, qa16, *pk, bn_p, nb=nb, nq=nq, tq=QBLK, row0=0, n_sel=n_sel_p,
                e_base=2 * QBLK, e_step=QBLK, q_base=0, q_step=QBLK, coff=PROMPT_COFF,
                n_valid=t, padf=DSA_PADF, n_chunks_max=ncm)

    n_keys = past + ds
    n_sel_s = min(TOPK_MAX, n_keys // 4)
    e_s = -(-n_keys // LANE) * LANE
    ncs = -(-e_s // DSA_CK)
    krs = ncs * DSA_CK
    fs = krs - e_s

    def cat_keys(cache, new, width_dup):
        c = cache.reshape(db, past, -1).astype(BF16)
        if width_dup:
            c = jnp.concatenate([c, c], axis=-1)
        a = jnp.concatenate([c, new[np_:].reshape(db, ds, -1)], axis=1)
        return jnp.pad(a, ((0, 0), (fs, krs - fs - n_keys), (0, 0)))

    sk = [cat_keys(cache_a_idx_k, ki16, True), cat_keys(cache_a_k, ka16, False),
          cat_keys(cache_a_v, va16, False)]
    bn_s = _bias_tile(rel_bias, A_KV_HEADS, A_HEADS // A_KV_HEADS, ds, DSA_CK, e_s - DSA_CK - past) * LOG2E
    oa_s = _dsa(rb_far, qi16, wi32, qa16, *sk, bn_s, nb=db, nq=1, tq=ds, row0=np_, n_sel=n_sel_s,
                e_base=e_s, e_step=0, q_base=past, q_step=0, coff=0, n_valid=n_keys, padf=fs,
                n_chunks_max=ncs)

    pfb = SB_BAND - QBLK
    kb_p = _front_pad(kb16[:np_], nb, tp, pfb, tp + pfb)
    vb_p = _front_pad(vb16[:np_], nb, tp, pfb, tp + pfb)
    ob_p = _stick_break(qb16, kb_p, vb_p, nb=nb, nq=nq, tq=QBLK, row0=0, e_base=QBLK, e_step=QBLK,
                        q_base=0, q_step=QBLK, padf=pfb, n_valid=t)
    rows_s = -(-e_s // SB_BAND) * SB_BAND
    pfs = rows_s - e_s

    def cat_b(cache, new):
        a = jnp.concatenate([cache.reshape(db, past, -1).astype(BF16), new[np_:].reshape(db, ds, -1)], axis=1)
        return jnp.pad(a, ((0, 0), (pfs, rows_s - pfs - n_keys), (0, 0)))

    ob_s = _stick_break(qb16, cat_b(cache_b_k, kb16), cat_b(cache_b_v, vb16), nb=db, nq=1, tq=ds,
                        row0=np_, e_base=e_s, e_step=0, q_base=past, q_step=0, padf=pfs, n_valid=n_keys)

    oa = jnp.concatenate([oa_p, oa_s], axis=0)
    ob = jnp.concatenate([ob_p, ob_s], axis=0)
    return dict(x_all=x_all, oa=oa, ob=ob, np=np_, tp=tp, t=t, nq=nq, rel_bias=rel_bias,
                ka32=ka32, va32=va32, ki32=ki32, kb32=kb32, vb32=vb32)


def kernel(x_prompt, x_sample, cache_a_k, cache_a_v, cache_a_idx_k, cache_b_k, cache_b_v, cache_c_k, cache_c_v, meta_tokens, rel_bias, l0_w_in, l0_w_out, l0_ln1_g, l0_ln1_b, l0_w_gate, l0_w_up, l0_w_down, l0_ln2_g, l0_ln2_b, l1_w_in, l1_sinks, l1_w_out, l1_ln1_g, l1_ln1_b, l1_router, l1_w_gate, l1_w_up, l1_w_down, l1_ln2_g, l1_ln2_b):
    a0 = _layer0_attention(x_prompt, x_sample, cache_a_k, cache_a_v, cache_a_idx_k, cache_b_k, cache_b_v,
                           meta_tokens, rel_bias, l0_w_in)
    x_all, oa, ob, np_, tp, t, nq, rel_bias = (a0[k] for k in ("x_all", "oa", "ob", "np", "tp", "t", "nq", "rel_bias"))
    ka32, va32, ki32, kb32, vb32 = (a0[k] for k in ("ka32", "va32", "ki32", "kb32", "vb32"))
    nb, _, d = x_prompt.shape
    db, ds, _ = x_sample.shape
    past = cache_a_k.shape[1]
    tm = 512
    w_out0 = l0_w_out.astype(BF16)
    half0 = A_HEADS * HEAD_DIM
    y0 = _mix_ln(x_all, [oa, ob], [_perm_rows(w_out0[:half0], A_HEADS), w_out0[half0:]],
                 l0_ln1_g, l0_ln1_b, tm)
    h1 = _ffn_ln(y0, l0_w_gate, l0_w_up, l0_w_down, l0_ln2_g, l0_ln2_b, tm, 256)

    qc16, kc32, kc16, vc32, vc16 = _project(h1, _l1_weight(l1_w_in), L1_SPECS, tm)
    ckp = 4 * QBLK
    krc = QBLK * (nq - 1) + ckp
    kc_p = _front_pad(kc16[:np_], nb, tp, 2 * QBLK, max(krc, tp + 2 * QBLK))
    vc_p = _front_pad(vc16[:np_], nb, tp, 2 * QBLK, max(krc, tp + 2 * QBLK))
    bc_p = _bias_tile(rel_bias, C_KV_HEADS, C_HEADS // C_KV_HEADS, QBLK, ckp, -2 * QBLK)
    sinks = l1_sinks.astype(F32)
    oc_p = _swa(sinks, qc16, kc_p, vc_p, bc_p, nb=nb, nq=nq, tq=QBLK, row0=0, ck=ckp, r_base=0,
                r_step=QBLK, p_base=-2 * QBLK, p_step=QBLK, q_base=0, q_step=QBLK, coff=PROMPT_COFF, n_valid=t)

    buf = cache_c_k.shape[1]
    cks = -(-(buf + ds) // LANE) * LANE

    def cat_c(cache, new):
        a = jnp.concatenate([cache.reshape(db, buf, -1).astype(BF16), new[np_:].reshape(db, ds, -1)], axis=1)
        return jnp.pad(a, ((0, 0), (0, cks - buf - ds), (0, 0)))

    bc_s = _bias_tile(rel_bias, C_KV_HEADS, C_HEADS // C_KV_HEADS, ds, cks, -buf)
    oc_s = _swa(sinks, qc16, cat_c(cache_c_k, kc16), cat_c(cache_c_v, vc16), bc_s, nb=db, nq=1, tq=ds,
                row0=np_, ck=cks, r_base=0, r_step=0, p_base=past - buf, p_step=0, q_base=past,
                q_step=0, coff=0, n_valid=past + ds)
    oc = jnp.concatenate([oc_p, oc_s], axis=0)
    y1 = _mix_ln(h1, [oc], [_perm_rows(l1_w_out.astype(BF16), C_HEADS)], l1_ln1_g, l1_ln1_b, tm)
    h2 = _moe(y1, l1_router, l1_w_gate, l1_w_up, l1_w_down, l1_ln2_g, l1_ln2_b, tm, 512, 512)

    def pr(a, heads):
        a = a[:np_].reshape(nb, tp, -1)[:, :t]
        return a.reshape(nb, t, heads, HEAD_DIM) if heads else a[..., :HEAD_DIM]

    def sm(a, heads):
        a = a[np_:].reshape(db, ds, -1)
        return a.reshape(db, ds, heads, HEAD_DIM) if heads else a[..., :HEAD_DIM]

    y_prompt = h2[:np_].reshape(nb, tp, d)[:, N_META:t]
    y_sample = h2[np_:].reshape(db, ds, d)
    bufp = min(WINDOW, t)
    p_ck = pr(kc32, C_KV_HEADS)[:, t - bufp:]
    p_cv = pr(vc32, C_KV_HEADS)[:, t - bufp:]
    s_ck = jnp.concatenate([cache_c_k, sm(kc32, C_KV_HEADS)], axis=1)[:, ds:]
    s_cv = jnp.concatenate([cache_c_v, sm(vc32, C_KV_HEADS)], axis=1)[:, ds:]
    return (y_prompt, y_sample,
            pr(ka32, A_KV_HEADS), pr(va32, A_KV_HEADS), pr(ki32, 0), pr(kb32, B_HEADS), pr(vb32, B_HEADS),
            p_ck, p_cv,
            sm(ka32, A_KV_HEADS), sm(va32, A_KV_HEADS), sm(ki32, 0), sm(kb32, B_HEADS), sm(vb32, B_HEADS),
            s_ck, s_cv)
```

```python
import functools

import jax
import jax.numpy as jnp
from jax import lax
from jax.experimental import pallas as pl
from jax.experimental.pallas import tpu as pltpu

F32 = jnp.float32
BF16 = jnp.bfloat16
I32 = jnp.int32

D_MODEL = 1024
CHUNK_SHIFT = 6
N_META = 16
HEAD_DIM = 64
A_HEADS = 8
A_KV_HEADS = 2
IDX_HEADS = 4
TOPK_MAX = 256
B_HEADS = 8
C_HEADS = 16
C_KV_HEADS = 2
WINDOW = 128
WIN_CHUNKS = 2
NUM_BUCKETS = 32
N_EXPERTS = 8
TOP_K = 2
LN_EPS = 1e-5
DEPTH = 2
DN_ALPHA = (2.0 * DEPTH) ** 0.25

LANE = 128
QBLK = 128
DSA_CK = 512
DSA_PADF = 384
DSA_RB = 32
LOG2E = 1.4426950408889634
SB_BAND = 512
SB_STOP = -120.0
NEG = -1e30
PROMPT_COFF = 64 - N_META
INT_MIN = -2 ** 31
VMEM_LIMIT = 56 * 1024 * 1024


def _cparams(sem):
    return pltpu.CompilerParams(dimension_semantics=sem, vmem_limit_bytes=VMEM_LIMIT)


def _nt(a, b):
    return lax.dot_general(a, b, (((1,), (1,)), ((), ())), preferred_element_type=F32)


def _layer_norm(v, g, b):
    mu = jnp.mean(v, axis=-1, keepdims=True)
    c = v - mu
    var = jnp.mean(c * c, axis=-1, keepdims=True)
    return c * lax.rsqrt(var + LN_EPS) * g + b


def _proj_kernel(x_ref, w_ref, *out_refs, specs):
    xb = x_ref[...].astype(BF16)
    k = 0
    for c0, width, kinds in specs:
        acc = jnp.dot(xb, w_ref[:, c0:c0 + width], preferred_element_type=F32)
        for kind in kinds:
            if kind == "f32":
                val = acc
            elif kind == "bf16":
                val = acc.astype(BF16)
            elif kind == "bf16s":
                val = (acc * 0.125).astype(BF16)
            elif kind == "bf16e":
                val = (acc * (0.125 * LOG2E)).astype(BF16)
            elif kind == "hilo":
                hi = acc.astype(BF16)
                lo = (acc - hi.astype(F32)).astype(BF16)
                lane = lax.broadcasted_iota(I32, acc.shape, 1) & (LANE - 1)
                val = jnp.where(lane < HEAD_DIM, hi, lo)
            elif kind == "wi":
                val = acc * 0.0625
            else:
                raise ValueError(kind)
            out_refs[k][...] = val
            k += 1


def _project(x, w16, specs, tm):
    n = x.shape[0]
    out_shape, out_specs = [], []
    for _, width, kinds in specs:
        for kind in kinds:
            dt = F32 if kind in ("f32", "wi") else BF16
            out_shape.append(jax.ShapeDtypeStruct((n, width), dt))
            out_specs.append(pl.BlockSpec((tm, width), lambda i: (i, 0)))
    return pl.pallas_call(
        functools.partial(_proj_kernel, specs=specs),
        grid=(pl.cdiv(n, tm),),
        in_specs=[pl.BlockSpec((tm, x.shape[1]), lambda i: (i, 0)),
                  pl.BlockSpec(w16.shape, lambda i: (0, 0))],
        out_specs=out_specs,
        out_shape=out_shape,
        compiler_params=_cparams(("parallel",)),
        name="proj",
    )(x, w16)


def _mix_ln_kernel(x_ref, *refs, n_pairs):
    o_refs = refs[:n_pairs]
    w_refs = refs[n_pairs:2 * n_pairs]
    g_ref, b_ref, y_ref = refs[2 * n_pairs:]
    acc = DN_ALPHA * x_ref[...]
    for o_ref, w_ref in zip(o_refs, w_refs):
        acc = acc + jnp.dot(o_ref[...], w_ref[...], preferred_element_type=F32)
    y_ref[...] = _layer_norm(acc, g_ref[...], b_ref[...])


def _mix_ln(x, os_, ws, g, b, tm):
    n, d = x.shape
    in_specs = [pl.BlockSpec((tm, d), lambda i: (i, 0))]
    in_specs += [pl.BlockSpec((tm, o.shape[1]), lambda i: (i, 0)) for o in os_]
    in_specs += [pl.BlockSpec(w.shape, lambda i: (0, 0)) for w in ws]
    in_specs += [pl.BlockSpec((1, d), lambda i: (0, 0))] * 2
    return pl.pallas_call(
        functools.partial(_mix_ln_kernel, n_pairs=len(os_)),
        grid=(pl.cdiv(n, tm),),
        in_specs=in_specs,
        out_specs=pl.BlockSpec((tm, d), lambda i: (i, 0)),
        out_shape=jax.ShapeDtypeStruct((n, d), F32),
        compiler_params=_cparams(("parallel",)),
        name="mix_ln",
    )(x, *os_, *ws, g.reshape(1, d), b.reshape(1, d))


def _ffn_ln_kernel(y_ref, wg_ref, wu_ref, wd_ref, g_ref, b_ref, o_ref, *, nf):
    y = y_ref[...]
    yb = y.astype(BF16)
    acc = DN_ALPHA * y
    for f in range(nf):
        hg = jnp.dot(yb, wg_ref[f], preferred_element_type=F32)
        hu = jnp.dot(yb, wu_ref[f], preferred_element_type=F32)
        h = (hg * jax.nn.sigmoid(hg) * hu).astype(BF16)
        acc = acc + jnp.dot(h, wd_ref[f], preferred_element_type=F32)
    o_ref[...] = _layer_norm(acc, g_ref[...], b_ref[...])


def _ffn_ln(y, wg, wu, wd, g, b, tm, tf):
    n, d = y.shape
    dff = wg.shape[1]
    nf = dff // tf
    wg3 = wg.astype(BF16).reshape(d, nf, tf).transpose(1, 0, 2)
    wu3 = wu.astype(BF16).reshape(d, nf, tf).transpose(1, 0, 2)
    wd3 = wd.astype(BF16).reshape(nf, tf, d)
    full3 = lambda i: (0, 0, 0)
    return pl.pallas_call(
        functools.partial(_ffn_ln_kernel, nf=nf),
        grid=(pl.cdiv(n, tm),),
        in_specs=[pl.BlockSpec((tm, d), lambda i: (i, 0)),
                  pl.BlockSpec(wg3.shape, full3), pl.BlockSpec(wu3.shape, full3),
                  pl.BlockSpec(wd3.shape, full3),
                  pl.BlockSpec((1, d), lambda i: (0, 0)), pl.BlockSpec((1, d), lambda i: (0, 0))],
        out_specs=pl.BlockSpec((tm, d), lambda i: (i, 0)),
        out_shape=jax.ShapeDtypeStruct((n, d), F32),
        compiler_params=_cparams(("parallel",)),
        name="ffn_ln",
    )(y, wg3, wu3, wd3, g.reshape(1, d), b.reshape(1, d))


def _sort_key(x):
    bits = lax.bitcast_convert_type(x, I32)
    return bits ^ ((bits >> 31) & 0x7FFFFFFF)


def _dsa_kernel(rb_ref, qi_ref, wi_ref, qa_ref, ki_ref, ka_ref, va_ref, bn_ref, o_ref,
                keys_ref, cut_ref, s_ref, p_ref, madd_ref, wbc_ref, cmax_ref, mpart_ref, alpha_ref,
                l_ref, acc_ref, *, tq, n_sel, e_base, e_step, q_base, q_step, coff, n_valid, padf):
    i = pl.program_id(1)
    ck = DSA_CK
    rbk = DSA_RB
    nt = ck // LANE
    e_end = e_base + e_step * i
    n_chunks = (e_end + ck - 1) // ck
    qpos = q_base + q_step * i + lax.broadcasted_iota(I32, (tq, 1), 0)
    bound = jnp.minimum(((((qpos + coff) >> CHUNK_SHIFT) + 1) << CHUNK_SHIFT) - coff, n_valid)
    lane_rb = lax.broadcasted_iota(I32, (rbk, ck), 1)
    lane128 = lax.broadcasted_iota(I32, (tq, LANE), 1)

    def key_rows(c):
        kpos0 = e_end - ck * (c + 1)
        return kpos0, pl.multiple_of(jnp.maximum(kpos0 + padf, 0), LANE)

    def tiles(x):
        return [x[:, t * LANE:(t + 1) * LANE] for t in range(nt)]

    qstack = jnp.concatenate([qi_ref[:, h * LANE:(h + 1) * LANE] for h in range(IDX_HEADS)], axis=0)
    for h in range(IDX_HEADS):
        wbc_ref[h] = jnp.broadcast_to(wi_ref[:, h:h + 1], (tq, LANE))
    cmax_ref[...] = jnp.full((tq, ck), -jnp.inf, F32)

    def idx_dot(c, dst):
        _, r = key_rows(c)
        s_ref[dst] = _nt(qstack, ki_ref[pl.ds(r, ck), :])

    def idx_keys(c, src):
        kpos0, _ = key_rows(c)
        for b in range(tq // rbk):
            r0 = b * rbk
            sc = None
            for h in range(IDX_HEADS):
                sh = jnp.maximum(s_ref[src, h * tq + r0:h * tq + r0 + rbk, :], 0.0)
                term = jnp.concatenate([wbc_ref[h, r0:r0 + rbk, :]] * nt, axis=1) * sh
                sc = term if sc is None else sc + term
            kpos = kpos0 + lane_rb
            allowed = (kpos >= 0) & (kpos < bound[r0:r0 + rbk])
            keys_ref[c, r0:r0 + rbk, :] = jnp.where(allowed, _sort_key(sc), INT_MIN)
            cmax_ref[r0:r0 + rbk, :] = jnp.maximum(cmax_ref[r0:r0 + rbk, :], jnp.where(allowed, sc, -jnp.inf))

    idx_dot(0, 0)

    def p1(j, carry):
        c = 2 * j
        idx_dot(c + 1, 1)
        idx_keys(c, 0)
        idx_dot(c + 2, 0)
        idx_keys(c + 1, 1)
        return carry

    lax.fori_loop(0, (n_chunks + 1) // 2, p1, 0)

    def count_ge(cand):
        def body(c, acc):
            u = keys_ref[c]
            for t in range(nt):
                acc = acc + jnp.where(u[:, t * LANE:(t + 1) * LANE] >= cand, 1.0, 0.0)
            return acc
        acc = lax.fori_loop(0, n_chunks, body, jnp.zeros((tq, LANE), F32))
        return jnp.sum(acc, axis=1, keepdims=True)

    cm = tiles(cmax_ref[...])
    if n_sel <= LANE:
        fold = [functools.reduce(jnp.maximum, cm)]
    elif n_sel <= 2 * LANE:
        fold = [jnp.maximum(cm[0], cm[2]), jnp.maximum(cm[1], cm[3])]
    else:
        fold = cm
    fmin = jnp.min(functools.reduce(jnp.minimum, fold), axis=1, keepdims=True)
    fmax = jnp.max(functools.reduce(jnp.maximum, cm), axis=1, keepdims=True)
    lo0 = _sort_key(fmin) - 1
    hi0 = _sort_key(fmax) + 2
    small = bound <= n_sel
    nbits = jnp.where(small, 0, 32 - lax.clz(lo0 ^ hi0))
    low_mask = (jnp.int32(1) << jnp.minimum(nbits, 31)) - 1
    t0 = jnp.where(small, INT_MIN + 1, jnp.where(nbits >= 32, INT_MIN, lo0 & ~low_mask))
    max_bits = jnp.max(nbits.astype(F32)).astype(I32)

    def bis(it, t):
        b = nbits - 1 - it
        cand = t + (jnp.int32(1) << jnp.maximum(b, 0))
        return jnp.where((b >= 0) & (count_ge(cand) >= n_sel), cand, t)

    thr = lax.fori_loop(0, max_bits, bis, t0)
    thr = jnp.maximum(thr, INT_MIN + 1)
    n_gt = count_ge(thr + 1)
    need = n_sel - n_gt
    excess = (count_ge(thr) - n_gt) > need

    cut_ref[...] = jnp.full((tq, LANE), 2 ** 30, I32)

    @pl.when(jnp.max(jnp.where(excess, 1.0, 0.0)) > 0.0)
    def _():
        rr = lax.broadcasted_iota(I32, (ck, ck), 0)
        cc = lax.broadcasted_iota(I32, (ck, ck), 1)
        upto = jnp.where(rr <= cc, 1.0, 0.0).astype(BF16)

        def body(j, st):
            seen, cut = st
            c = n_chunks - 1 - j
            kpos0, _ = key_rows(c)
            eq = jnp.where(keys_ref[c] == thr, 1.0, 0.0).astype(BF16)
            incl = jnp.dot(eq, upto, preferred_element_type=F32)
            below = jnp.sum(jnp.where(seen + incl < need, 1.0, 0.0), axis=1, keepdims=True)
            tot = seen + incl[:, ck - 1:ck]
            here = excess & (seen < need) & (tot >= need)
            cut = jnp.where(here, kpos0 + below.astype(I32) + 1, cut)
            return (tot, cut)

        _, cut = lax.fori_loop(0, n_chunks, body, (jnp.zeros((tq, 1), F32), jnp.full((tq, 1), 2 ** 30, I32)))
        cut_ref[...] = jnp.broadcast_to(cut, (tq, LANE))

    cut = cut_ref[:, 0:1]

    r4 = A_HEADS // A_KV_HEADS
    qg = []
    for g in range(A_KV_HEADS):
        rows = []
        for j in range(r4):
            slot = qa_ref[:, j * LANE:(j + 1) * LANE]
            half = (lane128 < HEAD_DIM) if g == 0 else (lane128 >= HEAD_DIM)
            rows.append(jnp.where(half, slot, jnp.zeros_like(slot)))
        qg.append(jnp.concatenate(rows, axis=0))
    farb = [jnp.concatenate([jnp.full((tq, 1), rb_ref[g * r4 + j], F32) for j in range(r4)], axis=0)
            for g in range(A_KV_HEADS)]
    l_ref[...] = jnp.zeros(l_ref.shape, F32)
    acc_ref[...] = jnp.zeros(acc_ref.shape, F32)

    def logits_dot(c, g):
        _, r = key_rows(c)
        s_ref[g] = _nt(qg[g], ka_ref[pl.ds(r, ck), :])

    def select_mask(c):
        kpos0, _ = key_rows(c)
        for b in range(tq // rbk):
            r0 = b * rbk
            u = keys_ref[c, r0:r0 + rbk, :]
            t_b = thr[r0:r0 + rbk]
            sel = (u > t_b) | ((u == t_b) & (kpos0 + lane_rb < cut[r0:r0 + rbk]))
            madd_ref[r0:r0 + rbk, :] = jnp.where(sel, 0.0, NEG)

    def softmax_passes(g, m_old, near):
        for b in range(r4 * tq // rbk):
            r0 = b * rbk
            q0 = r0 % tq
            sm = s_ref[g, r0:r0 + rbk, :] + madd_ref[q0:q0 + rbk, :]
            if near:
                sm = sm + bn_ref[g, r0:r0 + rbk, :]
            s_ref[g, r0:r0 + rbk, :] = sm
            mpart_ref[g, r0:r0 + rbk, :] = functools.reduce(jnp.maximum, tiles(sm))
        m_blk = jnp.max(mpart_ref[g], axis=1, keepdims=True)
        if not near:
            m_blk = m_blk + farb[g]
        m_new = jnp.maximum(m_old, m_blk)
        alpha = jnp.exp2(m_old - m_new)
        alpha_ref[g] = jnp.broadcast_to(alpha, (r4 * tq, LANE))
        shift = m_new if near else m_new - farb[g]
        for b in range(r4 * tq // rbk):
            r0 = b * rbk
            p = jnp.exp2(s_ref[g, r0:r0 + rbk, :] - shift[r0:r0 + rbk])
            l_ref[g, r0:r0 + rbk, :] = (alpha_ref[g, r0:r0 + rbk, :] * l_ref[g, r0:r0 + rbk, :]
                                        + functools.reduce(jnp.add, tiles(p)))
            p_ref[g, r0:r0 + rbk, :] = p.astype(BF16)
        return m_new

    def value_dot(c, g):
        _, r = key_rows(c)
        acc_ref[g] = alpha_ref[g] * acc_ref[g] + jnp.dot(p_ref[g], va_ref[pl.ds(r, ck), :],
                                                        preferred_element_type=F32)

    m0 = jnp.full((r4 * tq, 1), NEG, F32)
    logits_dot(0, 0)
    logits_dot(0, 1)
    select_mask(0)
    m0n = softmax_passes(0, m0, True)
    value_dot(0, 0)
    logits_dot(1, 0)
    m1n = softmax_passes(1, m0, True)

    def p3(c, ms):
        value_dot(c - 1, 1)
        logits_dot(c, 1)
        select_mask(c)
        m_a = softmax_passes(0, ms[0], False)
        value_dot(c, 0)
        logits_dot(c + 1, 0)
        m_b = softmax_passes(1, ms[1], False)
        return (m_a, m_b)

    lax.fori_loop(1, n_chunks, p3, (m0n, m1n))
    value_dot(n_chunks - 1, 1)
    outs = [acc_ref[g] / jnp.sum(l_ref[g], axis=1, keepdims=True) for g in range(A_KV_HEADS)]
    for j in range(r4):
        lo = outs[0][j * tq:(j + 1) * tq]
        hi = outs[1][j * tq:(j + 1) * tq]
        o_ref[:, j * LANE:(j + 1) * LANE] = jnp.where(lane128 < HEAD_DIM, lo, hi).astype(BF16)


def _dsa(rb_far, qi, wi, qa, ki, ka, va, bn, *, nb, nq, tq, row0, n_sel, e_base, e_step,
         q_base, q_step, coff, n_valid, padf, n_chunks_max):
    rb0 = row0 // tq
    r4 = A_HEADS // A_KV_HEADS
    assert IDX_HEADS == r4 and tq % DSA_RB == 0
    qmap = lambda b, i: (rb0 + b * nq + i, 0)
    kmap = lambda b, i: (b, 0, 0)
    kern = functools.partial(_dsa_kernel, tq=tq, n_sel=n_sel, e_base=e_base, e_step=e_step,
                             q_base=q_base, q_step=q_step, coff=coff, n_valid=n_valid, padf=padf)
    return pl.pallas_call(
        kern,
        grid=(nb, nq),
        in_specs=[pl.BlockSpec(memory_space=pltpu.SMEM),
                  pl.BlockSpec((tq, qi.shape[1]), qmap),
                  pl.BlockSpec((tq, LANE), qmap),
                  pl.BlockSpec((tq, qa.shape[1]), qmap),
                  pl.BlockSpec((None,) + ki.shape[1:], kmap),
                  pl.BlockSpec((None,) + ka.shape[1:], kmap),
                  pl.BlockSpec((None,) + va.shape[1:], kmap),
                  pl.BlockSpec(bn.shape, lambda b, i: (0, 0, 0))],
        out_specs=pl.BlockSpec((tq, qa.shape[1]), lambda b, i: (b * nq + i, 0)),
        out_shape=jax.ShapeDtypeStruct((nb * nq * tq, qa.shape[1]), BF16),
        scratch_shapes=[pltpu.VMEM((n_chunks_max + 1, tq, DSA_CK), I32),
                        pltpu.VMEM((tq, LANE), I32),
                        pltpu.VMEM((2, r4 * tq, DSA_CK), F32),
                        pltpu.VMEM((A_KV_HEADS, r4 * tq, DSA_CK), BF16),
                        pltpu.VMEM((tq, DSA_CK), F32),
                        pltpu.VMEM((IDX_HEADS, tq, LANE), F32),
                        pltpu.VMEM((tq, DSA_CK), F32),
                        pltpu.VMEM((A_KV_HEADS, r4 * tq, LANE), F32),
                        pltpu.VMEM((A_KV_HEADS, r4 * tq, LANE), F32),
                        pltpu.VMEM((A_KV_HEADS, r4 * tq, LANE), F32),
                        pltpu.VMEM((A_KV_HEADS, r4 * tq, LANE), F32)],
        compiler_params=_cparams(("parallel", "arbitrary")),
        name="dsa",
    )(rb_far, qi, wi, qa, ki, ka, va, bn)


def _sb_kernel(tri_ref, q_ref, k_ref, v_ref, o_ref, *, tq, e_base, e_step, q_base, q_step, padf, n_valid):
    i = pl.program_id(2)
    bw = SB_BAND
    e_end = e_base + e_step * i
    n_bands = (e_end + bw - 1) // bw
    qpos = q_base + q_step * i + lax.broadcasted_iota(I32, (tq, 1), 0)
    lane = lax.broadcasted_iota(I32, (tq, LANE), 1)
    q = q_ref[...]
    zero = jnp.zeros_like(q)
    qh = [jnp.where(lane < HEAD_DIM, q, zero), jnp.where(lane >= HEAD_DIM, q, zero)]
    lane_bw = lax.broadcasted_iota(I32, (tq, bw), 1)

    def cond(st):
        return (st[0] < n_bands) & (st[1] > SB_STOP)

    def body(st):
        m, _, carry, acc = st
        kpos0 = e_end - bw * (m + 1)
        r = pl.multiple_of(kpos0 + padf, LANE)
        kt = k_ref[pl.ds(r, bw), :]
        vt = v_ref[pl.ds(r, bw), :]
        kpos = kpos0 + lane_bw
        before = (kpos < qpos) & (kpos >= 0) & (kpos < n_valid)
        zs, lss, lks, parts = [], [], [], []
        for h in range(2):
            z = _nt(qh[h], kt)
            ls = -(jnp.maximum(z, 0.0) + jnp.log1p(jnp.exp(-jnp.abs(z))))
            lk = jnp.where(before, ls, 0.0)
            hi = lk.astype(BF16)
            parts += [hi, (lk - hi.astype(F32)).astype(BF16)]
            zs.append(z)
            lss.append(ls)
            lks.append(lk)
        sums = jnp.dot(jnp.concatenate(parts, axis=0), tri_ref[...], preferred_element_type=F32)
        new_c, new_a = [], []
        worst = jnp.float32(-jnp.inf)
        for h in range(2):
            bl = sums[2 * h * tq:(2 * h + 1) * tq] + sums[(2 * h + 1) * tq:(2 * h + 2) * tq]
            w = jnp.where(before, jnp.exp(lss[h] + zs[h] + bl + carry[h]), 0.0)
            new_a.append(acc[h] + jnp.dot(w.astype(BF16), vt, preferred_element_type=F32))
            c_n = carry[h] + bl[:, 0:1] + lks[h][:, 0:1]
            new_c.append(c_n)
            worst = jnp.maximum(worst, jnp.max(c_n))
        return (m + 1, worst, tuple(new_c), tuple(new_a))

    init = (jnp.int32(0), jnp.float32(0.0),
            (jnp.zeros((tq, 1), F32), jnp.zeros((tq, 1), F32)),
            (jnp.zeros((tq, LANE), F32), jnp.zeros((tq, LANE), F32)))
    _, _, _, acc = lax.while_loop(cond, body, init)
    o_ref[...] = jnp.where(lane < HEAD_DIM, acc[0], acc[1]).astype(BF16)


def _stick_break(q, k, v, *, nb, nq, tq, row0, e_base, e_step, q_base, q_step, padf, n_valid):
    rb0 = row0 // tq
    npair = q.shape[1] // LANE
    kern = functools.partial(_sb_kernel, tq=tq, e_base=e_base, e_step=e_step, q_base=q_base,
                             q_step=q_step, padf=padf, n_valid=n_valid)
    rr = lax.broadcasted_iota(I32, (SB_BAND, SB_BAND), 0)
    cc = lax.broadcasted_iota(I32, (SB_BAND, SB_BAND), 1)
    tri = jnp.where(rr > cc, 1.0, 0.0).astype(BF16)
    kspec = pl.BlockSpec((None, k.shape[1], LANE), lambda b, p, i: (b, 0, p))
    return pl.pallas_call(
        kern,
        grid=(nb, npair, nq),
        in_specs=[pl.BlockSpec(tri.shape, lambda b, p, i: (0, 0)),
                  pl.BlockSpec((tq, LANE), lambda b, p, i: (rb0 + b * nq + i, p)), kspec, kspec],
        out_specs=pl.BlockSpec((tq, LANE), lambda b, p, i: (b * nq + i, p)),
        out_shape=jax.ShapeDtypeStruct((nb * nq * tq, q.shape[1]), BF16),
        compiler_params=_cparams(("parallel", "parallel", "arbitrary")),
        name="stick_break",
    )(tri, q, k, v)


def _swa_kernel(sink_ref, q_ref, k_ref, v_ref, bias_ref, o_ref, *, tq, ck, r_base, r_step,
                p_base, p_step, q_base, q_step, coff, n_valid):
    i = pl.program_id(1)
    r0 = pl.multiple_of(r_base + r_step * i, 16)
    kpos0 = p_base + p_step * i
    kt = k_ref[pl.ds(r0, ck), :]
    vt = v_ref[pl.ds(r0, ck), :]
    qpos = q_base + q_step * i + lax.broadcasted_iota(I32, (tq, 1), 0)
    kpos = kpos0 + lax.broadcasted_iota(I32, (tq, ck), 1)
    qc = (qpos + coff) >> CHUNK_SHIFT
    hi_b = jnp.minimum(((qc + 1) << CHUNK_SHIFT) - coff, n_valid)
    lo_b = jnp.maximum(((qc - WIN_CHUNKS) << CHUNK_SHIFT) - coff, 0)
    allowed = (kpos >= lo_b) & (kpos < hi_b)
    r8 = C_HEADS // C_KV_HEADS
    allowed8 = jnp.concatenate([allowed] * r8, axis=0)
    lane = lax.broadcasted_iota(I32, (tq, LANE), 1)
    outs = []
    for g in range(C_KV_HEADS):
        rows = []
        for j in range(r8):
            slot = q_ref[:, j * LANE:(j + 1) * LANE]
            half = (lane < HEAD_DIM) if g == 0 else (lane >= HEAD_DIM)
            rows.append(jnp.where(half, slot, jnp.zeros_like(slot)))
        qs = jnp.concatenate(rows, axis=0)
        sink = jnp.concatenate([jnp.full((tq, 1), sink_ref[g * r8 + j], F32) for j in range(r8)], axis=0)
        s = _nt(qs, kt) + bias_ref[g]
        s = jnp.where(allowed8, s, NEG)
        m = jnp.maximum(jnp.max(s, axis=1, keepdims=True), sink)
        e = jnp.exp(s - m)
        den = jnp.sum(e, axis=1, keepdims=True) + jnp.exp(sink - m)
        p = (e / den).astype(BF16)
        outs.append(jnp.dot(p, vt, preferred_element_type=F32))
    for j in range(r8):
        lo = outs[0][j * tq:(j + 1) * tq]
        hi = outs[1][j * tq:(j + 1) * tq]
        o_ref[:, j * LANE:(j + 1) * LANE] = jnp.where(lane < HEAD_DIM, lo, hi).astype(BF16)


def _swa(sinks, q, k, v, bias, *, nb, nq, tq, row0, ck, r_base, r_step, p_base, p_step,
         q_base, q_step, coff, n_valid):
    rb0 = row0 // tq
    kern = functools.partial(_swa_kernel, tq=tq, ck=ck, r_base=r_base, r_step=r_step, p_base=p_base,
                             p_step=p_step, q_base=q_base, q_step=q_step, coff=coff, n_valid=n_valid)
    kmap = lambda b, i: (b, 0, 0)
    return pl.pallas_call(
        kern,
        grid=(nb, nq),
        in_specs=[pl.BlockSpec(memory_space=pltpu.SMEM),
                  pl.BlockSpec((tq, q.shape[1]), lambda b, i: (rb0 + b * nq + i, 0)),
                  pl.BlockSpec((None,) + k.shape[1:], kmap),
                  pl.BlockSpec((None,) + v.shape[1:], kmap),
                  pl.BlockSpec(bias.shape, lambda b, i: (0, 0, 0))],
        out_specs=pl.BlockSpec((tq, q.shape[1]), lambda b, i: (b * nq + i, 0)),
        out_shape=jax.ShapeDtypeStruct((nb * nq * tq, q.shape[1]), BF16),
        compiler_params=_cparams(("parallel", "arbitrary")),
        name="swa",
    )(sinks, q, k, v, bias)


def _router_kernel(y_ref, rh_ref, rl_ref, idx_ref, gate_ref):
    y = y_ref[...]
    yh = y.astype(BF16)
    yl = (y - yh.astype(F32)).astype(BF16)
    rh = rh_ref[...]
    logits = (jnp.dot(yh, rh, preferred_element_type=F32) + jnp.dot(yl, rh, preferred_element_type=F32)
              + jnp.dot(yh, rl_ref[...], preferred_element_type=F32))
    lane = lax.broadcasted_iota(I32, logits.shape, 1)
    logits = jnp.where(lane < N_EXPERTS, logits, -jnp.inf)
    m1 = jnp.max(logits, axis=1, keepdims=True)
    i1 = jnp.min(jnp.where(logits == m1, lane, LANE), axis=1, keepdims=True)
    rest = jnp.where(lane == i1, -jnp.inf, logits)
    m2 = jnp.max(rest, axis=1, keepdims=True)
    i2 = jnp.min(jnp.where(rest == m2, lane, LANE), axis=1, keepdims=True)
    e2 = jnp.exp(m2 - m1)
    den = 1.0 + e2
    idx_ref[...] = jnp.where(lane == 0, i1, jnp.where(lane == 1, i2, 0))
    gate_ref[...] = jnp.where(lane == 0, 1.0 / den, jnp.where(lane == 1, e2 / den, 0.0))


def _router(y, router, tm):
    n, d = y.shape
    rpad = jnp.pad(router.astype(F32), ((0, 0), (0, LANE - router.shape[1])))
    rh = rpad.astype(BF16)
    rl = (rpad - rh.astype(F32)).astype(BF16)
    return pl.pallas_call(
        _router_kernel,
        grid=(pl.cdiv(n, tm),),
        in_specs=[pl.BlockSpec((tm, d), lambda i: (i, 0)),
                  pl.BlockSpec((d, LANE), lambda i: (0, 0)), pl.BlockSpec((d, LANE), lambda i: (0, 0))],
        out_specs=[pl.BlockSpec((tm, LANE), lambda i: (i, 0)), pl.BlockSpec((tm, LANE), lambda i: (i, 0))],
        out_shape=[jax.ShapeDtypeStruct((n, LANE), I32), jax.ShapeDtypeStruct((n, LANE), F32)],
        compiler_params=_cparams(("parallel",)),
        name="router",
    )(y, rh, rl)


def _moe_kernel(be_ref, nu_ref, x_ref, wg_ref, wu_ref, wd_ref, o_ref, acc_ref, xb_ref):
    t = pl.program_id(0)
    f = pl.program_id(1)

    @pl.when(f == 0)
    def _():
        acc_ref[...] = jnp.zeros_like(acc_ref)
        xb_ref[...] = x_ref[...].astype(BF16)

    @pl.when(t < nu_ref[0])
    def _():
        xb = xb_ref[...]
        hg = jnp.dot(xb, wg_ref[...], preferred_element_type=F32)
        hu = jnp.dot(xb, wu_ref[...], preferred_element_type=F32)
        h = (hg * jax.nn.sigmoid(hg) * hu).astype(BF16)
        acc_ref[...] += jnp.dot(h, wd_ref[...], preferred_element_type=F32)

    @pl.when(f == pl.num_programs(1) - 1)
    def _():
        o_ref[...] = acc_ref[...]


def _moe_experts(block_expert, n_used, xs, wg, wu, wd, tmb, tf):
    n_slots, d = xs.shape
    de = wg.shape[2]
    grid_spec = pltpu.PrefetchScalarGridSpec(
        num_scalar_prefetch=2,
        grid=(n_slots // tmb, de // tf),
        in_specs=[pl.BlockSpec((tmb, d), lambda t, f, be, nu: (t, 0)),
                  pl.BlockSpec((None, d, tf), lambda t, f, be, nu: (be[t], 0, f)),
                  pl.BlockSpec((None, d, tf), lambda t, f, be, nu: (be[t], 0, f)),
                  pl.BlockSpec((None, tf, d), lambda t, f, be, nu: (be[t], f, 0))],
        out_specs=pl.BlockSpec((tmb, d), lambda t, f, be, nu: (t, 0)),
        scratch_shapes=[pltpu.VMEM((tmb, d), F32), pltpu.VMEM((tmb, d), BF16)],
    )
    return pl.pallas_call(
        _moe_kernel,
        grid_spec=grid_spec,
        out_shape=jax.ShapeDtypeStruct((n_slots, d), F32),
        compiler_params=_cparams(("parallel", "arbitrary")),
        name="moe_experts",
    )(block_expert, n_used, xs, wg, wu, wd)


def _row_copy(src_hbm, src_row, dst, dst_row, sem):
    return pltpu.make_async_copy(src_hbm.at[pl.ds(src_row, 1)], dst.at[pl.ds(dst_row, 1)], sem)


def _dispatch_kernel(slot_ref, x_ref, init_hbm, xs_hbm, sem, *, tt):
    del init_hbm

    def start(j, carry):
        for k in range(TOP_K):
            _row_copy(x_ref, j, xs_hbm, slot_ref[0, 0, TOP_K * j + k], sem).start()
        return carry

    def wait(j, carry):
        _row_copy(x_ref, 0, xs_hbm, 0, sem).wait()
        return carry

    lax.fori_loop(0, tt, start, 0, unroll=8)
    lax.fori_loop(0, TOP_K * tt, wait, 0, unroll=8)


def _dispatch(x, slot3, n_slots, tt):
    n, d = x.shape
    init = jnp.zeros((n_slots, d), x.dtype)
    return pl.pallas_call(
        functools.partial(_dispatch_kernel, tt=tt),
        grid=(n // tt,),
        in_specs=[pl.BlockSpec((1, 1, TOP_K * tt), lambda i: (i, 0, 0), memory_space=pltpu.SMEM),
                  pl.BlockSpec((tt, d), lambda i: (i, 0)), pl.BlockSpec(memory_space=pl.ANY)],
        out_specs=pl.BlockSpec(memory_space=pl.ANY),
        out_shape=jax.ShapeDtypeStruct((n_slots, d), x.dtype),
        scratch_shapes=[pltpu.SemaphoreType.DMA],
        input_output_aliases={2: 0},
        compiler_params=_cparams(("arbitrary",)),
        name="moe_dispatch",
    )(slot3, x, init)


def _combine_ln_kernel(slot_ref, y_ref, gate_ref, g_ref, b_ref, outs_hbm, o_ref, buf_ref, sem, *, tt):
    def start(j, carry):
        for k in range(TOP_K):
            _row_copy(outs_hbm, slot_ref[0, 0, TOP_K * j + k], buf_ref.at[k], j, sem).start()
        return carry

    def wait(j, carry):
        _row_copy(outs_hbm, 0, buf_ref.at[0], 0, sem).wait()
        return carry

    lax.fori_loop(0, tt, start, 0, unroll=8)
    lax.fori_loop(0, TOP_K * tt, wait, 0, unroll=8)
    mo = gate_ref[:, 0:1] * buf_ref[0]
    for k in range(1, TOP_K):
        mo = mo + gate_ref[:, k:k + 1] * buf_ref[k]
    o_ref[...] = _layer_norm(DN_ALPHA * y_ref[...] + mo, g_ref[...], b_ref[...])


def _combine_ln(y, outs, slot3, gate_p, g, b, tt):
    n, d = y.shape
    row = pl.BlockSpec((tt, d), lambda i: (i, 0))
    vec = pl.BlockSpec((1, d), lambda i: (0, 0))
    return pl.pallas_call(
        functools.partial(_combine_ln_kernel, tt=tt),
        grid=(n // tt,),
        in_specs=[pl.BlockSpec((1, 1, TOP_K * tt), lambda i: (i, 0, 0), memory_space=pltpu.SMEM),
                  row, pl.BlockSpec((tt, LANE), lambda i: (i, 0)), vec, vec,
                  pl.BlockSpec(memory_space=pl.ANY)],
        out_specs=row,
        out_shape=jax.ShapeDtypeStruct((n, d), F32),
        scratch_shapes=[pltpu.VMEM((TOP_K, tt, d), F32), pltpu.SemaphoreType.DMA],
        compiler_params=_cparams(("arbitrary",)),
        name="moe_combine_ln",
    )(slot3, y, gate_p, g.reshape(1, d), b.reshape(1, d), outs)


def _moe(y, router, w_gate, w_up, w_down, g, b, tm, tmb, tf, tt):
    n, d = y.shape
    assert n % tt == 0
    idx_p, gate_p = _router(y, router, tm)
    e_flat = idx_p[:, :TOP_K].reshape(-1)
    onehot = (e_flat[:, None] == jnp.arange(N_EXPERTS, dtype=I32)[None, :]).astype(I32)
    rank = jnp.sum((jnp.cumsum(onehot, axis=0) - onehot) * onehot, axis=1)
    counts = jnp.sum(onehot, axis=0)
    padded = (counts + tmb - 1) // tmb * tmb
    pad_end = jnp.cumsum(padded)
    slot = ((pad_end - padded)[e_flat] + rank).astype(I32)
    n_blocks = -(-(TOP_K * n) // tmb) + N_EXPERTS
    n_slots = n_blocks * tmb
    block_expert = jnp.minimum(
        jnp.searchsorted(pad_end, jnp.arange(n_blocks, dtype=I32) * tmb, side="right"),
        N_EXPERTS - 1).astype(I32)
    n_used = (pad_end[-1] // tmb).astype(I32).reshape(1)
    slot3 = slot.reshape(n // tt, 1, TOP_K * tt)
    xs = _dispatch(y, slot3, n_slots, tt)
    outs = _moe_experts(block_expert, n_used, xs, w_gate.astype(BF16), w_up.astype(BF16),
                        w_down.astype(BF16), tmb, tf)
    return _combine_ln(y, outs, slot3, gate_p, g, b, tt)


def _t5_bucket(rel):
    half = NUM_BUCKETS // 2
    exact = half // 2
    n = jnp.abs(rel)
    far = exact + sum((n >= t).astype(I32) for t in (12, 16, 23, 32, 46, 64, 91))
    return jnp.where(rel > 0, half, 0) + jnp.where(n < exact, n, far)


def _bias_tile(rel_bias, n_groups, per_group, tq, ck, d0):
    d = d0 + jnp.arange(ck, dtype=I32)[None, :] - jnp.arange(tq, dtype=I32)[:, None]
    tile = rel_bias.astype(F32)[_t5_bucket(d)]
    tile = jnp.transpose(tile[:, :, :n_groups * per_group], (2, 0, 1))
    return tile.reshape(n_groups, per_group * tq, ck)


def _cols(w, ranges):
    parts = []
    for r in ranges:
        if isinstance(r, int):
            parts.append(jnp.zeros((w.shape[0], r), w.dtype))
        else:
            parts.append(w[:, r[0]:r[1]])
    return jnp.concatenate(parts, axis=1)


def _head_pair_order(n_heads):
    half = n_heads // 2
    order = []
    for j in range(half):
        order += [j, half + j]
    return order


L0_SPECS = ((0, 512, ("bf16e",)), (512, 128, ("f32", "bf16")), (640, 128, ("f32", "bf16")),
            (768, 512, ("hilo",)), (1280, 128, ("f32", "bf16")), (1408, 128, ("wi",)),
            (1536, 512, ("bf16s",)), (2048, 512, ("f32", "bf16")), (2560, 512, ("f32", "bf16")))
L1_SPECS = ((0, 1024, ("bf16s",)), (1024, 128, ("f32", "bf16")), (1152, 128, ("f32", "bf16")))


def _l0_weight(w_in):
    hd = HEAD_DIM
    rng = [(hd * h, hd * h + hd) for h in _head_pair_order(A_HEADS)]
    rng += [(512, 640), (640, 768)]
    for h in range(IDX_HEADS):
        rng += [(768 + hd * h, 768 + hd * h + hd)] * 2
    rng += [(1024, 1088)] * 2
    rng += [(1088, 1092), LANE - IDX_HEADS]
    rng += [(1092, 1604), (1604, 2116), (2116, 2628)]
    return _cols(w_in, rng).astype(BF16)


def _l1_weight(w_in):
    hd = HEAD_DIM
    rng = [(hd * h, hd * h + hd) for h in _head_pair_order(C_HEADS)]
    rng += [(1024, 1152), (1152, 1280)]
    return _cols(w_in, rng).astype(BF16)


def _perm_rows(w, n_heads):
    return jnp.concatenate([w[HEAD_DIM * h:HEAD_DIM * (h + 1)] for h in _head_pair_order(n_heads)], axis=0)


def _front_pad(a, nb, rows_in, front, rows_out):
    a = a.reshape(nb, rows_in, a.shape[-1])
    return jnp.pad(a, ((0, 0), (front, rows_out - front - rows_in), (0, 0)))


def _layer0_attention(x_prompt, x_sample, cache_a_k, cache_a_v, cache_a_idx_k, cache_b_k, cache_b_v,
                      meta_tokens, rel_bias, l0_w_in):
    nb, seq, d = x_prompt.shape
    t = N_META + seq
    nq = -(-t // QBLK)
    tp = nq * QBLK
    db, ds, _ = x_sample.shape
    past = cache_a_k.shape[1]
    assert d == D_MODEL and past % LANE == 0 and ds % DSA_RB == 0 and ds <= 64
    np_, ns = nb * tp, db * ds
    tm = 512

    meta = jnp.broadcast_to(meta_tokens[None].astype(x_prompt.dtype), (nb, N_META, d))
    hp = jnp.concatenate([meta, x_prompt, jnp.zeros((nb, tp - t, d), x_prompt.dtype)], axis=1)
    x_all = jnp.concatenate([hp.reshape(np_, d), x_sample.reshape(ns, d)], axis=0)
    rel_bias = rel_bias.astype(F32)
    rb_far = rel_bias[NUM_BUCKETS // 2 - 1] * LOG2E

    (qa16, ka32, ka16, va32, va16, qi16, ki32, ki16, wi32, qb16, kb32, kb16, vb32, vb16) = _project(
        x_all, _l0_weight(l0_w_in), L0_SPECS, tm)

    n_sel_p = min(TOPK_MAX, (t - N_META) // 4)
    kr = QBLK * (nq + 1) + DSA_PADF
    ncm = -(-(QBLK * (nq + 1)) // DSA_CK)
    pk = [_front_pad(a[:np_], nb, tp, DSA_PADF, kr) for a in (ki16, ka16, va16)]
    bn_p = _bias_tile(rel_bias, A_KV_HEADS, A_HEADS // A_KV_HEADS, QBLK, DSA_CK, -(DSA_CK // 2)) * LOG2E
    oa_p = _dsa(rb_far, qi16, wi32, qa16, *pk, bn_p, nb=nb, nq=nq, tq=QBLK, row0=0, n_sel=n_sel_p,
                e_base=2 * QBLK, e_step=QBLK, q_base=0, q_step=QBLK, coff=PROMPT_COFF,
                n_valid=t, padf=DSA_PADF, n_chunks_max=ncm)

    n_keys = past + ds
    n_sel_s = min(TOPK_MAX, n_keys // 4)
    e_s = -(-n_keys // LANE) * LANE
    ncs = -(-e_s // DSA_CK)
    krs = ncs * DSA_CK
    fs = krs - e_s

    def cat_keys(cache, new, width_dup):
        c = cache.reshape(db, past, -1).astype(BF16)
        if width_dup:
            c = jnp.concatenate([c, c], axis=-1)
        a = jnp.concatenate([c, new[np_:].reshape(db, ds, -1)], axis=1)
        return jnp.pad(a, ((0, 0), (fs, krs - fs - n_keys), (0, 0)))

    sk = [cat_keys(cache_a_idx_k, ki16, True), cat_keys(cache_a_k, ka16, False),
          cat_keys(cache_a_v, va16, False)]
    bn_s = _bias_tile(rel_bias, A_KV_HEADS, A_HEADS // A_KV_HEADS, ds, DSA_CK, e_s - DSA_CK - past) * LOG2E
    oa_s = _dsa(rb_far, qi16, wi32, qa16, *sk, bn_s, nb=db, nq=1, tq=ds, row0=np_, n_sel=n_sel_s,
                e_base=e_s, e_step=0, q_base=past, q_step=0, coff=0, n_valid=n_keys, padf=fs,
                n_chunks_max=ncs)

    pfb = SB_BAND - QBLK
    kb_p = _front_pad(kb16[:np_], nb, tp, pfb, tp + pfb)
    vb_p = _front_pad(vb16[:np_], nb, tp, pfb, tp + pfb)
    ob_p = _stick_break(qb16, kb_p, vb_p, nb=nb, nq=nq, tq=QBLK, row0=0, e_base=QBLK, e_step=QBLK,
                        q_base=0, q_step=QBLK, padf=pfb, n_valid=t)
    rows_s = -(-e_s // SB_BAND) * SB_BAND
    pfs = rows_s - e_s

    def cat_b(cache, new):
        a = jnp.concatenate([cache.reshape(db, past, -1).astype(BF16), new[np_:].reshape(db, ds, -1)], axis=1)
        return jnp.pad(a, ((0, 0), (pfs, rows_s - pfs - n_keys), (0, 0)))

    ob_s = _stick_break(qb16, cat_b(cache_b_k, kb16), cat_b(cache_b_v, vb16), nb=db, nq=1, tq=ds,
                        row0=np_, e_base=e_s, e_step=0, q_base=past, q_step=0, padf=pfs, n_valid=n_keys)

    oa = jnp.concatenate([oa_p, oa_s], axis=0)
    ob = jnp.concatenate([ob_p, ob_s], axis=0)
    return dict(x_all=x_all, oa=oa, ob=ob, np=np_, tp=tp, t=t, nq=nq, rel_bias=rel_bias,
                ka32=ka32, va32=va32, ki32=ki32, kb32=kb32, vb32=vb32)


def kernel(x_prompt, x_sample, cache_a_k, cache_a_v, cache_a_idx_k, cache_b_k, cache_b_v, cache_c_k, cache_c_v, meta_tokens, rel_bias, l0_w_in, l0_w_out, l0_ln1_g, l0_ln1_b, l0_w_gate, l0_w_up, l0_w_down, l0_ln2_g, l0_ln2_b, l1_w_in, l1_sinks, l1_w_out, l1_ln1_g, l1_ln1_b, l1_router, l1_w_gate, l1_w_up, l1_w_down, l1_ln2_g, l1_ln2_b):
    a0 = _layer0_attention(x_prompt, x_sample, cache_a_k, cache_a_v, cache_a_idx_k, cache_b_k, cache_b_v,
                           meta_tokens, rel_bias, l0_w_in)
    x_all, oa, ob, np_, tp, t, nq, rel_bias = (a0[k] for k in ("x_all", "oa", "ob", "np", "tp", "t", "nq", "rel_bias"))
    ka32, va32, ki32, kb32, vb32 = (a0[k] for k in ("ka32", "va32", "ki32", "kb32", "vb32"))
    nb, _, d = x_prompt.shape
    db, ds, _ = x_sample.shape
    past = cache_a_k.shape[1]
    tm = 512
    w_out0 = l0_w_out.astype(BF16)
    half0 = A_HEADS * HEAD_DIM
    y0 = _mix_ln(x_all, [oa, ob], [_perm_rows(w_out0[:half0], A_HEADS), w_out0[half0:]],
                 l0_ln1_g, l0_ln1_b, tm)
    h1 = _ffn_ln(y0, l0_w_gate, l0_w_up, l0_w_down, l0_ln2_g, l0_ln2_b, tm, 256)

    qc16, kc32, kc16, vc32, vc16 = _project(h1, _l1_weight(l1_w_in), L1_SPECS, tm)
    ckp = 4 * QBLK
    krc = QBLK * (nq - 1) + ckp
    kc_p = _front_pad(kc16[:np_], nb, tp, 2 * QBLK, max(krc, tp + 2 * QBLK))
    vc_p = _front_pad(vc16[:np_], nb, tp, 2 * QBLK, max(krc, tp + 2 * QBLK))
    bc_p = _bias_tile(rel_bias, C_KV_HEADS, C_HEADS // C_KV_HEADS, QBLK, ckp, -2 * QBLK)
    sinks = l1_sinks.astype(F32)
    oc_p = _swa(sinks, qc16, kc_p, vc_p, bc_p, nb=nb, nq=nq, tq=QBLK, row0=0, ck=ckp, r_base=0,
                r_step=QBLK, p_base=-2 * QBLK, p_step=QBLK, q_base=0, q_step=QBLK, coff=PROMPT_COFF, n_valid=t)

    buf = cache_c_k.shape[1]
    cks = -(-(buf + ds) // LANE) * LANE

    def cat_c(cache, new):
        a = jnp.concatenate([cache.reshape(db, buf, -1).astype(BF16), new[np_:].reshape(db, ds, -1)], axis=1)
        return jnp.pad(a, ((0, 0), (0, cks - buf - ds), (0, 0)))

    bc_s = _bias_tile(rel_bias, C_KV_HEADS, C_HEADS // C_KV_HEADS, ds, cks, -buf)
    oc_s = _swa(sinks, qc16, cat_c(cache_c_k, kc16), cat_c(cache_c_v, vc16), bc_s, nb=db, nq=1, tq=ds,
                row0=np_, ck=cks, r_base=0, r_step=0, p_base=past - buf, p_step=0, q_base=past,
                q_step=0, coff=0, n_valid=past + ds)
    oc = jnp.concatenate([oc_p, oc_s], axis=0)
    y1 = _mix_ln(h1, [oc], [_perm_rows(l1_w_out.astype(BF16), C_HEADS)], l1_ln1_g, l1_ln1_b, tm)
    n_all = y1.shape[0]
    tt = next(c for c in (256, 128, 64, 32, 16, 8) if n_all % c == 0)
    h2 = _moe(y1, l1_router, l1_w_gate, l1_w_up, l1_w_down, l1_ln2_g, l1_ln2_b, tm, 1024, 512, tt)

    def pr(a, heads):
        a = a[:np_].reshape(nb, tp, -1)[:, :t]
        return a.reshape(nb, t, heads, HEAD_DIM) if heads else a[..., :HEAD_DIM]

    def sm(a, heads):
        a = a[np_:].reshape(db, ds, -1)
        return a.reshape(db, ds, heads, HEAD_DIM) if heads else a[..., :HEAD_DIM]

    y_prompt = h2[:np_].reshape(nb, tp, d)[:, N_META:t]
    y_sample = h2[np_:].reshape(db, ds, d)
    bufp = min(WINDOW, t)
    p_ck = pr(kc32, C_KV_HEADS)[:, t - bufp:]
    p_cv = pr(vc32, C_KV_HEADS)[:, t - bufp:]
    s_ck = jnp.concatenate([cache_c_k, sm(kc32, C_KV_HEADS)], axis=1)[:, ds:]
    s_cv = jnp.concatenate([cache_c_v, sm(vc32, C_KV_HEADS)], axis=1)[:, ds:]
    return (y_prompt, y_sample,
            pr(ka32, A_KV_HEADS), pr(va32, A_KV_HEADS), pr(ki32, 0), pr(kb32, B_HEADS), pr(vb32, B_HEADS),
            p_ck, p_cv,
            sm(ka32, A_KV_HEADS), sm(va32, A_KV_HEADS), sm(ki32, 0), sm(kb32, B_HEADS), sm(vb32, B_HEADS),
            s_ck, s_cv)
```

```python
import functools

import jax
import jax.numpy as jnp
from jax import lax
from jax.experimental import pallas as pl
from jax.experimental.pallas import tpu as pltpu

F32 = jnp.float32
BF16 = jnp.bfloat16
I32 = jnp.int32

D_MODEL = 1024
CHUNK_SHIFT = 6
N_META = 16
HEAD_DIM = 64
A_HEADS = 8
A_KV_HEADS = 2
IDX_HEADS = 4
TOPK_MAX = 256
B_HEADS = 8
C_HEADS = 16
C_KV_HEADS = 2
WINDOW = 128
WIN_CHUNKS = 2
NUM_BUCKETS = 32
N_EXPERTS = 8
TOP_K = 2
LN_EPS = 1e-5
DEPTH = 2
DN_ALPHA = (2.0 * DEPTH) ** 0.25

LANE = 128
QBLK = 128
DSA_CK = 512
DSA_PADF = 384
DSA_RB = 32
LOG2E = 1.4426950408889634
SB_BAND = 512
SB_STOP = -120.0
NEG = -1e30
PROMPT_COFF = 64 - N_META
INT_MIN = -2 ** 31
VMEM_LIMIT = 56 * 1024 * 1024


def _cparams(sem):
    return pltpu.CompilerParams(dimension_semantics=sem, vmem_limit_bytes=VMEM_LIMIT)


def _nt(a, b):
    return lax.dot_general(a, b, (((1,), (1,)), ((), ())), preferred_element_type=F32)


def _layer_norm(v, g, b):
    mu = jnp.mean(v, axis=-1, keepdims=True)
    c = v - mu
    var = jnp.mean(c * c, axis=-1, keepdims=True)
    return c * lax.rsqrt(var + LN_EPS) * g + b


def _proj_kernel(x_ref, w_ref, *out_refs, specs):
    xb = x_ref[...].astype(BF16)
    k = 0
    for c0, width, kinds in specs:
        acc = jnp.dot(xb, w_ref[:, c0:c0 + width], preferred_element_type=F32)
        for kind in kinds:
            if kind == "f32":
                val = acc
            elif kind == "bf16":
                val = acc.astype(BF16)
            elif kind == "bf16s":
                val = (acc * 0.125).astype(BF16)
            elif kind == "bf16e":
                val = (acc * (0.125 * LOG2E)).astype(BF16)
            elif kind == "hilo":
                hi = acc.astype(BF16)
                lo = (acc - hi.astype(F32)).astype(BF16)
                lane = lax.broadcasted_iota(I32, acc.shape, 1) & (LANE - 1)
                val = jnp.where(lane < HEAD_DIM, hi, lo)
            elif kind == "wi":
                val = acc * 0.0625
            else:
                raise ValueError(kind)
            out_refs[k][...] = val
            k += 1


def _project(x, w16, specs, tm):
    n = x.shape[0]
    out_shape, out_specs = [], []
    for _, width, kinds in specs:
        for kind in kinds:
            dt = F32 if kind in ("f32", "wi") else BF16
            out_shape.append(jax.ShapeDtypeStruct((n, width), dt))
            out_specs.append(pl.BlockSpec((tm, width), lambda i: (i, 0)))
    return pl.pallas_call(
        functools.partial(_proj_kernel, specs=specs),
        grid=(pl.cdiv(n, tm),),
        in_specs=[pl.BlockSpec((tm, x.shape[1]), lambda i: (i, 0)),
                  pl.BlockSpec(w16.shape, lambda i: (0, 0))],
        out_specs=out_specs,
        out_shape=out_shape,
        compiler_params=_cparams(("parallel",)),
        name="proj",
    )(x, w16)


def _mix_ln_kernel(x_ref, *refs, n_pairs):
    o_refs = refs[:n_pairs]
    w_refs = refs[n_pairs:2 * n_pairs]
    g_ref, b_ref, y_ref = refs[2 * n_pairs:]
    acc = DN_ALPHA * x_ref[...]
    for o_ref, w_ref in zip(o_refs, w_refs):
        acc = acc + jnp.dot(o_ref[...], w_ref[...], preferred_element_type=F32)
    y_ref[...] = _layer_norm(acc, g_ref[...], b_ref[...])


def _mix_ln(x, os_, ws, g, b, tm):
    n, d = x.shape
    in_specs = [pl.BlockSpec((tm, d), lambda i: (i, 0))]
    in_specs += [pl.BlockSpec((tm, o.shape[1]), lambda i: (i, 0)) for o in os_]
    in_specs += [pl.BlockSpec(w.shape, lambda i: (0, 0)) for w in ws]
    in_specs += [pl.BlockSpec((1, d), lambda i: (0, 0))] * 2
    return pl.pallas_call(
        functools.partial(_mix_ln_kernel, n_pairs=len(os_)),
        grid=(pl.cdiv(n, tm),),
        in_specs=in_specs,
        out_specs=pl.BlockSpec((tm, d), lambda i: (i, 0)),
        out_shape=jax.ShapeDtypeStruct((n, d), F32),
        compiler_params=_cparams(("parallel",)),
        name="mix_ln",
    )(x, *os_, *ws, g.reshape(1, d), b.reshape(1, d))


def _ffn_ln_kernel(y_ref, wg_ref, wu_ref, wd_ref, g_ref, b_ref, o_ref, *, nf):
    y = y_ref[...]
    yb = y.astype(BF16)
    acc = DN_ALPHA * y
    for f in range(nf):
        hg = jnp.dot(yb, wg_ref[f], preferred_element_type=F32)
        hu = jnp.dot(yb, wu_ref[f], preferred_element_type=F32)
        h = (hg * jax.nn.sigmoid(hg) * hu).astype(BF16)
        acc = acc + jnp.dot(h, wd_ref[f], preferred_element_type=F32)
    o_ref[...] = _layer_norm(acc, g_ref[...], b_ref[...])


def _ffn_ln(y, wg, wu, wd, g, b, tm, tf):
    n, d = y.shape
    dff = wg.shape[1]
    nf = dff // tf
    wg3 = wg.astype(BF16).reshape(d, nf, tf).transpose(1, 0, 2)
    wu3 = wu.astype(BF16).reshape(d, nf, tf).transpose(1, 0, 2)
    wd3 = wd.astype(BF16).reshape(nf, tf, d)
    full3 = lambda i: (0, 0, 0)
    return pl.pallas_call(
        functools.partial(_ffn_ln_kernel, nf=nf),
        grid=(pl.cdiv(n, tm),),
        in_specs=[pl.BlockSpec((tm, d), lambda i: (i, 0)),
                  pl.BlockSpec(wg3.shape, full3), pl.BlockSpec(wu3.shape, full3),
                  pl.BlockSpec(wd3.shape, full3),
                  pl.BlockSpec((1, d), lambda i: (0, 0)), pl.BlockSpec((1, d), lambda i: (0, 0))],
        out_specs=pl.BlockSpec((tm, d), lambda i: (i, 0)),
        out_shape=jax.ShapeDtypeStruct((n, d), F32),
        compiler_params=_cparams(("parallel",)),
        name="ffn_ln",
    )(y, wg3, wu3, wd3, g.reshape(1, d), b.reshape(1, d))


def _sort_key(x):
    bits = lax.bitcast_convert_type(x, I32)
    return bits ^ ((bits >> 31) & 0x7FFFFFFF)


def _dsa_kernel(rb_ref, qi_ref, wi_ref, qa_ref, ki_ref, ka_ref, va_ref, bn_ref, o_ref,
                keys_ref, cut_ref, s_ref, p_ref, madd_ref, wbc_ref, cmax_ref, mpart_ref, alpha_ref,
                l_ref, acc_ref, seen_ref, *, tq, n_sel, e_base, e_step, q_base, q_step, coff, n_valid, padf):
    i = pl.program_id(1)
    ck = DSA_CK
    rbk = DSA_RB
    nt = ck // LANE
    e_end = e_base + e_step * i
    n_chunks = (e_end + ck - 1) // ck
    qpos = q_base + q_step * i + lax.broadcasted_iota(I32, (tq, 1), 0)
    bound = jnp.minimum(((((qpos + coff) >> CHUNK_SHIFT) + 1) << CHUNK_SHIFT) - coff, n_valid)
    lane_rb = lax.broadcasted_iota(I32, (rbk, ck), 1)
    lane128 = lax.broadcasted_iota(I32, (tq, LANE), 1)

    def key_rows(c):
        kpos0 = e_end - ck * (c + 1)
        return kpos0, pl.multiple_of(jnp.maximum(kpos0 + padf, 0), LANE)

    def tiles(x):
        return [x[:, t * LANE:(t + 1) * LANE] for t in range(nt)]

    qstack = jnp.concatenate([qi_ref[:, h * LANE:(h + 1) * LANE] for h in range(IDX_HEADS)], axis=0)
    for h in range(IDX_HEADS):
        wbc_ref[h] = jnp.broadcast_to(wi_ref[:, h:h + 1], (tq, LANE))
    cmax_ref[...] = jnp.full((tq, ck), -jnp.inf, F32)

    def idx_dot(c, dst):
        _, r = key_rows(c)
        s_ref[dst] = _nt(qstack, ki_ref[pl.ds(r, ck), :])

    def idx_keys(c, src):
        kpos0, _ = key_rows(c)
        for b in range(tq // rbk):
            r0 = b * rbk
            sc = None
            for h in range(IDX_HEADS):
                sh = jnp.maximum(s_ref[src, h * tq + r0:h * tq + r0 + rbk, :], 0.0)
                term = jnp.concatenate([wbc_ref[h, r0:r0 + rbk, :]] * nt, axis=1) * sh
                sc = term if sc is None else sc + term
            kpos = kpos0 + lane_rb
            allowed = (kpos >= 0) & (kpos < bound[r0:r0 + rbk])
            keys_ref[c, r0:r0 + rbk, :] = jnp.where(allowed, _sort_key(sc), INT_MIN)
            cmax_ref[r0:r0 + rbk, :] = jnp.maximum(cmax_ref[r0:r0 + rbk, :], jnp.where(allowed, sc, -jnp.inf))

    idx_dot(0, 0)

    def p1(j, carry):
        c = 2 * j
        idx_dot(c + 1, 1)
        idx_keys(c, 0)
        idx_dot(c + 2, 0)
        idx_keys(c + 1, 1)
        return carry

    lax.fori_loop(0, (n_chunks + 1) // 2, p1, 0)

    def count_ge(cand):
        def body(c, acc):
            u = keys_ref[c]
            for t in range(nt):
                acc = acc + jnp.where(u[:, t * LANE:(t + 1) * LANE] >= cand, 1.0, 0.0)
            return acc
        acc = lax.fori_loop(0, n_chunks, body, jnp.zeros((tq, LANE), F32))
        return jnp.sum(acc, axis=1, keepdims=True)

    cm = tiles(cmax_ref[...])
    if n_sel <= LANE:
        fold = [functools.reduce(jnp.maximum, cm)]
    elif n_sel <= 2 * LANE:
        fold = [jnp.maximum(cm[0], cm[2]), jnp.maximum(cm[1], cm[3])]
    else:
        fold = cm
    fmin = jnp.min(functools.reduce(jnp.minimum, fold), axis=1, keepdims=True)
    fmax = jnp.max(functools.reduce(jnp.maximum, cm), axis=1, keepdims=True)
    lo0 = _sort_key(fmin) - 1
    hi0 = _sort_key(fmax) + 2
    small = bound <= n_sel
    pos_side = count_ge(jnp.full((tq, 1), 1, I32)) >= n_sel
    at_zero = jnp.logical_not(pos_side) & (count_ge(jnp.zeros((tq, 1), I32)) >= n_sel)
    lo1 = jnp.where(pos_side, jnp.maximum(lo0, 1), lo0)
    top1 = jnp.where(pos_side, hi0, jnp.minimum(hi0, 0)) - 1
    fixed = small | at_zero
    nbits = jnp.where(fixed, 0, 32 - lax.clz(lo1 ^ top1))
    low_mask = (jnp.int32(1) << jnp.minimum(nbits, 31)) - 1
    t0 = jnp.where(small, INT_MIN + 1,
                   jnp.where(at_zero, 0, jnp.where(nbits >= 32, INT_MIN, lo1 & ~low_mask)))
    max_bits = jnp.max(nbits.astype(F32)).astype(I32)

    def bis(it, t):
        b = nbits - 1 - it
        cand = t + (jnp.int32(1) << jnp.maximum(b, 0))
        return jnp.where((b >= 0) & (count_ge(cand) >= n_sel), cand, t)

    thr = lax.fori_loop(0, max_bits, bis, t0)
    thr = jnp.maximum(thr, INT_MIN + 1)
    n_gt = count_ge(thr + 1)
    need = n_sel - n_gt
    excess = (count_ge(thr) - n_gt) > need

    cut_ref[...] = jnp.full((tq, LANE), 2 ** 30, I32)

    @pl.when(jnp.max(jnp.where(excess, 1.0, 0.0)) > 0.0)
    def _():
        rr = lax.broadcasted_iota(I32, (ck, ck), 0)
        cc = lax.broadcasted_iota(I32, (ck, ck), 1)
        upto = jnp.where(rr <= cc, 1.0, 0.0).astype(BF16)

        def tally(j, run):
            c = n_chunks - 1 - j
            seen_ref[c] = run
            u = keys_ref[c]
            for t in range(nt):
                run = run + jnp.where(u[:, t * LANE:(t + 1) * LANE] == thr, 1.0, 0.0)
            return run

        lax.fori_loop(0, n_chunks, tally, jnp.zeros((tq, LANE), F32))

        def locate(c, cut):
            kpos0, _ = key_rows(c)
            seen = jnp.sum(seen_ref[c], axis=1, keepdims=True)
            eq = jnp.where(keys_ref[c] == thr, 1.0, 0.0).astype(BF16)
            incl = jnp.dot(eq, upto, preferred_element_type=F32)
            below = jnp.sum(jnp.where(seen + incl < need, 1.0, 0.0), axis=1, keepdims=True)
            here = excess & (seen < need) & (seen + incl[:, ck - 1:ck] >= need)
            return jnp.where(here, kpos0 + below.astype(I32) + 1, cut)

        def locate4(j, cut):
            for k in range(4):
                cut = locate(jnp.minimum(4 * j + k, n_chunks - 1), cut)
            return cut

        cut = lax.fori_loop(0, (n_chunks + 3) // 4, locate4, jnp.full((tq, 1), 2 ** 30, I32))
        cut_ref[...] = jnp.broadcast_to(cut, (tq, LANE))

    cut = cut_ref[:, 0:1]

    r4 = A_HEADS // A_KV_HEADS
    qg = []
    for g in range(A_KV_HEADS):
        rows = []
        for j in range(r4):
            slot = qa_ref[:, j * LANE:(j + 1) * LANE]
            half = (lane128 < HEAD_DIM) if g == 0 else (lane128 >= HEAD_DIM)
            rows.append(jnp.where(half, slot, jnp.zeros_like(slot)))
        qg.append(jnp.concatenate(rows, axis=0))
    farb = [jnp.concatenate([jnp.full((tq, 1), rb_ref[g * r4 + j], F32) for j in range(r4)], axis=0)
            for g in range(A_KV_HEADS)]
    l_ref[...] = jnp.zeros(l_ref.shape, F32)
    acc_ref[...] = jnp.zeros(acc_ref.shape, F32)

    def logits_dot(c, g):
        _, r = key_rows(c)
        s_ref[g] = _nt(qg[g], ka_ref[pl.ds(r, ck), :])

    def select_mask(c):
        kpos0, _ = key_rows(c)
        for b in range(tq // rbk):
            r0 = b * rbk
            u = keys_ref[c, r0:r0 + rbk, :]
            t_b = thr[r0:r0 + rbk]
            sel = (u > t_b) | ((u == t_b) & (kpos0 + lane_rb < cut[r0:r0 + rbk]))
            madd_ref[r0:r0 + rbk, :] = jnp.where(sel, 0.0, NEG)

    def softmax_passes(g, m_old, near):
        for b in range(r4 * tq // rbk):
            r0 = b * rbk
            q0 = r0 % tq
            sm = s_ref[g, r0:r0 + rbk, :] + madd_ref[q0:q0 + rbk, :]
            if near:
                sm = sm + bn_ref[g, r0:r0 + rbk, :]
            s_ref[g, r0:r0 + rbk, :] = sm
            mpart_ref[g, r0:r0 + rbk, :] = functools.reduce(jnp.maximum, tiles(sm))
        m_blk = jnp.max(mpart_ref[g], axis=1, keepdims=True)
        if not near:
            m_blk = m_blk + farb[g]
        m_new = jnp.maximum(m_old, m_blk)
        alpha = jnp.exp2(m_old - m_new)
        alpha_ref[g] = jnp.broadcast_to(alpha, (r4 * tq, LANE))
        shift = m_new if near else m_new - farb[g]
        for b in range(r4 * tq // rbk):
            r0 = b * rbk
            p = jnp.exp2(s_ref[g, r0:r0 + rbk, :] - shift[r0:r0 + rbk])
            l_ref[g, r0:r0 + rbk, :] = (alpha_ref[g, r0:r0 + rbk, :] * l_ref[g, r0:r0 + rbk, :]
                                        + functools.reduce(jnp.add, tiles(p)))
            p_ref[g, r0:r0 + rbk, :] = p.astype(BF16)
        return m_new

    def value_dot(c, g):
        _, r = key_rows(c)
        acc_ref[g] = alpha_ref[g] * acc_ref[g] + jnp.dot(p_ref[g], va_ref[pl.ds(r, ck), :],
                                                        preferred_element_type=F32)

    m0 = jnp.full((r4 * tq, 1), NEG, F32)
    logits_dot(0, 0)
    logits_dot(0, 1)
    select_mask(0)
    m0n = softmax_passes(0, m0, True)
    value_dot(0, 0)
    logits_dot(1, 0)
    m1n = softmax_passes(1, m0, True)

    def p3(c, ms):
        value_dot(c - 1, 1)
        logits_dot(c, 1)
        select_mask(c)
        m_a = softmax_passes(0, ms[0], False)
        value_dot(c, 0)
        logits_dot(c + 1, 0)
        m_b = softmax_passes(1, ms[1], False)
        return (m_a, m_b)

    lax.fori_loop(1, n_chunks, p3, (m0n, m1n))
    value_dot(n_chunks - 1, 1)
    outs = [acc_ref[g] / jnp.sum(l_ref[g], axis=1, keepdims=True) for g in range(A_KV_HEADS)]
    for j in range(r4):
        lo = outs[0][j * tq:(j + 1) * tq]
        hi = outs[1][j * tq:(j + 1) * tq]
        o_ref[:, j * LANE:(j + 1) * LANE] = jnp.where(lane128 < HEAD_DIM, lo, hi).astype(BF16)


def _dsa(rb_far, qi, wi, qa, ki, ka, va, bn, *, nb, nq, tq, row0, n_sel, e_base, e_step,
         q_base, q_step, coff, n_valid, padf, n_chunks_max):
    rb0 = row0 // tq
    r4 = A_HEADS // A_KV_HEADS
    assert IDX_HEADS == r4 and tq % DSA_RB == 0
    qmap = lambda b, i: (rb0 + b * nq + i, 0)
    kmap = lambda b, i: (b, 0, 0)
    kern = functools.partial(_dsa_kernel, tq=tq, n_sel=n_sel, e_base=e_base, e_step=e_step,
                             q_base=q_base, q_step=q_step, coff=coff, n_valid=n_valid, padf=padf)
    return pl.pallas_call(
        kern,
        grid=(nb, nq),
        in_specs=[pl.BlockSpec(memory_space=pltpu.SMEM),
                  pl.BlockSpec((tq, qi.shape[1]), qmap),
                  pl.BlockSpec((tq, LANE), qmap),
                  pl.BlockSpec((tq, qa.shape[1]), qmap),
                  pl.BlockSpec((None,) + ki.shape[1:], kmap),
                  pl.BlockSpec((None,) + ka.shape[1:], kmap),
                  pl.BlockSpec((None,) + va.shape[1:], kmap),
                  pl.BlockSpec(bn.shape, lambda b, i: (0, 0, 0))],
        out_specs=pl.BlockSpec((tq, qa.shape[1]), lambda b, i: (b * nq + i, 0)),
        out_shape=jax.ShapeDtypeStruct((nb * nq * tq, qa.shape[1]), BF16),
        scratch_shapes=[pltpu.VMEM((n_chunks_max + 1, tq, DSA_CK), I32),
                        pltpu.VMEM((tq, LANE), I32),
                        pltpu.VMEM((2, r4 * tq, DSA_CK), F32),
                        pltpu.VMEM((A_KV_HEADS, r4 * tq, DSA_CK), BF16),
                        pltpu.VMEM((tq, DSA_CK), F32),
                        pltpu.VMEM((IDX_HEADS, tq, LANE), F32),
                        pltpu.VMEM((tq, DSA_CK), F32),
                        pltpu.VMEM((A_KV_HEADS, r4 * tq, LANE), F32),
                        pltpu.VMEM((A_KV_HEADS, r4 * tq, LANE), F32),
                        pltpu.VMEM((A_KV_HEADS, r4 * tq, LANE), F32),
                        pltpu.VMEM((A_KV_HEADS, r4 * tq, LANE), F32),
                        pltpu.VMEM((n_chunks_max, tq, LANE), F32)],
        compiler_params=_cparams(("parallel", "arbitrary")),
        name="dsa",
    )(rb_far, qi, wi, qa, ki, ka, va, bn)


def _sb_kernel(tri_ref, q_ref, k_ref, v_ref, o_ref, *, tq, e_base, e_step, q_base, q_step, padf, n_valid):
    i = pl.program_id(2)
    bw = SB_BAND
    e_end = e_base + e_step * i
    n_bands = (e_end + bw - 1) // bw
    qpos = q_base + q_step * i + lax.broadcasted_iota(I32, (tq, 1), 0)
    lane = lax.broadcasted_iota(I32, (tq, LANE), 1)
    q = q_ref[...]
    zero = jnp.zeros_like(q)
    qh = [jnp.where(lane < HEAD_DIM, q, zero), jnp.where(lane >= HEAD_DIM, q, zero)]
    lane_bw = lax.broadcasted_iota(I32, (tq, bw), 1)

    def cond(st):
        return (st[0] < n_bands) & (st[1] > SB_STOP)

    def body(st):
        m, _, carry, acc = st
        kpos0 = e_end - bw * (m + 1)
        r = pl.multiple_of(kpos0 + padf, LANE)
        kt = k_ref[pl.ds(r, bw), :]
        vt = v_ref[pl.ds(r, bw), :]
        kpos = kpos0 + lane_bw
        before = (kpos < qpos) & (kpos >= 0) & (kpos < n_valid)
        zs, lss, lks, parts = [], [], [], []
        for h in range(2):
            z = _nt(qh[h], kt)
            ls = -(jnp.maximum(z, 0.0) + jnp.log1p(jnp.exp(-jnp.abs(z))))
            lk = jnp.where(before, ls, 0.0)
            hi = lk.astype(BF16)
            parts += [hi, (lk - hi.astype(F32)).astype(BF16)]
            zs.append(z)
            lss.append(ls)
            lks.append(lk)
        sums = jnp.dot(jnp.concatenate(parts, axis=0), tri_ref[...], preferred_element_type=F32)
        new_c, new_a = [], []
        worst = jnp.float32(-jnp.inf)
        for h in range(2):
            bl = sums[2 * h * tq:(2 * h + 1) * tq] + sums[(2 * h + 1) * tq:(2 * h + 2) * tq]
            w = jnp.where(before, jnp.exp(lss[h] + zs[h] + bl + carry[h]), 0.0)
            new_a.append(acc[h] + jnp.dot(w.astype(BF16), vt, preferred_element_type=F32))
            c_n = carry[h] + bl[:, 0:1] + lks[h][:, 0:1]
            new_c.append(c_n)
            worst = jnp.maximum(worst, jnp.max(c_n))
        return (m + 1, worst, tuple(new_c), tuple(new_a))

    init = (jnp.int32(0), jnp.float32(0.0),
            (jnp.zeros((tq, 1), F32), jnp.zeros((tq, 1), F32)),
            (jnp.zeros((tq, LANE), F32), jnp.zeros((tq, LANE), F32)))
    _, _, _, acc = lax.while_loop(cond, body, init)
    o_ref[...] = jnp.where(lane < HEAD_DIM, acc[0], acc[1]).astype(BF16)


def _stick_break(q, k, v, *, nb, nq, tq, row0, e_base, e_step, q_base, q_step, padf, n_valid):
    rb0 = row0 // tq
    npair = q.shape[1] // LANE
    kern = functools.partial(_sb_kernel, tq=tq, e_base=e_base, e_step=e_step, q_base=q_base,
                             q_step=q_step, padf=padf, n_valid=n_valid)
    rr = lax.broadcasted_iota(I32, (SB_BAND, SB_BAND), 0)
    cc = lax.broadcasted_iota(I32, (SB_BAND, SB_BAND), 1)
    tri = jnp.where(rr > cc, 1.0, 0.0).astype(BF16)
    kspec = pl.BlockSpec((None, k.shape[1], LANE), lambda b, p, i: (b, 0, p))
    return pl.pallas_call(
        kern,
        grid=(nb, npair, nq),
        in_specs=[pl.BlockSpec(tri.shape, lambda b, p, i: (0, 0)),
                  pl.BlockSpec((tq, LANE), lambda b, p, i: (rb0 + b * nq + i, p)), kspec, kspec],
        out_specs=pl.BlockSpec((tq, LANE), lambda b, p, i: (b * nq + i, p)),
        out_shape=jax.ShapeDtypeStruct((nb * nq * tq, q.shape[1]), BF16),
        compiler_params=_cparams(("parallel", "parallel", "arbitrary")),
        name="stick_break",
    )(tri, q, k, v)


def _swa_kernel(sink_ref, q_ref, k_ref, v_ref, bias_ref, o_ref, s_ref, p_ref, madd_ref, part_ref,
                *, tq, ck, r_base, r_step, p_base, p_step, q_base, q_step, coff, n_valid):
    i = pl.program_id(1)
    rbk = DSA_RB
    nt = ck // LANE
    r0 = pl.multiple_of(r_base + r_step * i, 16)
    kpos0 = p_base + p_step * i
    kt = k_ref[pl.ds(r0, ck), :]
    vt = v_ref[pl.ds(r0, ck), :]
    qpos = q_base + q_step * i + lax.broadcasted_iota(I32, (tq, 1), 0)
    qc = (qpos + coff) >> CHUNK_SHIFT
    hi_b = jnp.minimum(((qc + 1) << CHUNK_SHIFT) - coff, n_valid)
    lo_b = jnp.maximum(((qc - WIN_CHUNKS) << CHUNK_SHIFT) - coff, 0)
    kpos = kpos0 + lax.broadcasted_iota(I32, (rbk, ck), 1)
    for b in range(tq // rbk):
        rs = slice(b * rbk, (b + 1) * rbk)
        madd_ref[rs, :] = jnp.where((kpos >= lo_b[rs]) & (kpos < hi_b[rs]), 0.0, NEG)
    r8 = C_HEADS // C_KV_HEADS
    lane = lax.broadcasted_iota(I32, (tq, LANE), 1)

    def tiles(x):
        return [x[:, t * LANE:(t + 1) * LANE] for t in range(nt)]

    outs = []
    for g in range(C_KV_HEADS):
        rows = []
        for j in range(r8):
            slot = q_ref[:, j * LANE:(j + 1) * LANE]
            half = (lane < HEAD_DIM) if g == 0 else (lane >= HEAD_DIM)
            rows.append(jnp.where(half, slot, jnp.zeros_like(slot)))
        s_ref[g] = _nt(jnp.concatenate(rows, axis=0), kt)
        sink = jnp.concatenate([jnp.full((tq, 1), sink_ref[g * r8 + j], F32) for j in range(r8)], axis=0)
        for b in range(r8 * tq // rbk):
            rs = slice(b * rbk, (b + 1) * rbk)
            q0 = (b * rbk) % tq
            sm = s_ref[g, rs, :] + bias_ref[g, rs, :] + madd_ref[q0:q0 + rbk, :]
            s_ref[g, rs, :] = sm
            part_ref[g, rs, :] = functools.reduce(jnp.maximum, tiles(sm))
        m = jnp.maximum(jnp.max(part_ref[g], axis=1, keepdims=True), sink)
        for b in range(r8 * tq // rbk):
            rs = slice(b * rbk, (b + 1) * rbk)
            p = jnp.exp2(s_ref[g, rs, :] - m[rs])
            part_ref[g, rs, :] = functools.reduce(jnp.add, tiles(p))
            p_ref[g, rs, :] = p.astype(BF16)
        den = jnp.sum(part_ref[g], axis=1, keepdims=True) + jnp.exp2(sink - m)
        outs.append(jnp.dot(p_ref[g], vt, preferred_element_type=F32) / den)
    for j in range(r8):
        lo = outs[0][j * tq:(j + 1) * tq]
        hi = outs[1][j * tq:(j + 1) * tq]
        o_ref[:, j * LANE:(j + 1) * LANE] = jnp.where(lane < HEAD_DIM, lo, hi).astype(BF16)


def _swa(sinks, q, k, v, bias, *, nb, nq, tq, row0, ck, r_base, r_step, p_base, p_step,
         q_base, q_step, coff, n_valid):
    rb0 = row0 // tq
    r8 = C_HEADS // C_KV_HEADS
    assert tq % DSA_RB == 0 and ck % LANE == 0
    kern = functools.partial(_swa_kernel, tq=tq, ck=ck, r_base=r_base, r_step=r_step, p_base=p_base,
                             p_step=p_step, q_base=q_base, q_step=q_step, coff=coff, n_valid=n_valid)
    kmap = lambda b, i: (b, 0, 0)
    return pl.pallas_call(
        kern,
        grid=(nb, nq),
        in_specs=[pl.BlockSpec(memory_space=pltpu.SMEM),
                  pl.BlockSpec((tq, q.shape[1]), lambda b, i: (rb0 + b * nq + i, 0)),
                  pl.BlockSpec((None,) + k.shape[1:], kmap),
                  pl.BlockSpec((None,) + v.shape[1:], kmap),
                  pl.BlockSpec(bias.shape, lambda b, i: (0, 0, 0))],
        out_specs=pl.BlockSpec((tq, q.shape[1]), lambda b, i: (b * nq + i, 0)),
        out_shape=jax.ShapeDtypeStruct((nb * nq * tq, q.shape[1]), BF16),
        scratch_shapes=[pltpu.VMEM((C_KV_HEADS, r8 * tq, ck), F32),
                        pltpu.VMEM((C_KV_HEADS, r8 * tq, ck), BF16),
                        pltpu.VMEM((tq, ck), F32),
                        pltpu.VMEM((C_KV_HEADS, r8 * tq, LANE), F32)],
        compiler_params=_cparams(("parallel", "arbitrary")),
        name="swa",
    )(sinks, q, k, v, bias)


def _router_kernel(y_ref, rh_ref, rl_ref, idx_ref, gate_ref):
    y = y_ref[...]
    yh = y.astype(BF16)
    yl = (y - yh.astype(F32)).astype(BF16)
    rh = rh_ref[...]
    logits = (jnp.dot(yh, rh, preferred_element_type=F32) + jnp.dot(yl, rh, preferred_element_type=F32)
              + jnp.dot(yh, rl_ref[...], preferred_element_type=F32))
    lane = lax.broadcasted_iota(I32, logits.shape, 1)
    logits = jnp.where(lane < N_EXPERTS, logits, -jnp.inf)
    m1 = jnp.max(logits, axis=1, keepdims=True)
    i1 = jnp.min(jnp.where(logits == m1, lane, LANE), axis=1, keepdims=True)
    rest = jnp.where(lane == i1, -jnp.inf, logits)
    m2 = jnp.max(rest, axis=1, keepdims=True)
    i2 = jnp.min(jnp.where(rest == m2, lane, LANE), axis=1, keepdims=True)
    e2 = jnp.exp(m2 - m1)
    den = 1.0 + e2
    idx_ref[...] = jnp.where(lane == 0, i1, jnp.where(lane == 1, i2, 0))
    gate_ref[...] = jnp.where(lane == 0, 1.0 / den, jnp.where(lane == 1, e2 / den, 0.0))


def _router(y, router, tm):
    n, d = y.shape
    rpad = jnp.pad(router.astype(F32), ((0, 0), (0, LANE - router.shape[1])))
    rh = rpad.astype(BF16)
    rl = (rpad - rh.astype(F32)).astype(BF16)
    return pl.pallas_call(
        _router_kernel,
        grid=(pl.cdiv(n, tm),),
        in_specs=[pl.BlockSpec((tm, d), lambda i: (i, 0)),
                  pl.BlockSpec((d, LANE), lambda i: (0, 0)), pl.BlockSpec((d, LANE), lambda i: (0, 0))],
        out_specs=[pl.BlockSpec((tm, LANE), lambda i: (i, 0)), pl.BlockSpec((tm, LANE), lambda i: (i, 0))],
        out_shape=[jax.ShapeDtypeStruct((n, LANE), I32), jax.ShapeDtypeStruct((n, LANE), F32)],
        compiler_params=_cparams(("parallel",)),
        name="router",
    )(y, rh, rl)


def _moe_kernel(be_ref, nu_ref, x_ref, wg_ref, wu_ref, wd_ref, o_ref, acc_ref, xb_ref):
    t = pl.program_id(0)
    f = pl.program_id(1)

    @pl.when(f == 0)
    def _():
        acc_ref[...] = jnp.zeros_like(acc_ref)
        xb_ref[...] = x_ref[...].astype(BF16)

    @pl.when(t < nu_ref[0])
    def _():
        xb = xb_ref[...]
        hg = jnp.dot(xb, wg_ref[...], preferred_element_type=F32)
        hu = jnp.dot(xb, wu_ref[...], preferred_element_type=F32)
        h = (hg * jax.nn.sigmoid(hg) * hu).astype(BF16)
        acc_ref[...] += jnp.dot(h, wd_ref[...], preferred_element_type=F32)

    @pl.when(f == pl.num_programs(1) - 1)
    def _():
        o_ref[...] = acc_ref[...]


def _moe_experts(block_expert, n_used, xs, wg, wu, wd, tmb, tf):
    n_slots, d = xs.shape
    de = wg.shape[2]
    grid_spec = pltpu.PrefetchScalarGridSpec(
        num_scalar_prefetch=2,
        grid=(n_slots // tmb, de // tf),
        in_specs=[pl.BlockSpec((tmb, d), lambda t, f, be, nu: (t, 0)),
                  pl.BlockSpec((None, d, tf), lambda t, f, be, nu: (be[t], 0, f)),
                  pl.BlockSpec((None, d, tf), lambda t, f, be, nu: (be[t], 0, f)),
                  pl.BlockSpec((None, tf, d), lambda t, f, be, nu: (be[t], f, 0))],
        out_specs=pl.BlockSpec((tmb, d), lambda t, f, be, nu: (t, 0)),
        scratch_shapes=[pltpu.VMEM((tmb, d), F32), pltpu.VMEM((tmb, d), BF16)],
    )
    return pl.pallas_call(
        _moe_kernel,
        grid_spec=grid_spec,
        out_shape=jax.ShapeDtypeStruct((n_slots, d), F32),
        compiler_params=_cparams(("parallel", "arbitrary")),
        name="moe_experts",
    )(block_expert, n_used, xs, wg, wu, wd)


def _row_copy(src_hbm, src_row, dst, dst_row, sem):
    return pltpu.make_async_copy(src_hbm.at[pl.ds(src_row, 1)], dst.at[pl.ds(dst_row, 1)], sem)


def _dispatch_kernel(slot_ref, x_ref, init_hbm, xs_hbm, sem, *, tt):
    del init_hbm

    def start(j, carry):
        for k in range(TOP_K):
            _row_copy(x_ref, j, xs_hbm, slot_ref[0, 0, TOP_K * j + k], sem).start()
        return carry

    def wait(j, carry):
        _row_copy(x_ref, 0, xs_hbm, 0, sem).wait()
        return carry

    lax.fori_loop(0, tt, start, 0, unroll=8)
    lax.fori_loop(0, TOP_K * tt, wait, 0, unroll=8)


def _dispatch(x, slot3, n_slots, tt):
    n, d = x.shape
    init = jnp.zeros((n_slots, d), x.dtype)
    return pl.pallas_call(
        functools.partial(_dispatch_kernel, tt=tt),
        grid=(n // tt,),
        in_specs=[pl.BlockSpec((1, 1, TOP_K * tt), lambda i: (i, 0, 0), memory_space=pltpu.SMEM),
                  pl.BlockSpec((tt, d), lambda i: (i, 0)), pl.BlockSpec(memory_space=pl.ANY)],
        out_specs=pl.BlockSpec(memory_space=pl.ANY),
        out_shape=jax.ShapeDtypeStruct((n_slots, d), x.dtype),
        scratch_shapes=[pltpu.SemaphoreType.DMA],
        input_output_aliases={2: 0},
        compiler_params=_cparams(("arbitrary",)),
        name="moe_dispatch",
    )(slot3, x, init)


def _combine_ln_kernel(slot_ref, y_ref, gate_ref, g_ref, b_ref, outs_hbm, o_ref, buf_ref, sem, *, tt):
    def start(j, carry):
        for k in range(TOP_K):
            _row_copy(outs_hbm, slot_ref[0, 0, TOP_K * j + k], buf_ref.at[k], j, sem).start()
        return carry

    def wait(j, carry):
        _row_copy(outs_hbm, 0, buf_ref.at[0], 0, sem).wait()
        return carry

    lax.fori_loop(0, tt, start, 0, unroll=8)
    lax.fori_loop(0, TOP_K * tt, wait, 0, unroll=8)
    mo = gate_ref[:, 0:1] * buf_ref[0]
    for k in range(1, TOP_K):
        mo = mo + gate_ref[:, k:k + 1] * buf_ref[k]
    o_ref[...] = _layer_norm(DN_ALPHA * y_ref[...] + mo, g_ref[...], b_ref[...])


def _combine_ln(y, outs, slot3, gate_p, g, b, tt):
    n, d = y.shape
    row = pl.BlockSpec((tt, d), lambda i: (i, 0))
    vec = pl.BlockSpec((1, d), lambda i: (0, 0))
    return pl.pallas_call(
        functools.partial(_combine_ln_kernel, tt=tt),
        grid=(n // tt,),
        in_specs=[pl.BlockSpec((1, 1, TOP_K * tt), lambda i: (i, 0, 0), memory_space=pltpu.SMEM),
                  row, pl.BlockSpec((tt, LANE), lambda i: (i, 0)), vec, vec,
                  pl.BlockSpec(memory_space=pl.ANY)],
        out_specs=row,
        out_shape=jax.ShapeDtypeStruct((n, d), F32),
        scratch_shapes=[pltpu.VMEM((TOP_K, tt, d), F32), pltpu.SemaphoreType.DMA],
        compiler_params=_cparams(("arbitrary",)),
        name="moe_combine_ln",
    )(slot3, y, gate_p, g.reshape(1, d), b.reshape(1, d), outs)


def _moe(y, router, w_gate, w_up, w_down, g, b, tm, tmb, tf, tt):
    n, d = y.shape
    assert n % tt == 0
    idx_p, gate_p = _router(y, router, tm)
    e_flat = idx_p[:, :TOP_K].reshape(-1)
    onehot = (e_flat[:, None] == jnp.arange(N_EXPERTS, dtype=I32)[None, :]).astype(I32)
    rank = jnp.sum((jnp.cumsum(onehot, axis=0) - onehot) * onehot, axis=1)
    counts = jnp.sum(onehot, axis=0)
    padded = (counts + tmb - 1) // tmb * tmb
    pad_end = jnp.cumsum(padded)
    slot = ((pad_end - padded)[e_flat] + rank).astype(I32)
    n_blocks = -(-(TOP_K * n) // tmb) + N_EXPERTS
    n_slots = n_blocks * tmb
    block_expert = jnp.minimum(
        jnp.searchsorted(pad_end, jnp.arange(n_blocks, dtype=I32) * tmb, side="right"),
        N_EXPERTS - 1).astype(I32)
    n_used = (pad_end[-1] // tmb).astype(I32).reshape(1)
    slot3 = slot.reshape(n // tt, 1, TOP_K * tt)
    xs = _dispatch(y, slot3, n_slots, tt)
    outs = _moe_experts(block_expert, n_used, xs, w_gate.astype(BF16), w_up.astype(BF16),
                        w_down.astype(BF16), tmb, tf)
    return _combine_ln(y, outs, slot3, gate_p, g, b, tt)


def _t5_bucket(rel):
    half = NUM_BUCKETS // 2
    exact = half // 2
    n = jnp.abs(rel)
    far = exact + sum((n >= t).astype(I32) for t in (12, 16, 23, 32, 46, 64, 91))
    return jnp.where(rel > 0, half, 0) + jnp.where(n < exact, n, far)


def _bias_tile(rel_bias, n_groups, per_group, tq, ck, d0):
    d = d0 + jnp.arange(ck, dtype=I32)[None, :] - jnp.arange(tq, dtype=I32)[:, None]
    tile = rel_bias.astype(F32)[_t5_bucket(d)]
    tile = jnp.transpose(tile[:, :, :n_groups * per_group], (2, 0, 1))
    return tile.reshape(n_groups, per_group * tq, ck)


def _cols(w, ranges):
    parts = []
    for r in ranges:
        if isinstance(r, int):
            parts.append(jnp.zeros((w.shape[0], r), w.dtype))
        else:
            parts.append(w[:, r[0]:r[1]])
    return jnp.concatenate(parts, axis=1)


def _head_pair_order(n_heads):
    half = n_heads // 2
    order = []
    for j in range(half):
        order += [j, half + j]
    return order


L0_SPECS = ((0, 512, ("bf16e",)), (512, 128, ("f32", "bf16")), (640, 128, ("f32", "bf16")),
            (768, 512, ("hilo",)), (1280, 128, ("f32", "bf16")), (1408, 128, ("wi",)),
            (1536, 512, ("bf16s",)), (2048, 512, ("f32", "bf16")), (2560, 512, ("f32", "bf16")))
L1_SPECS = ((0, 1024, ("bf16e",)), (1024, 128, ("f32", "bf16")), (1152, 128, ("f32", "bf16")))


def _l0_weight(w_in):
    hd = HEAD_DIM
    rng = [(hd * h, hd * h + hd) for h in _head_pair_order(A_HEADS)]
    rng += [(512, 640), (640, 768)]
    for h in range(IDX_HEADS):
        rng += [(768 + hd * h, 768 + hd * h + hd)] * 2
    rng += [(1024, 1088)] * 2
    rng += [(1088, 1092), LANE - IDX_HEADS]
    rng += [(1092, 1604), (1604, 2116), (2116, 2628)]
    return _cols(w_in, rng).astype(BF16)


def _l1_weight(w_in):
    hd = HEAD_DIM
    rng = [(hd * h, hd * h + hd) for h in _head_pair_order(C_HEADS)]
    rng += [(1024, 1152), (1152, 1280)]
    return _cols(w_in, rng).astype(BF16)


def _perm_rows(w, n_heads):
    return jnp.concatenate([w[HEAD_DIM * h:HEAD_DIM * (h + 1)] for h in _head_pair_order(n_heads)], axis=0)


def _front_pad(a, nb, rows_in, front, rows_out):
    a = a.reshape(nb, rows_in, a.shape[-1])
    return jnp.pad(a, ((0, 0), (front, rows_out - front - rows_in), (0, 0)))


def _layer0_attention(x_prompt, x_sample, cache_a_k, cache_a_v, cache_a_idx_k, cache_b_k, cache_b_v,
                      meta_tokens, rel_bias, l0_w_in):
    nb, seq, d = x_prompt.shape
    t = N_META + seq
    nq = -(-t // QBLK)
    tp = nq * QBLK
    db, ds, _ = x_sample.shape
    past = cache_a_k.shape[1]
    assert d == D_MODEL and past % LANE == 0 and ds % DSA_RB == 0 and ds <= 64
    np_, ns = nb * tp, db * ds
    tm = 512

    meta = jnp.broadcast_to(meta_tokens[None].astype(x_prompt.dtype), (nb, N_META, d))
    hp = jnp.concatenate([meta, x_prompt, jnp.zeros((nb, tp - t, d), x_prompt.dtype)], axis=1)
    x_all = jnp.concatenate([hp.reshape(np_, d), x_sample.reshape(ns, d)], axis=0)
    rel_bias = rel_bias.astype(F32)
    rb_far = rel_bias[NUM_BUCKETS // 2 - 1] * LOG2E

    (qa16, ka32, ka16, va32, va16, qi16, ki32, ki16, wi32, qb16, kb32, kb16, vb32, vb16) = _project(
        x_all, _l0_weight(l0_w_in), L0_SPECS, tm)

    n_sel_p = min(TOPK_MAX, (t - N_META) // 4)
    kr = QBLK * (nq + 1) + DSA_PADF
    ncm = -(-(QBLK * (nq + 1)) // DSA_CK)
    pk = [_front_pad(a[:np_], nb, tp, DSA_PADF, kr) for a in (ki16, ka16, va16)]
    bn_p = _bias_tile(rel_bias, A_KV_HEADS, A_HEADS // A_KV_HEADS, QBLK, DSA_CK, -(DSA_CK // 2)) * LOG2E
    oa_p = _dsa(rb_far, qi16, wi32, qa16, *pk, bn_p, nb=nb, nq=nq, tq=QBLK, row0=0, n_sel=n_sel_p,
                e_base=2 * QBLK, e_step=QBLK, q_base=0, q_step=QBLK, coff=PROMPT_COFF,
                n_valid=t, padf=DSA_PADF, n_chunks_max=ncm)

    n_keys = past + ds
    n_sel_s = min(TOPK_MAX, n_keys // 4)
    e_s = -(-n_keys // LANE) * LANE
    ncs = -(-e_s // DSA_CK)
    krs = ncs * DSA_CK
    fs = krs - e_s

    def cat_keys(cache, new, width_dup):
        c = cache.reshape(db, past, -1).astype(BF16)
        if width_dup:
            c = jnp.concatenate([c, c], axis=-1)
        a = jnp.concatenate([c, new[np_:].reshape(db, ds, -1)], axis=1)
        return jnp.pad(a, ((0, 0), (fs, krs - fs - n_keys), (0, 0)))

    sk = [cat_keys(cache_a_idx_k, ki16, True), cat_keys(cache_a_k, ka16, False),
          cat_keys(cache_a_v, va16, False)]
    bn_s = _bias_tile(rel_bias, A_KV_HEADS, A_HEADS // A_KV_HEADS, ds, DSA_CK, e_s - DSA_CK - past) * LOG2E
    oa_s = _dsa(rb_far, qi16, wi32, qa16, *sk, bn_s, nb=db, nq=1, tq=ds, row0=np_, n_sel=n_sel_s,
                e_base=e_s, e_step=0, q_base=past, q_step=0, coff=0, n_valid=n_keys, padf=fs,
                n_chunks_max=ncs)

    pfb = SB_BAND - QBLK
    kb_p = _front_pad(kb16[:np_], nb, tp, pfb, tp + pfb)
    vb_p = _front_pad(vb16[:np_], nb, tp, pfb, tp + pfb)
    ob_p = _stick_break(qb16, kb_p, vb_p, nb=nb, nq=nq, tq=QBLK, row0=0, e_base=QBLK, e_step=QBLK,
                        q_base=0, q_step=QBLK, padf=pfb, n_valid=t)
    rows_s = -(-e_s // SB_BAND) * SB_BAND
    pfs = rows_s - e_s

    def cat_b(cache, new):
        a = jnp.concatenate([cache.reshape(db, past, -1).astype(BF16), new[np_:].reshape(db, ds, -1)], axis=1)
        return jnp.pad(a, ((0, 0), (pfs, rows_s - pfs - n_keys), (0, 0)))

    ob_s = _stick_break(qb16, cat_b(cache_b_k, kb16), cat_b(cache_b_v, vb16), nb=db, nq=1, tq=ds,
                        row0=np_, e_base=e_s, e_step=0, q_base=past, q_step=0, padf=pfs, n_valid=n_keys)

    oa = jnp.concatenate([oa_p, oa_s], axis=0)
    ob = jnp.concatenate([ob_p, ob_s], axis=0)
    return dict(x_all=x_all, oa=oa, ob=ob, np=np_, tp=tp, t=t, nq=nq, rel_bias=rel_bias,
                ka32=ka32, va32=va32, ki32=ki32, kb32=kb32, vb32=vb32)


def kernel(x_prompt, x_sample, cache_a_k, cache_a_v, cache_a_idx_k, cache_b_k, cache_b_v, cache_c_k, cache_c_v, meta_tokens, rel_bias, l0_w_in, l0_w_out, l0_ln1_g, l0_ln1_b, l0_w_gate, l0_w_up, l0_w_down, l0_ln2_g, l0_ln2_b, l1_w_in, l1_sinks, l1_w_out, l1_ln1_g, l1_ln1_b, l1_router, l1_w_gate, l1_w_up, l1_w_down, l1_ln2_g, l1_ln2_b):
    a0 = _layer0_attention(x_prompt, x_sample, cache_a_k, cache_a_v, cache_a_idx_k, cache_b_k, cache_b_v,
                           meta_tokens, rel_bias, l0_w_in)
    x_all, oa, ob, np_, tp, t, nq, rel_bias = (a0[k] for k in ("x_all", "oa", "ob", "np", "tp", "t", "nq", "rel_bias"))
    ka32, va32, ki32, kb32, vb32 = (a0[k] for k in ("ka32", "va32", "ki32", "kb32", "vb32"))
    nb, _, d = x_prompt.shape
    db, ds, _ = x_sample.shape
    past = cache_a_k.shape[1]
    tm = 512
    w_out0 = l0_w_out.astype(BF16)
    half0 = A_HEADS * HEAD_DIM
    y0 = _mix_ln(x_all, [oa, ob], [_perm_rows(w_out0[:half0], A_HEADS), w_out0[half0:]],
                 l0_ln1_g, l0_ln1_b, tm)
    h1 = _ffn_ln(y0, l0_w_gate, l0_w_up, l0_w_down, l0_ln2_g, l0_ln2_b, tm, 256)

    oc, kc32, vc32 = _layer1_attention(h1, cache_c_k, cache_c_v, rel_bias, l1_w_in, l1_sinks,
                                       nb=nb, nq=nq, t=t, db=db, ds=ds, past=past, tm=tm)
    y1 = _mix_ln(h1, [oc], [_perm_rows(l1_w_out.astype(BF16), C_HEADS)], l1_ln1_g, l1_ln1_b, tm)
    n_all = y1.shape[0]
    tt = next(c for c in (256, 128, 64, 32, 16, 8) if n_all % c == 0)
    h2 = _moe(y1, l1_router, l1_w_gate, l1_w_up, l1_w_down, l1_ln2_g, l1_ln2_b, tm, 1024, 512, tt)
    return _assemble(h2, ka32, va32, ki32, kb32, vb32, kc32, vc32, cache_c_k, cache_c_v,
                     nb=nb, tp=tp, t=t, db=db, ds=ds)


def _layer1_attention(h1, cache_c_k, cache_c_v, rel_bias, l1_w_in, l1_sinks, *, nb, nq, t, db, ds, past, tm):
    tp = nq * QBLK
    np_ = nb * tp
    qc16, kc32, kc16, vc32, vc16 = _project(h1, _l1_weight(l1_w_in), L1_SPECS, tm)
    ckp = 4 * QBLK
    krc = QBLK * (nq - 1) + ckp
    kc_p = _front_pad(kc16[:np_], nb, tp, 2 * QBLK, max(krc, tp + 2 * QBLK))
    vc_p = _front_pad(vc16[:np_], nb, tp, 2 * QBLK, max(krc, tp + 2 * QBLK))
    bc_p = _bias_tile(rel_bias, C_KV_HEADS, C_HEADS // C_KV_HEADS, QBLK, ckp, -2 * QBLK) * LOG2E
    sinks = l1_sinks.astype(F32) * LOG2E
    oc_p = _swa(sinks, qc16, kc_p, vc_p, bc_p, nb=nb, nq=nq, tq=QBLK, row0=0, ck=ckp, r_base=0,
                r_step=QBLK, p_base=-2 * QBLK, p_step=QBLK, q_base=0, q_step=QBLK, coff=PROMPT_COFF, n_valid=t)

    buf = cache_c_k.shape[1]
    cks = -(-(buf + ds) // LANE) * LANE

    def cat_c(cache, new):
        a = jnp.concatenate([cache.reshape(db, buf, -1).astype(BF16), new[np_:].reshape(db, ds, -1)], axis=1)
        return jnp.pad(a, ((0, 0), (0, cks - buf - ds), (0, 0)))

    bc_s = _bias_tile(rel_bias, C_KV_HEADS, C_HEADS // C_KV_HEADS, ds, cks, -buf) * LOG2E
    oc_s = _swa(sinks, qc16, cat_c(cache_c_k, kc16), cat_c(cache_c_v, vc16), bc_s, nb=db, nq=1, tq=ds,
                row0=np_, ck=cks, r_base=0, r_step=0, p_base=past - buf, p_step=0, q_base=past,
                q_step=0, coff=0, n_valid=past + ds)
    return jnp.concatenate([oc_p, oc_s], axis=0), kc32, vc32


def _assemble(h2, ka32, va32, ki32, kb32, vb32, kc32, vc32, cache_c_k, cache_c_v, *, nb, tp, t, db, ds):
    np_ = nb * tp
    d = h2.shape[1]

    def pr(a, heads):
        a = a[:np_].reshape(nb, tp, -1)[:, :t]
        return a.reshape(nb, t, heads, HEAD_DIM) if heads else a[..., :HEAD_DIM]

    def sm(a, heads):
        a = a[np_:].reshape(db, ds, -1)
        return a.reshape(db, ds, heads, HEAD_DIM) if heads else a[..., :HEAD_DIM]

    y_prompt = h2[:np_].reshape(nb, tp, d)[:, N_META:t]
    y_sample = h2[np_:].reshape(db, ds, d)
    bufp = min(WINDOW, t)
    p_ck = pr(kc32, C_KV_HEADS)[:, t - bufp:]
    p_cv = pr(vc32, C_KV_HEADS)[:, t - bufp:]
    s_ck = jnp.concatenate([cache_c_k, sm(kc32, C_KV_HEADS)], axis=1)[:, ds:]
    s_cv = jnp.concatenate([cache_c_v, sm(vc32, C_KV_HEADS)], axis=1)[:, ds:]
    return (y_prompt, y_sample,
            pr(ka32, A_KV_HEADS), pr(va32, A_KV_HEADS), pr(ki32, 0), pr(kb32, B_HEADS), pr(vb32, B_HEADS),
            p_ck, p_cv,
            sm(ka32, A_KV_HEADS), sm(va32, A_KV_HEADS), sm(ki32, 0), sm(kb32, B_HEADS), sm(vb32, B_HEADS),
            s_ck, s_cv)
```

```python
import functools

import jax
import jax.numpy as jnp
from jax import lax
from jax.experimental import pallas as pl
from jax.experimental.pallas import tpu as pltpu

F32 = jnp.float32
BF16 = jnp.bfloat16
I32 = jnp.int32

D_MODEL = 1024
CHUNK_SHIFT = 6
N_META = 16
HEAD_DIM = 64
A_HEADS = 8
A_KV_HEADS = 2
IDX_HEADS = 4
TOPK_MAX = 256
B_HEADS = 8
C_HEADS = 16
C_KV_HEADS = 2
WINDOW = 128
WIN_CHUNKS = 2
NUM_BUCKETS = 32
N_EXPERTS = 8
TOP_K = 2
LN_EPS = 1e-5
DEPTH = 2
DN_ALPHA = (2.0 * DEPTH) ** 0.25

LANE = 128
QBLK = 128
DSA_CK = 512
DSA_PADF = 384
DSA_RB = 32
LOG2E = 1.4426950408889634
SB_BAND = 512
SB_PAIRS = 2
SB_STOP = -120.0
NEG = -1e30
PROMPT_COFF = 64 - N_META
INT_MIN = -2 ** 31
VMEM_LIMIT = 56 * 1024 * 1024


def _cparams(sem):
    return pltpu.CompilerParams(dimension_semantics=sem, vmem_limit_bytes=VMEM_LIMIT)


def _nt(a, b):
    return lax.dot_general(a, b, (((1,), (1,)), ((), ())), preferred_element_type=F32)


def _layer_norm(v, g, b):
    mu = jnp.mean(v, axis=-1, keepdims=True)
    c = v - mu
    var = jnp.mean(c * c, axis=-1, keepdims=True)
    return c * lax.rsqrt(var + LN_EPS) * g + b


def _proj_kernel(x_ref, w_ref, *out_refs, specs):
    xb = x_ref[...].astype(BF16)
    k = 0
    for c0, width, kinds in specs:
        acc = jnp.dot(xb, w_ref[:, c0:c0 + width], preferred_element_type=F32)
        for kind in kinds:
            if kind == "f32":
                val = acc
            elif kind == "bf16":
                val = acc.astype(BF16)
            elif kind == "bf16s":
                val = (acc * 0.125).astype(BF16)
            elif kind == "bf16e":
                val = (acc * (0.125 * LOG2E)).astype(BF16)
            elif kind == "hilo":
                hi = acc.astype(BF16)
                lo = (acc - hi.astype(F32)).astype(BF16)
                lane = lax.broadcasted_iota(I32, acc.shape, 1) & (LANE - 1)
                val = jnp.where(lane < HEAD_DIM, hi, lo)
            elif kind == "wi":
                val = acc * 0.0625
            else:
                raise ValueError(kind)
            out_refs[k][...] = val
            k += 1


def _project(x, w16, specs, tm):
    n = x.shape[0]
    out_shape, out_specs = [], []
    for _, width, kinds in specs:
        for kind in kinds:
            dt = F32 if kind in ("f32", "wi") else BF16
            out_shape.append(jax.ShapeDtypeStruct((n, width), dt))
            out_specs.append(pl.BlockSpec((tm, width), lambda i: (i, 0)))
    return pl.pallas_call(
        functools.partial(_proj_kernel, specs=specs),
        grid=(pl.cdiv(n, tm),),
        in_specs=[pl.BlockSpec((tm, x.shape[1]), lambda i: (i, 0)),
                  pl.BlockSpec(w16.shape, lambda i: (0, 0))],
        out_specs=out_specs,
        out_shape=out_shape,
        compiler_params=_cparams(("parallel",)),
        name="proj",
    )(x, w16)


def _mix_ln_kernel(x_ref, *refs, n_pairs):
    o_refs = refs[:n_pairs]
    w_refs = refs[n_pairs:2 * n_pairs]
    g_ref, b_ref, y_ref = refs[2 * n_pairs:]
    acc = DN_ALPHA * x_ref[...]
    for o_ref, w_ref in zip(o_refs, w_refs):
        acc = acc + jnp.dot(o_ref[...], w_ref[...], preferred_element_type=F32)
    y_ref[...] = _layer_norm(acc, g_ref[...], b_ref[...])


def _mix_ln(x, os_, ws, g, b, tm):
    n, d = x.shape
    in_specs = [pl.BlockSpec((tm, d), lambda i: (i, 0))]
    in_specs += [pl.BlockSpec((tm, o.shape[1]), lambda i: (i, 0)) for o in os_]
    in_specs += [pl.BlockSpec(w.shape, lambda i: (0, 0)) for w in ws]
    in_specs += [pl.BlockSpec((1, d), lambda i: (0, 0))] * 2
    return pl.pallas_call(
        functools.partial(_mix_ln_kernel, n_pairs=len(os_)),
        grid=(pl.cdiv(n, tm),),
        in_specs=in_specs,
        out_specs=pl.BlockSpec((tm, d), lambda i: (i, 0)),
        out_shape=jax.ShapeDtypeStruct((n, d), F32),
        compiler_params=_cparams(("parallel",)),
        name="mix_ln",
    )(x, *os_, *ws, g.reshape(1, d), b.reshape(1, d))


def _ffn_ln_kernel(y_ref, wg_ref, wu_ref, wd_ref, g_ref, b_ref, o_ref, *, nf):
    y = y_ref[...]
    yb = y.astype(BF16)
    acc = DN_ALPHA * y
    for f in range(nf):
        hg = jnp.dot(yb, wg_ref[f], preferred_element_type=F32)
        hu = jnp.dot(yb, wu_ref[f], preferred_element_type=F32)
        h = (hg * jax.nn.sigmoid(hg) * hu).astype(BF16)
        acc = acc + jnp.dot(h, wd_ref[f], preferred_element_type=F32)
    o_ref[...] = _layer_norm(acc, g_ref[...], b_ref[...])


def _ffn_ln(y, wg, wu, wd, g, b, tm, tf):
    n, d = y.shape
    dff = wg.shape[1]
    nf = dff // tf
    wg3 = wg.astype(BF16).reshape(d, nf, tf).transpose(1, 0, 2)
    wu3 = wu.astype(BF16).reshape(d, nf, tf).transpose(1, 0, 2)
    wd3 = wd.astype(BF16).reshape(nf, tf, d)
    full3 = lambda i: (0, 0, 0)
    return pl.pallas_call(
        functools.partial(_ffn_ln_kernel, nf=nf),
        grid=(pl.cdiv(n, tm),),
        in_specs=[pl.BlockSpec((tm, d), lambda i: (i, 0)),
                  pl.BlockSpec(wg3.shape, full3), pl.BlockSpec(wu3.shape, full3),
                  pl.BlockSpec(wd3.shape, full3),
                  pl.BlockSpec((1, d), lambda i: (0, 0)), pl.BlockSpec((1, d), lambda i: (0, 0))],
        out_specs=pl.BlockSpec((tm, d), lambda i: (i, 0)),
        out_shape=jax.ShapeDtypeStruct((n, d), F32),
        compiler_params=_cparams(("parallel",)),
        name="ffn_ln",
    )(y, wg3, wu3, wd3, g.reshape(1, d), b.reshape(1, d))


def _sort_key(x):
    bits = lax.bitcast_convert_type(x, I32)
    return bits ^ ((bits >> 31) & 0x7FFFFFFF)


def _dsa_kernel(rb_ref, qi_ref, wi_ref, qa_ref, ki_ref, ka_ref, va_ref, bn_ref, o_ref,
                keys_ref, cut_ref, s_ref, p_ref, madd_ref, wbc_ref, cmax_ref, mpart_ref, alpha_ref,
                l_ref, acc_ref, seen_ref, *, tq, n_sel, e_base, e_step, q_base, q_step, coff, n_valid, padf):
    i = pl.program_id(1)
    ck = DSA_CK
    rbk = DSA_RB
    nt = ck // LANE
    e_end = e_base + e_step * i
    n_chunks = (e_end + ck - 1) // ck
    qpos = q_base + q_step * i + lax.broadcasted_iota(I32, (tq, 1), 0)
    bound = jnp.minimum(((((qpos + coff) >> CHUNK_SHIFT) + 1) << CHUNK_SHIFT) - coff, n_valid)
    lane_rb = lax.broadcasted_iota(I32, (rbk, ck), 1)
    lane128 = lax.broadcasted_iota(I32, (tq, LANE), 1)

    def key_rows(c):
        kpos0 = e_end - ck * (c + 1)
        return kpos0, pl.multiple_of(jnp.maximum(kpos0 + padf, 0), LANE)

    def tiles(x):
        return [x[:, t * LANE:(t + 1) * LANE] for t in range(nt)]

    qstack = jnp.concatenate([qi_ref[:, h * LANE:(h + 1) * LANE] for h in range(IDX_HEADS)], axis=0)
    for h in range(IDX_HEADS):
        wbc_ref[h] = jnp.broadcast_to(wi_ref[:, h:h + 1], (tq, LANE))
    cmax_ref[...] = jnp.full((tq, ck), -jnp.inf, F32)

    def idx_dot(c, dst):
        _, r = key_rows(c)
        s_ref[dst] = _nt(qstack, ki_ref[pl.ds(r, ck), :])

    def idx_keys(c, src):
        kpos0, _ = key_rows(c)
        for b in range(tq // rbk):
            r0 = b * rbk
            sc = None
            for h in range(IDX_HEADS):
                sh = jnp.maximum(s_ref[src, h * tq + r0:h * tq + r0 + rbk, :], 0.0)
                term = jnp.concatenate([wbc_ref[h, r0:r0 + rbk, :]] * nt, axis=1) * sh
                sc = term if sc is None else sc + term
            kpos = kpos0 + lane_rb
            allowed = (kpos >= 0) & (kpos < bound[r0:r0 + rbk])
            keys_ref[c, r0:r0 + rbk, :] = jnp.where(allowed, _sort_key(sc), INT_MIN)
            cmax_ref[r0:r0 + rbk, :] = jnp.maximum(cmax_ref[r0:r0 + rbk, :], jnp.where(allowed, sc, -jnp.inf))

    idx_dot(0, 0)

    def p1(j, carry):
        c = 2 * j
        idx_dot(c + 1, 1)
        idx_keys(c, 0)
        idx_dot(c + 2, 0)
        idx_keys(c + 1, 1)
        return carry

    lax.fori_loop(0, (n_chunks + 1) // 2, p1, 0)

    def count_ge(cand):
        def body(c, acc):
            u = keys_ref[c]
            for t in range(nt):
                acc = acc + jnp.where(u[:, t * LANE:(t + 1) * LANE] >= cand, 1.0, 0.0)
            return acc
        acc = lax.fori_loop(0, n_chunks, body, jnp.zeros((tq, LANE), F32))
        return jnp.sum(acc, axis=1, keepdims=True)

    cm = tiles(cmax_ref[...])
    if n_sel <= LANE:
        fold = [functools.reduce(jnp.maximum, cm)]
    elif n_sel <= 2 * LANE:
        fold = [jnp.maximum(cm[0], cm[2]), jnp.maximum(cm[1], cm[3])]
    else:
        fold = cm
    fmin = jnp.min(functools.reduce(jnp.minimum, fold), axis=1, keepdims=True)
    fmax = jnp.max(functools.reduce(jnp.maximum, cm), axis=1, keepdims=True)
    lo0 = _sort_key(fmin) - 1
    hi0 = _sort_key(fmax) + 2
    small = bound <= n_sel
    pos_side = count_ge(jnp.full((tq, 1), 1, I32)) >= n_sel
    at_zero = jnp.logical_not(pos_side) & (count_ge(jnp.zeros((tq, 1), I32)) >= n_sel)
    lo1 = jnp.where(pos_side, jnp.maximum(lo0, 1), lo0)
    top1 = jnp.where(pos_side, hi0, jnp.minimum(hi0, 0)) - 1
    fixed = small | at_zero
    nbits = jnp.where(fixed, 0, 32 - lax.clz(lo1 ^ top1))
    low_mask = (jnp.int32(1) << jnp.minimum(nbits, 31)) - 1
    t0 = jnp.where(small, INT_MIN + 1,
                   jnp.where(at_zero, 0, jnp.where(nbits >= 32, INT_MIN, lo1 & ~low_mask)))
    max_bits = jnp.max(nbits.astype(F32)).astype(I32)

    def bis(it, t):
        b = nbits - 1 - it
        cand = t + (jnp.int32(1) << jnp.maximum(b, 0))
        return jnp.where((b >= 0) & (count_ge(cand) >= n_sel), cand, t)

    thr = lax.fori_loop(0, max_bits, bis, t0)
    thr = jnp.maximum(thr, INT_MIN + 1)
    n_gt = count_ge(thr + 1)
    need = n_sel - n_gt
    excess = (count_ge(thr) - n_gt) > need

    cut_ref[...] = jnp.full((tq, LANE), 2 ** 30, I32)

    @pl.when(jnp.max(jnp.where(excess, 1.0, 0.0)) > 0.0)
    def _():
        rr = lax.broadcasted_iota(I32, (ck, ck), 0)
        cc = lax.broadcasted_iota(I32, (ck, ck), 1)
        upto = jnp.where(rr <= cc, 1.0, 0.0).astype(BF16)

        def tally(j, run):
            c = n_chunks - 1 - j
            seen_ref[c] = run
            u = keys_ref[c]
            for t in range(nt):
                run = run + jnp.where(u[:, t * LANE:(t + 1) * LANE] == thr, 1.0, 0.0)
            return run

        lax.fori_loop(0, n_chunks, tally, jnp.zeros((tq, LANE), F32))

        def locate(c, cut):
            kpos0, _ = key_rows(c)
            seen = jnp.sum(seen_ref[c], axis=1, keepdims=True)
            eq = jnp.where(keys_ref[c] == thr, 1.0, 0.0).astype(BF16)
            incl = jnp.dot(eq, upto, preferred_element_type=F32)
            below = jnp.sum(jnp.where(seen + incl < need, 1.0, 0.0), axis=1, keepdims=True)
            here = excess & (seen < need) & (seen + incl[:, ck - 1:ck] >= need)
            return jnp.where(here, kpos0 + below.astype(I32) + 1, cut)

        def locate4(j, cut):
            for k in range(4):
                cut = locate(jnp.minimum(4 * j + k, n_chunks - 1), cut)
            return cut

        cut = lax.fori_loop(0, (n_chunks + 3) // 4, locate4, jnp.full((tq, 1), 2 ** 30, I32))
        cut_ref[...] = jnp.broadcast_to(cut, (tq, LANE))

    cut = cut_ref[:, 0:1]

    r4 = A_HEADS // A_KV_HEADS
    qg = []
    for g in range(A_KV_HEADS):
        rows = []
        for j in range(r4):
            slot = qa_ref[:, j * LANE:(j + 1) * LANE]
            half = (lane128 < HEAD_DIM) if g == 0 else (lane128 >= HEAD_DIM)
            rows.append(jnp.where(half, slot, jnp.zeros_like(slot)))
        qg.append(jnp.concatenate(rows, axis=0))
    farb = [jnp.concatenate([jnp.full((tq, 1), rb_ref[g * r4 + j], F32) for j in range(r4)], axis=0)
            for g in range(A_KV_HEADS)]
    l_ref[...] = jnp.zeros(l_ref.shape, F32)
    acc_ref[...] = jnp.zeros(acc_ref.shape, F32)

    def logits_dot(c, g):
        _, r = key_rows(c)
        s_ref[g] = _nt(qg[g], ka_ref[pl.ds(r, ck), :])

    def select_mask(c):
        kpos0, _ = key_rows(c)
        for b in range(tq // rbk):
            r0 = b * rbk
            u = keys_ref[c, r0:r0 + rbk, :]
            t_b = thr[r0:r0 + rbk]
            sel = (u > t_b) | ((u == t_b) & (kpos0 + lane_rb < cut[r0:r0 + rbk]))
            madd_ref[r0:r0 + rbk, :] = jnp.where(sel, 0.0, NEG)

    def softmax_passes(g, m_old, near):
        for b in range(r4 * tq // rbk):
            r0 = b * rbk
            q0 = r0 % tq
            sm = s_ref[g, r0:r0 + rbk, :] + madd_ref[q0:q0 + rbk, :]
            if near:
                sm = sm + bn_ref[g, r0:r0 + rbk, :]
            s_ref[g, r0:r0 + rbk, :] = sm
            mpart_ref[g, r0:r0 + rbk, :] = functools.reduce(jnp.maximum, tiles(sm))
        m_blk = jnp.max(mpart_ref[g], axis=1, keepdims=True)
        if not near:
            m_blk = m_blk + farb[g]
        m_new = jnp.maximum(m_old, m_blk)
        alpha = jnp.exp2(m_old - m_new)
        alpha_ref[g] = jnp.broadcast_to(alpha, (r4 * tq, LANE))
        shift = m_new if near else m_new - farb[g]
        for b in range(r4 * tq // rbk):
            r0 = b * rbk
            p = jnp.exp2(s_ref[g, r0:r0 + rbk, :] - shift[r0:r0 + rbk])
            l_ref[g, r0:r0 + rbk, :] = (alpha_ref[g, r0:r0 + rbk, :] * l_ref[g, r0:r0 + rbk, :]
                                        + functools.reduce(jnp.add, tiles(p)))
            p_ref[g, r0:r0 + rbk, :] = p.astype(BF16)
        return m_new

    def value_dot(c, g):
        _, r = key_rows(c)
        acc_ref[g] = alpha_ref[g] * acc_ref[g] + jnp.dot(p_ref[g], va_ref[pl.ds(r, ck), :],
                                                        preferred_element_type=F32)

    m0 = jnp.full((r4 * tq, 1), NEG, F32)
    logits_dot(0, 0)
    logits_dot(0, 1)
    select_mask(0)
    m0n = softmax_passes(0, m0, True)
    value_dot(0, 0)
    logits_dot(1, 0)
    m1n = softmax_passes(1, m0, True)

    def p3(c, ms):
        value_dot(c - 1, 1)
        logits_dot(c, 1)
        select_mask(c)
        m_a = softmax_passes(0, ms[0], False)
        value_dot(c, 0)
        logits_dot(c + 1, 0)
        m_b = softmax_passes(1, ms[1], False)
        return (m_a, m_b)

    lax.fori_loop(1, n_chunks, p3, (m0n, m1n))
    value_dot(n_chunks - 1, 1)
    outs = [acc_ref[g] / jnp.sum(l_ref[g], axis=1, keepdims=True) for g in range(A_KV_HEADS)]
    for j in range(r4):
        lo = outs[0][j * tq:(j + 1) * tq]
        hi = outs[1][j * tq:(j + 1) * tq]
        o_ref[:, j * LANE:(j + 1) * LANE] = jnp.where(lane128 < HEAD_DIM, lo, hi).astype(BF16)


def _dsa(rb_far, qi, wi, qa, ki, ka, va, bn, *, nb, nq, tq, row0, n_sel, e_base, e_step,
         q_base, q_step, coff, n_valid, padf, n_chunks_max):
    rb0 = row0 // tq
    r4 = A_HEADS // A_KV_HEADS
    assert IDX_HEADS == r4 and tq % DSA_RB == 0
    qmap = lambda b, i: (rb0 + b * nq + i, 0)
    kmap = lambda b, i: (b, 0, 0)
    kern = functools.partial(_dsa_kernel, tq=tq, n_sel=n_sel, e_base=e_base, e_step=e_step,
                             q_base=q_base, q_step=q_step, coff=coff, n_valid=n_valid, padf=padf)
    return pl.pallas_call(
        kern,
        grid=(nb, nq),
        in_specs=[pl.BlockSpec(memory_space=pltpu.SMEM),
                  pl.BlockSpec((tq, qi.shape[1]), qmap),
                  pl.BlockSpec((tq, LANE), qmap),
                  pl.BlockSpec((tq, qa.shape[1]), qmap),
                  pl.BlockSpec((None,) + ki.shape[1:], kmap),
                  pl.BlockSpec((None,) + ka.shape[1:], kmap),
                  pl.BlockSpec((None,) + va.shape[1:], kmap),
                  pl.BlockSpec(bn.shape, lambda b, i: (0, 0, 0))],
        out_specs=pl.BlockSpec((tq, qa.shape[1]), lambda b, i: (b * nq + i, 0)),
        out_shape=jax.ShapeDtypeStruct((nb * nq * tq, qa.shape[1]), BF16),
        scratch_shapes=[pltpu.VMEM((n_chunks_max + 1, tq, DSA_CK), I32),
                        pltpu.VMEM((tq, LANE), I32),
                        pltpu.VMEM((2, r4 * tq, DSA_CK), F32),
                        pltpu.VMEM((A_KV_HEADS, r4 * tq, DSA_CK), BF16),
                        pltpu.VMEM((tq, DSA_CK), F32),
                        pltpu.VMEM((IDX_HEADS, tq, LANE), F32),
                        pltpu.VMEM((tq, DSA_CK), F32),
                        pltpu.VMEM((A_KV_HEADS, r4 * tq, LANE), F32),
                        pltpu.VMEM((A_KV_HEADS, r4 * tq, LANE), F32),
                        pltpu.VMEM((A_KV_HEADS, r4 * tq, LANE), F32),
                        pltpu.VMEM((A_KV_HEADS, r4 * tq, LANE), F32),
                        pltpu.VMEM((n_chunks_max, tq, LANE), F32)],
        compiler_params=_cparams(("parallel", "arbitrary")),
        name="dsa",
    )(rb_far, qi, wi, qa, ki, ka, va, bn)


def _sb_kernel(tri_ref, q_ref, k_ref, v_ref, o_ref, *, tq, e_base, e_step, q_base, q_step, padf, n_valid):
    i = pl.program_id(2)
    bw = SB_BAND
    npp = q_ref.shape[1] // LANE
    nh = 2 * npp
    e_end = e_base + e_step * i
    n_bands = (e_end + bw - 1) // bw
    qpos = q_base + q_step * i + lax.broadcasted_iota(I32, (tq, 1), 0)
    lane = lax.broadcasted_iota(I32, (tq, LANE), 1)
    qh = []
    for p in range(npp):
        q = q_ref[:, p * LANE:(p + 1) * LANE]
        zero = jnp.zeros_like(q)
        qh += [jnp.where(lane < HEAD_DIM, q, zero), jnp.where(lane >= HEAD_DIM, q, zero)]
    lane_bw = lax.broadcasted_iota(I32, (tq, bw), 1)

    def cond(st):
        return (st[0] < n_bands) & (st[1] > SB_STOP)

    def body(st):
        m, _, carry, acc = st
        kpos0 = e_end - bw * (m + 1)
        r = pl.multiple_of(kpos0 + padf, LANE)
        kpos = kpos0 + lane_bw
        before = (kpos < qpos) & (kpos >= 0) & (kpos < n_valid)
        zs, lss, lks, parts = [], [], [], []
        for h in range(nh):
            kt = k_ref[pl.ds(r, bw), (h // 2) * LANE:(h // 2 + 1) * LANE]
            z = _nt(qh[h], kt)
            ls = -(jnp.maximum(z, 0.0) + jnp.log1p(jnp.exp(-jnp.abs(z))))
            lk = jnp.where(before, ls, 0.0)
            hi = lk.astype(BF16)
            parts += [hi, (lk - hi.astype(F32)).astype(BF16)]
            zs.append(z)
            lss.append(ls)
            lks.append(lk)
        sums = jnp.dot(jnp.concatenate(parts, axis=0), tri_ref[...], preferred_element_type=F32)
        new_c, new_a = [], []
        worst = jnp.float32(-jnp.inf)
        for h in range(nh):
            vt = v_ref[pl.ds(r, bw), (h // 2) * LANE:(h // 2 + 1) * LANE]
            bl = sums[2 * h * tq:(2 * h + 1) * tq] + sums[(2 * h + 1) * tq:(2 * h + 2) * tq]
            w = jnp.where(before, jnp.exp(lss[h] + zs[h] + bl + carry[h]), 0.0)
            new_a.append(acc[h] + jnp.dot(w.astype(BF16), vt, preferred_element_type=F32))
            c_n = carry[h] + bl[:, 0:1] + lks[h][:, 0:1]
            new_c.append(c_n)
            worst = jnp.maximum(worst, jnp.max(c_n))
        return (m + 1, worst, tuple(new_c), tuple(new_a))

    init = (jnp.int32(0), jnp.float32(0.0),
            tuple(jnp.zeros((tq, 1), F32) for _ in range(nh)),
            tuple(jnp.zeros((tq, LANE), F32) for _ in range(nh)))
    _, _, _, acc = lax.while_loop(cond, body, init)
    for p in range(npp):
        o_ref[:, p * LANE:(p + 1) * LANE] = jnp.where(lane < HEAD_DIM, acc[2 * p], acc[2 * p + 1]).astype(BF16)


def _stick_break(q, k, v, *, nb, nq, tq, row0, e_base, e_step, q_base, q_step, padf, n_valid):
    rb0 = row0 // tq
    wq = SB_PAIRS * LANE
    npair = q.shape[1] // wq
    kern = functools.partial(_sb_kernel, tq=tq, e_base=e_base, e_step=e_step, q_base=q_base,
                             q_step=q_step, padf=padf, n_valid=n_valid)
    rr = lax.broadcasted_iota(I32, (SB_BAND, SB_BAND), 0)
    cc = lax.broadcasted_iota(I32, (SB_BAND, SB_BAND), 1)
    tri = jnp.where(rr > cc, 1.0, 0.0).astype(BF16)
    kspec = pl.BlockSpec((None, k.shape[1], wq), lambda b, p, i: (b, 0, p))
    return pl.pallas_call(
        kern,
        grid=(nb, npair, nq),
        in_specs=[pl.BlockSpec(tri.shape, lambda b, p, i: (0, 0)),
                  pl.BlockSpec((tq, wq), lambda b, p, i: (rb0 + b * nq + i, p)), kspec, kspec],
        out_specs=pl.BlockSpec((tq, wq), lambda b, p, i: (b * nq + i, p)),
        out_shape=jax.ShapeDtypeStruct((nb * nq * tq, q.shape[1]), BF16),
        compiler_params=_cparams(("parallel", "parallel", "arbitrary")),
        name="stick_break",
    )(tri, q, k, v)


def _swa_kernel(sink_ref, q_ref, k_ref, v_ref, bias_ref, o_ref,
                *, tq, ck, r_base, r_step, p_base, p_step, q_base, q_step, coff, n_valid):
    i = pl.program_id(1)
    r0 = pl.multiple_of(r_base + r_step * i, 16)
    kpos0 = p_base + p_step * i
    kt = k_ref[pl.ds(r0, ck), :]
    vt = v_ref[pl.ds(r0, ck), :]
    qpos = q_base + q_step * i + lax.broadcasted_iota(I32, (tq, 1), 0)
    kpos = kpos0 + lax.broadcasted_iota(I32, (tq, ck), 1)
    qc = (qpos + coff) >> CHUNK_SHIFT
    hi_b = jnp.minimum(((qc + 1) << CHUNK_SHIFT) - coff, n_valid)
    lo_b = jnp.maximum(((qc - WIN_CHUNKS) << CHUNK_SHIFT) - coff, 0)
    r8 = C_HEADS // C_KV_HEADS
    madd8 = jnp.concatenate([jnp.where((kpos >= lo_b) & (kpos < hi_b), 0.0, NEG)] * r8, axis=0)
    lane = lax.broadcasted_iota(I32, (tq, LANE), 1)
    outs = []
    for g in range(C_KV_HEADS):
        rows = []
        for j in range(r8):
            slot = q_ref[:, j * LANE:(j + 1) * LANE]
            half = (lane < HEAD_DIM) if g == 0 else (lane >= HEAD_DIM)
            rows.append(jnp.where(half, slot, jnp.zeros_like(slot)))
        sink = jnp.concatenate([jnp.full((tq, 1), sink_ref[g * r8 + j], F32) for j in range(r8)], axis=0)
        s = _nt(jnp.concatenate(rows, axis=0), kt) + bias_ref[g] + madd8
        m = jnp.maximum(jnp.max(s, axis=1, keepdims=True), sink)
        e = jnp.exp2(s - m)
        den = jnp.sum(e, axis=1, keepdims=True) + jnp.exp2(sink - m)
        outs.append(jnp.dot(e.astype(BF16), vt, preferred_element_type=F32) / den)
    for j in range(r8):
        lo = outs[0][j * tq:(j + 1) * tq]
        hi = outs[1][j * tq:(j + 1) * tq]
        o_ref[:, j * LANE:(j + 1) * LANE] = jnp.where(lane < HEAD_DIM, lo, hi).astype(BF16)


def _swa(sinks, q, k, v, bias, *, nb, nq, tq, row0, ck, r_base, r_step, p_base, p_step,
         q_base, q_step, coff, n_valid):
    rb0 = row0 // tq
    kern = functools.partial(_swa_kernel, tq=tq, ck=ck, r_base=r_base, r_step=r_step, p_base=p_base,
                             p_step=p_step, q_base=q_base, q_step=q_step, coff=coff, n_valid=n_valid)
    kmap = lambda b, i: (b, 0, 0)
    return pl.pallas_call(
        kern,
        grid=(nb, nq),
        in_specs=[pl.BlockSpec(memory_space=pltpu.SMEM),
                  pl.BlockSpec((tq, q.shape[1]), lambda b, i: (rb0 + b * nq + i, 0)),
                  pl.BlockSpec((None,) + k.shape[1:], kmap),
                  pl.BlockSpec((None,) + v.shape[1:], kmap),
                  pl.BlockSpec(bias.shape, lambda b, i: (0, 0, 0))],
        out_specs=pl.BlockSpec((tq, q.shape[1]), lambda b, i: (b * nq + i, 0)),
        out_shape=jax.ShapeDtypeStruct((nb * nq * tq, q.shape[1]), BF16),
        compiler_params=_cparams(("parallel", "arbitrary")),
        name="swa",
    )(sinks, q, k, v, bias)


def _router_kernel(y_ref, rh_ref, rl_ref, idx_ref, gate_ref):
    y = y_ref[...]
    yh = y.astype(BF16)
    yl = (y - yh.astype(F32)).astype(BF16)
    rh = rh_ref[...]
    logits = (jnp.dot(yh, rh, preferred_element_type=F32) + jnp.dot(yl, rh, preferred_element_type=F32)
              + jnp.dot(yh, rl_ref[...], preferred_element_type=F32))
    lane = lax.broadcasted_iota(I32, logits.shape, 1)
    logits = jnp.where(lane < N_EXPERTS, logits, -jnp.inf)
    m1 = jnp.max(logits, axis=1, keepdims=True)
    i1 = jnp.min(jnp.where(logits == m1, lane, LANE), axis=1, keepdims=True)
    rest = jnp.where(lane == i1, -jnp.inf, logits)
    m2 = jnp.max(rest, axis=1, keepdims=True)
    i2 = jnp.min(jnp.where(rest == m2, lane, LANE), axis=1, keepdims=True)
    e2 = jnp.exp(m2 - m1)
    den = 1.0 + e2
    idx_ref[...] = jnp.where(lane == 0, i1, jnp.where(lane == 1, i2, 0))
    gate_ref[...] = jnp.where(lane == 0, 1.0 / den, jnp.where(lane == 1, e2 / den, 0.0))


def _router(y, router, tm):
    n, d = y.shape
    rpad = jnp.pad(router.astype(F32), ((0, 0), (0, LANE - router.shape[1])))
    rh = rpad.astype(BF16)
    rl = (rpad - rh.astype(F32)).astype(BF16)
    return pl.pallas_call(
        _router_kernel,
        grid=(pl.cdiv(n, tm),),
        in_specs=[pl.BlockSpec((tm, d), lambda i: (i, 0)),
                  pl.BlockSpec((d, LANE), lambda i: (0, 0)), pl.BlockSpec((d, LANE), lambda i: (0, 0))],
        out_specs=[pl.BlockSpec((tm, LANE), lambda i: (i, 0)), pl.BlockSpec((tm, LANE), lambda i: (i, 0))],
        out_shape=[jax.ShapeDtypeStruct((n, LANE), I32), jax.ShapeDtypeStruct((n, LANE), F32)],
        compiler_params=_cparams(("parallel",)),
        name="router",
    )(y, rh, rl)


def _moe_kernel(be_ref, nu_ref, x_ref, wg_ref, wu_ref, wd_ref, o_ref, acc_ref, xb_ref):
    t = pl.program_id(0)
    f = pl.program_id(1)

    @pl.when(f == 0)
    def _():
        acc_ref[...] = jnp.zeros_like(acc_ref)
        xb_ref[...] = x_ref[...].astype(BF16)

    @pl.when(t < nu_ref[0])
    def _():
        xb = xb_ref[...]
        hg = jnp.dot(xb, wg_ref[...], preferred_element_type=F32)
        hu = jnp.dot(xb, wu_ref[...], preferred_element_type=F32)
        h = (hg * jax.nn.sigmoid(hg) * hu).astype(BF16)
        acc_ref[...] += jnp.dot(h, wd_ref[...], preferred_element_type=F32)

    @pl.when(f == pl.num_programs(1) - 1)
    def _():
        o_ref[...] = acc_ref[...]


def _moe_experts(block_expert, n_used, xs, wg, wu, wd, tmb, tf):
    n_slots, d = xs.shape
    de = wg.shape[2]
    grid_spec = pltpu.PrefetchScalarGridSpec(
        num_scalar_prefetch=2,
        grid=(n_slots // tmb, de // tf),
        in_specs=[pl.BlockSpec((tmb, d), lambda t, f, be, nu: (t, 0)),
                  pl.BlockSpec((None, d, tf), lambda t, f, be, nu: (be[t], 0, f)),
                  pl.BlockSpec((None, d, tf), lambda t, f, be, nu: (be[t], 0, f)),
                  pl.BlockSpec((None, tf, d), lambda t, f, be, nu: (be[t], f, 0))],
        out_specs=pl.BlockSpec((tmb, d), lambda t, f, be, nu: (t, 0)),
        scratch_shapes=[pltpu.VMEM((tmb, d), F32), pltpu.VMEM((tmb, d), BF16)],
    )
    return pl.pallas_call(
        _moe_kernel,
        grid_spec=grid_spec,
        out_shape=jax.ShapeDtypeStruct((n_slots, d), F32),
        compiler_params=_cparams(("parallel", "arbitrary")),
        name="moe_experts",
    )(block_expert, n_used, xs, wg, wu, wd)


def _row_copy(src_hbm, src_row, dst, dst_row, sem):
    return pltpu.make_async_copy(src_hbm.at[pl.ds(src_row, 1)], dst.at[pl.ds(dst_row, 1)], sem)


def _dispatch_kernel(slot_ref, x_ref, init_hbm, xs_hbm, sem, *, tt):
    del init_hbm

    def start(j, carry):
        for k in range(TOP_K):
            _row_copy(x_ref, j, xs_hbm, slot_ref[0, 0, TOP_K * j + k], sem).start(priority=k % 2)
        return carry

    def wait(j, carry):
        _row_copy(x_ref, 0, xs_hbm, 0, sem).wait()
        return carry

    lax.fori_loop(0, tt, start, 0, unroll=8)
    lax.fori_loop(0, TOP_K * tt, wait, 0, unroll=8)


def _dispatch(x, slot3, n_slots, tt):
    n, d = x.shape
    init = jnp.zeros((n_slots, d), x.dtype)
    return pl.pallas_call(
        functools.partial(_dispatch_kernel, tt=tt),
        grid=(n // tt,),
        in_specs=[pl.BlockSpec((1, 1, TOP_K * tt), lambda i: (i, 0, 0), memory_space=pltpu.SMEM),
                  pl.BlockSpec((tt, d), lambda i: (i, 0)), pl.BlockSpec(memory_space=pl.ANY)],
        out_specs=pl.BlockSpec(memory_space=pl.ANY),
        out_shape=jax.ShapeDtypeStruct((n_slots, d), x.dtype),
        scratch_shapes=[pltpu.SemaphoreType.DMA],
        input_output_aliases={2: 0},
        compiler_params=_cparams(("arbitrary",)),
        name="moe_dispatch",
    )(slot3, x, init)


def _combine_ln_kernel(slot_ref, y_ref, gate_ref, g_ref, b_ref, outs_hbm, o_ref, buf_ref, sem, *, tt):
    def start(j, carry):
        for k in range(TOP_K):
            _row_copy(outs_hbm, slot_ref[0, 0, TOP_K * j + k], buf_ref.at[k], j, sem).start(priority=k % 2)
        return carry

    def wait(j, carry):
        _row_copy(outs_hbm, 0, buf_ref.at[0], 0, sem).wait()
        return carry

    lax.fori_loop(0, tt, start, 0, unroll=8)
    lax.fori_loop(0, TOP_K * tt, wait, 0, unroll=8)
    mo = gate_ref[:, 0:1] * buf_ref[0]
    for k in range(1, TOP_K):
        mo = mo + gate_ref[:, k:k + 1] * buf_ref[k]
    o_ref[...] = _layer_norm(DN_ALPHA * y_ref[...] + mo, g_ref[...], b_ref[...])


def _combine_ln(y, outs, slot3, gate_p, g, b, tt):
    n, d = y.shape
    row = pl.BlockSpec((tt, d), lambda i: (i, 0))
    vec = pl.BlockSpec((1, d), lambda i: (0, 0))
    return pl.pallas_call(
        functools.partial(_combine_ln_kernel, tt=tt),
        grid=(n // tt,),
        in_specs=[pl.BlockSpec((1, 1, TOP_K * tt), lambda i: (i, 0, 0), memory_space=pltpu.SMEM),
                  row, pl.BlockSpec((tt, LANE), lambda i: (i, 0)), vec, vec,
                  pl.BlockSpec(memory_space=pl.ANY)],
        out_specs=row,
        out_shape=jax.ShapeDtypeStruct((n, d), F32),
        scratch_shapes=[pltpu.VMEM((TOP_K, tt, d), F32), pltpu.SemaphoreType.DMA],
        compiler_params=_cparams(("arbitrary",)),
        name="moe_combine_ln",
    )(slot3, y, gate_p, g.reshape(1, d), b.reshape(1, d), outs)


def _moe(y, router, w_gate, w_up, w_down, g, b, tm, tmb, tf, tt):
    n, d = y.shape
    assert n % tt == 0
    idx_p, gate_p = _router(y, router, tm)
    e_flat = idx_p[:, :TOP_K].reshape(-1)
    onehot = (e_flat[:, None] == jnp.arange(N_EXPERTS, dtype=I32)[None, :]).astype(I32)
    rank = jnp.sum((jnp.cumsum(onehot, axis=0) - onehot) * onehot, axis=1)
    counts = jnp.sum(onehot, axis=0)
    padded = (counts + tmb - 1) // tmb * tmb
    pad_end = jnp.cumsum(padded)
    slot = ((pad_end - padded)[e_flat] + rank).astype(I32)
    n_blocks = -(-(TOP_K * n) // tmb) + N_EXPERTS
    n_slots = n_blocks * tmb
    block_expert = jnp.minimum(
        jnp.searchsorted(pad_end, jnp.arange(n_blocks, dtype=I32) * tmb, side="right"),
        N_EXPERTS - 1).astype(I32)
    n_used = (pad_end[-1] // tmb).astype(I32).reshape(1)
    slot3 = slot.reshape(n // tt, 1, TOP_K * tt)
    xs = _dispatch(y, slot3, n_slots, tt)
    outs = _moe_experts(block_expert, n_used, xs, w_gate.astype(BF16), w_up.astype(BF16),
                        w_down.astype(BF16), tmb, tf)
    return _combine_ln(y, outs, slot3, gate_p, g, b, tt)


def _t5_bucket(rel):
    half = NUM_BUCKETS // 2
    exact = half // 2
    n = jnp.abs(rel)
    far = exact + sum((n >= t).astype(I32) for t in (12, 16, 23, 32, 46, 64, 91))
    return jnp.where(rel > 0, half, 0) + jnp.where(n < exact, n, far)


def _bias_tile(rel_bias, n_groups, per_group, tq, ck, d0):
    d = d0 + jnp.arange(ck, dtype=I32)[None, :] - jnp.arange(tq, dtype=I32)[:, None]
    onehot = (_t5_bucket(d)[..., None] == jnp.arange(NUM_BUCKETS, dtype=I32)).astype(F32)
    tile = jnp.einsum("qkb,bh->qkh", onehot, rel_bias.astype(F32),
                      precision=lax.Precision.HIGHEST)
    tile = jnp.transpose(tile[:, :, :n_groups * per_group], (2, 0, 1))
    return tile.reshape(n_groups, per_group * tq, ck)


def _cols(w, ranges):
    parts = []
    for r in ranges:
        if isinstance(r, int):
            parts.append(jnp.zeros((w.shape[0], r), w.dtype))
        else:
            parts.append(w[:, r[0]:r[1]])
    return jnp.concatenate(parts, axis=1)


def _head_pair_order(n_heads):
    half = n_heads // 2
    order = []
    for j in range(half):
        order += [j, half + j]
    return order


L0_SPECS = ((0, 512, ("bf16e",)), (512, 128, ("f32", "bf16")), (640, 128, ("f32", "bf16")),
            (768, 512, ("hilo",)), (1280, 128, ("f32", "bf16")), (1408, 128, ("wi",)),
            (1536, 512, ("bf16s",)), (2048, 512, ("f32", "bf16")), (2560, 512, ("f32", "bf16")))
L1_SPECS = ((0, 1024, ("bf16e",)), (1024, 128, ("f32", "bf16")), (1152, 128, ("f32", "bf16")))


def _l0_weight(w_in):
    hd = HEAD_DIM
    rng = [(hd * h, hd * h + hd) for h in _head_pair_order(A_HEADS)]
    rng += [(512, 640), (640, 768)]
    for h in range(IDX_HEADS):
        rng += [(768 + hd * h, 768 + hd * h + hd)] * 2
    rng += [(1024, 1088)] * 2
    rng += [(1088, 1092), LANE - IDX_HEADS]
    rng += [(1092, 1604), (1604, 2116), (2116, 2628)]
    return _cols(w_in, rng).astype(BF16)


def _l1_weight(w_in):
    hd = HEAD_DIM
    rng = [(hd * h, hd * h + hd) for h in _head_pair_order(C_HEADS)]
    rng += [(1024, 1152), (1152, 1280)]
    return _cols(w_in, rng).astype(BF16)


def _perm_rows(w, n_heads):
    return jnp.concatenate([w[HEAD_DIM * h:HEAD_DIM * (h + 1)] for h in _head_pair_order(n_heads)], axis=0)


def _front_pad(a, nb, rows_in, front, rows_out):
    a = a.reshape(nb, rows_in, a.shape[-1])
    return jnp.pad(a, ((0, 0), (front, rows_out - front - rows_in), (0, 0)))


def _layer0_attention(x_prompt, x_sample, cache_a_k, cache_a_v, cache_a_idx_k, cache_b_k, cache_b_v,
                      meta_tokens, rel_bias, l0_w_in):
    nb, seq, d = x_prompt.shape
    t = N_META + seq
    nq = -(-t // QBLK)
    tp = nq * QBLK
    db, ds, _ = x_sample.shape
    past = cache_a_k.shape[1]
    assert d == D_MODEL and past % LANE == 0 and ds % DSA_RB == 0 and ds <= 64
    np_, ns = nb * tp, db * ds
    tm = 512

    meta = jnp.broadcast_to(meta_tokens[None].astype(x_prompt.dtype), (nb, N_META, d))
    hp = jnp.concatenate([meta, x_prompt, jnp.zeros((nb, tp - t, d), x_prompt.dtype)], axis=1)
    x_all = jnp.concatenate([hp.reshape(np_, d), x_sample.reshape(ns, d)], axis=0)
    rel_bias = rel_bias.astype(F32)
    rb_far = rel_bias[NUM_BUCKETS // 2 - 1] * LOG2E

    (qa16, ka32, ka16, va32, va16, qi16, ki32, ki16, wi32, qb16, kb32, kb16, vb32, vb16) = _project(
        x_all, _l0_weight(l0_w_in), L0_SPECS, tm)

    n_sel_p = min(TOPK_MAX, (t - N_META) // 4)
    kr = QBLK * (nq + 1) + DSA_PADF
    ncm = -(-(QBLK * (nq + 1)) // DSA_CK)
    pk = [_front_pad(a[:np_], nb, tp, DSA_PADF, kr) for a in (ki16, ka16, va16)]
    bn_p = _bias_tile(rel_bias, A_KV_HEADS, A_HEADS // A_KV_HEADS, QBLK, DSA_CK, -(DSA_CK // 2)) * LOG2E
    oa_p = _dsa(rb_far, qi16, wi32, qa16, *pk, bn_p, nb=nb, nq=nq, tq=QBLK, row0=0, n_sel=n_sel_p,
                e_base=2 * QBLK, e_step=QBLK, q_base=0, q_step=QBLK, coff=PROMPT_COFF,
                n_valid=t, padf=DSA_PADF, n_chunks_max=ncm)

    n_keys = past + ds
    n_sel_s = min(TOPK_MAX, n_keys // 4)
    e_s = -(-n_keys // LANE) * LANE
    ncs = -(-e_s // DSA_CK)
    krs = ncs * DSA_CK
    fs = krs - e_s

    def cat_keys(cache, new, width_dup):
        c = cache.reshape(db, past, -1).astype(BF16)
        if width_dup:
            c = jnp.concatenate([c, c], axis=-1)
        a = jnp.concatenate([c, new[np_:].reshape(db, ds, -1)], axis=1)
        return jnp.pad(a, ((0, 0), (fs, krs - fs - n_keys), (0, 0)))

    sk = [cat_keys(cache_a_idx_k, ki16, True), cat_keys(cache_a_k, ka16, False),
          cat_keys(cache_a_v, va16, False)]
    bn_s = _bias_tile(rel_bias, A_KV_HEADS, A_HEADS // A_KV_HEADS, ds, DSA_CK, e_s - DSA_CK - past) * LOG2E
    oa_s = _dsa(rb_far, qi16, wi32, qa16, *sk, bn_s, nb=db, nq=1, tq=ds, row0=np_, n_sel=n_sel_s,
                e_base=e_s, e_step=0, q_base=past, q_step=0, coff=0, n_valid=n_keys, padf=fs,
                n_chunks_max=ncs)

    pfb = SB_BAND - QBLK
    kb_p = _front_pad(kb16[:np_], nb, tp, pfb, tp + pfb)
    vb_p = _front_pad(vb16[:np_], nb, tp, pfb, tp + pfb)
    ob_p = _stick_break(qb16, kb_p, vb_p, nb=nb, nq=nq, tq=QBLK, row0=0, e_base=QBLK, e_step=QBLK,
                        q_base=0, q_step=QBLK, padf=pfb, n_valid=t)
    rows_s = -(-e_s // SB_BAND) * SB_BAND
    pfs = rows_s - e_s

    def cat_b(cache, new):
        a = jnp.concatenate([cache.reshape(db, past, -1).astype(BF16), new[np_:].reshape(db, ds, -1)], axis=1)
        return jnp.pad(a, ((0, 0), (pfs, rows_s - pfs - n_keys), (0, 0)))

    ob_s = _stick_break(qb16, cat_b(cache_b_k, kb16), cat_b(cache_b_v, vb16), nb=db, nq=1, tq=ds,
                        row0=np_, e_base=e_s, e_step=0, q_base=past, q_step=0, padf=pfs, n_valid=n_keys)

    oa = jnp.concatenate([oa_p, oa_s], axis=0)
    ob = jnp.concatenate([ob_p, ob_s], axis=0)
    return dict(x_all=x_all, oa=oa, ob=ob, np=np_, tp=tp, t=t, nq=nq, rel_bias=rel_bias,
                ka32=ka32, va32=va32, ki32=ki32, kb32=kb32, vb32=vb32)


def kernel(x_prompt, x_sample, cache_a_k, cache_a_v, cache_a_idx_k, cache_b_k, cache_b_v, cache_c_k, cache_c_v, meta_tokens, rel_bias, l0_w_in, l0_w_out, l0_ln1_g, l0_ln1_b, l0_w_gate, l0_w_up, l0_w_down, l0_ln2_g, l0_ln2_b, l1_w_in, l1_sinks, l1_w_out, l1_ln1_g, l1_ln1_b, l1_router, l1_w_gate, l1_w_up, l1_w_down, l1_ln2_g, l1_ln2_b):
    a0 = _layer0_attention(x_prompt, x_sample, cache_a_k, cache_a_v, cache_a_idx_k, cache_b_k, cache_b_v,
                           meta_tokens, rel_bias, l0_w_in)
    x_all, oa, ob, np_, tp, t, nq, rel_bias = (a0[k] for k in ("x_all", "oa", "ob", "np", "tp", "t", "nq", "rel_bias"))
    ka32, va32, ki32, kb32, vb32 = (a0[k] for k in ("ka32", "va32", "ki32", "kb32", "vb32"))
    nb, _, d = x_prompt.shape
    db, ds, _ = x_sample.shape
    past = cache_a_k.shape[1]
    tm = 512
    w_out0 = l0_w_out.astype(BF16)
    half0 = A_HEADS * HEAD_DIM
    y0 = _mix_ln(x_all, [oa, ob], [_perm_rows(w_out0[:half0], A_HEADS), w_out0[half0:]],
                 l0_ln1_g, l0_ln1_b, tm)
    h1 = _ffn_ln(y0, l0_w_gate, l0_w_up, l0_w_down, l0_ln2_g, l0_ln2_b, tm, 256)

    oc, kc32, vc32 = _layer1_attention(h1, cache_c_k, cache_c_v, rel_bias, l1_w_in, l1_sinks,
                                       nb=nb, nq=nq, t=t, db=db, ds=ds, past=past, tm=tm)
    y1 = _mix_ln(h1, [oc], [_perm_rows(l1_w_out.astype(BF16), C_HEADS)], l1_ln1_g, l1_ln1_b, tm)
    n_all = y1.shape[0]
    tt = next(c for c in (256, 128, 64, 32, 16, 8) if n_all % c == 0)
    h2 = _moe(y1, l1_router, l1_w_gate, l1_w_up, l1_w_down, l1_ln2_g, l1_ln2_b, tm, 1024, 512, tt)
    return _assemble(h2, ka32, va32, ki32, kb32, vb32, kc32, vc32, cache_c_k, cache_c_v,
                     nb=nb, tp=tp, t=t, db=db, ds=ds)


def _layer1_attention(h1, cache_c_k, cache_c_v, rel_bias, l1_w_in, l1_sinks, *, nb, nq, t, db, ds, past, tm):
    tp = nq * QBLK
    np_ = nb * tp
    qc16, kc32, kc16, vc32, vc16 = _project(h1, _l1_weight(l1_w_in), L1_SPECS, tm)
    ckp = 4 * QBLK
    krc = QBLK * (nq - 1) + ckp
    kc_p = _front_pad(kc16[:np_], nb, tp, 2 * QBLK, max(krc, tp + 2 * QBLK))
    vc_p = _front_pad(vc16[:np_], nb, tp, 2 * QBLK, max(krc, tp + 2 * QBLK))
    bc_p = _bias_tile(rel_bias, C_KV_HEADS, C_HEADS // C_KV_HEADS, QBLK, ckp, -2 * QBLK) * LOG2E
    sinks = l1_sinks.astype(F32) * LOG2E
    oc_p = _swa(sinks, qc16, kc_p, vc_p, bc_p, nb=nb, nq=nq, tq=QBLK, row0=0, ck=ckp, r_base=0,
                r_step=QBLK, p_base=-2 * QBLK, p_step=QBLK, q_base=0, q_step=QBLK, coff=PROMPT_COFF, n_valid=t)

    buf = cache_c_k.shape[1]
    cks = -(-(buf + ds) // LANE) * LANE

    def cat_c(cache, new):
        a = jnp.concatenate([cache.reshape(db, buf, -1).astype(BF16), new[np_:].reshape(db, ds, -1)], axis=1)
        return jnp.pad(a, ((0, 0), (0, cks - buf - ds), (0, 0)))

    bc_s = _bias_tile(rel_bias, C_KV_HEADS, C_HEADS // C_KV_HEADS, ds, cks, -buf) * LOG2E
    oc_s = _swa(sinks, qc16, cat_c(cache_c_k, kc16), cat_c(cache_c_v, vc16), bc_s, nb=db, nq=1, tq=ds,
                row0=np_, ck=cks, r_base=0, r_step=0, p_base=past - buf, p_step=0, q_base=past,
                q_step=0, coff=0, n_valid=past + ds)
    return jnp.concatenate([oc_p, oc_s], axis=0), kc32, vc32


def _assemble(h2, ka32, va32, ki32, kb32, vb32, kc32, vc32, cache_c_k, cache_c_v, *, nb, tp, t, db, ds):
    np_ = nb * tp
    d = h2.shape[1]

    def pr(a, heads):
        a = a[:np_].reshape(nb, tp, -1)[:, :t]
        return a.reshape(nb, t, heads, HEAD_DIM) if heads else a[..., :HEAD_DIM]

    def sm(a, heads):
        a = a[np_:].reshape(db, ds, -1)
        return a.reshape(db, ds, heads, HEAD_DIM) if heads else a[..., :HEAD_DIM]

    y_prompt = h2[:np_].reshape(nb, tp, d)[:, N_META:t]
    y_sample = h2[np_:].reshape(db, ds, d)
    bufp = min(WINDOW, t)
    p_ck = pr(kc32, C_KV_HEADS)[:, t - bufp:]
    p_cv = pr(vc32, C_KV_HEADS)[:, t - bufp:]
    s_ck = jnp.concatenate([cache_c_k, sm(kc32, C_KV_HEADS)], axis=1)[:, ds:]
    s_cv = jnp.concatenate([cache_c_v, sm(vc32, C_KV_HEADS)], axis=1)[:, ds:]
    return (y_prompt, y_sample,
            pr(ka32, A_KV_HEADS), pr(va32, A_KV_HEADS), pr(ki32, 0), pr(kb32, B_HEADS), pr(vb32, B_HEADS),
            p_ck, p_cv,
            sm(ka32, A_KV_HEADS), sm(va32, A_KV_HEADS), sm(ki32, 0), sm(kb32, B_HEADS), sm(vb32, B_HEADS),
            s_ck, s_cv)
```

```python
import functools

import jax
import jax.numpy as jnp
from jax import lax
from jax.experimental import pallas as pl
from jax.experimental.pallas import tpu as pltpu

F32 = jnp.float32
BF16 = jnp.bfloat16
I32 = jnp.int32

D_MODEL = 1024
CHUNK_SHIFT = 6
N_META = 16
HEAD_DIM = 64
A_HEADS = 8
A_KV_HEADS = 2
IDX_HEADS = 4
TOPK_MAX = 256
B_HEADS = 8
C_HEADS = 16
C_KV_HEADS = 2
WINDOW = 128
WIN_CHUNKS = 2
NUM_BUCKETS = 32
N_EXPERTS = 8
TOP_K = 2
LN_EPS = 1e-5
DEPTH = 2
DN_ALPHA = (2.0 * DEPTH) ** 0.25

LANE = 128
QBLK = 128
DSA_CK = 512
DSA_PADF = 384
DSA_RB = 32
LOG2E = 1.4426950408889634
SB_BAND = 512
SB_PAIRS = 2
SB_STOP = -120.0
NEG = -1e30
PROMPT_COFF = 64 - N_META
INT_MIN = -2 ** 31
VMEM_LIMIT = 56 * 1024 * 1024


def _cparams(sem):
    return pltpu.CompilerParams(dimension_semantics=sem, vmem_limit_bytes=VMEM_LIMIT)


def _nt(a, b):
    return lax.dot_general(a, b, (((1,), (1,)), ((), ())), preferred_element_type=F32)


def _layer_norm(v, g, b):
    mu = jnp.mean(v, axis=-1, keepdims=True)
    c = v - mu
    var = jnp.mean(c * c, axis=-1, keepdims=True)
    return c * lax.rsqrt(var + LN_EPS) * g + b


def _proj_kernel(x_ref, w_ref, *out_refs, specs):
    xb = x_ref[...].astype(BF16)
    k = 0
    for c0, width, kinds in specs:
        acc = jnp.dot(xb, w_ref[:, c0:c0 + width], preferred_element_type=F32)
        for kind in kinds:
            if kind == "f32":
                val = acc
            elif kind == "bf16":
                val = acc.astype(BF16)
            elif kind == "bf16s":
                val = (acc * 0.125).astype(BF16)
            elif kind == "bf16e":
                val = (acc * (0.125 * LOG2E)).astype(BF16)
            elif kind == "hilo":
                hi = acc.astype(BF16)
                lo = (acc - hi.astype(F32)).astype(BF16)
                lane = lax.broadcasted_iota(I32, acc.shape, 1) & (LANE - 1)
                val = jnp.where(lane < HEAD_DIM, hi, lo)
            elif kind == "wi":
                val = acc * 0.0625
            else:
                raise ValueError(kind)
            out_refs[k][...] = val
            k += 1


def _project(x, w16, specs, tm):
    n = x.shape[0]
    out_shape, out_specs = [], []
    for _, width, kinds in specs:
        for kind in kinds:
            dt = F32 if kind in ("f32", "wi") else BF16
            out_shape.append(jax.ShapeDtypeStruct((n, width), dt))
            out_specs.append(pl.BlockSpec((tm, width), lambda i: (i, 0)))
    return pl.pallas_call(
        functools.partial(_proj_kernel, specs=specs),
        grid=(pl.cdiv(n, tm),),
        in_specs=[pl.BlockSpec((tm, x.shape[1]), lambda i: (i, 0)),
                  pl.BlockSpec(w16.shape, lambda i: (0, 0))],
        out_specs=out_specs,
        out_shape=out_shape,
        compiler_params=_cparams(("parallel",)),
        name="proj",
    )(x, w16)


def _mix_ln_kernel(x_ref, *refs, n_pairs):
    o_refs = refs[:n_pairs]
    w_refs = refs[n_pairs:2 * n_pairs]
    g_ref, b_ref, y_ref = refs[2 * n_pairs:]
    acc = DN_ALPHA * x_ref[...]
    for o_ref, w_ref in zip(o_refs, w_refs):
        acc = acc + jnp.dot(o_ref[...], w_ref[...], preferred_element_type=F32)
    y_ref[...] = _layer_norm(acc, g_ref[...], b_ref[...])


def _mix_ln(x, os_, ws, g, b, tm):
    n, d = x.shape
    in_specs = [pl.BlockSpec((tm, d), lambda i: (i, 0))]
    in_specs += [pl.BlockSpec((tm, o.shape[1]), lambda i: (i, 0)) for o in os_]
    in_specs += [pl.BlockSpec(w.shape, lambda i: (0, 0)) for w in ws]
    in_specs += [pl.BlockSpec((1, d), lambda i: (0, 0))] * 2
    return pl.pallas_call(
        functools.partial(_mix_ln_kernel, n_pairs=len(os_)),
        grid=(pl.cdiv(n, tm),),
        in_specs=in_specs,
        out_specs=pl.BlockSpec((tm, d), lambda i: (i, 0)),
        out_shape=jax.ShapeDtypeStruct((n, d), F32),
        compiler_params=_cparams(("parallel",)),
        name="mix_ln",
    )(x, *os_, *ws, g.reshape(1, d), b.reshape(1, d))


def _ffn_ln_kernel(y_ref, wg_ref, wu_ref, wd_ref, g_ref, b_ref, o_ref, *, nf):
    y = y_ref[...]
    yb = y.astype(BF16)
    acc = DN_ALPHA * y
    for f in range(nf):
        hg = jnp.dot(yb, wg_ref[f], preferred_element_type=F32)
        hu = jnp.dot(yb, wu_ref[f], preferred_element_type=F32)
        h = (hg * jax.nn.sigmoid(hg) * hu).astype(BF16)
        acc = acc + jnp.dot(h, wd_ref[f], preferred_element_type=F32)
    o_ref[...] = _layer_norm(acc, g_ref[...], b_ref[...])


def _ffn_ln(y, wg, wu, wd, g, b, tm, tf):
    n, d = y.shape
    dff = wg.shape[1]
    nf = dff // tf
    wg3 = wg.astype(BF16).reshape(d, nf, tf).transpose(1, 0, 2)
    wu3 = wu.astype(BF16).reshape(d, nf, tf).transpose(1, 0, 2)
    wd3 = wd.astype(BF16).reshape(nf, tf, d)
    full3 = lambda i: (0, 0, 0)
    return pl.pallas_call(
        functools.partial(_ffn_ln_kernel, nf=nf),
        grid=(pl.cdiv(n, tm),),
        in_specs=[pl.BlockSpec((tm, d), lambda i: (i, 0)),
                  pl.BlockSpec(wg3.shape, full3), pl.BlockSpec(wu3.shape, full3),
                  pl.BlockSpec(wd3.shape, full3),
                  pl.BlockSpec((1, d), lambda i: (0, 0)), pl.BlockSpec((1, d), lambda i: (0, 0))],
        out_specs=pl.BlockSpec((tm, d), lambda i: (i, 0)),
        out_shape=jax.ShapeDtypeStruct((n, d), F32),
        compiler_params=_cparams(("parallel",)),
        name="ffn_ln",
    )(y, wg3, wu3, wd3, g.reshape(1, d), b.reshape(1, d))


def _sort_key(x):
    bits = lax.bitcast_convert_type(x, I32)
    return bits ^ ((bits >> 31) & 0x7FFFFFFF)


def _dsa_kernel(rb_ref, qi_ref, wi_ref, qa_ref, ki_ref, ka_ref, va_ref, bn_ref, o_ref,
                keys_ref, cut_ref, s_ref, p_ref, madd_ref, wbc_ref, cmax_ref, alpha_ref,
                l_ref, acc_ref, seen_ref, *, tq, n_sel, e_base, e_step, q_base, q_step, coff, n_valid, padf):
    i = pl.program_id(1)
    ck = DSA_CK
    rbk = DSA_RB
    nt = ck // LANE
    e_end = e_base + e_step * i
    n_chunks = (e_end + ck - 1) // ck
    qpos = q_base + q_step * i + lax.broadcasted_iota(I32, (tq, 1), 0)
    bound = jnp.minimum(((((qpos + coff) >> CHUNK_SHIFT) + 1) << CHUNK_SHIFT) - coff, n_valid)
    lane_rb = lax.broadcasted_iota(I32, (rbk, ck), 1)
    lane128 = lax.broadcasted_iota(I32, (tq, LANE), 1)

    def key_rows(c):
        kpos0 = e_end - ck * (c + 1)
        return kpos0, pl.multiple_of(jnp.maximum(kpos0 + padf, 0), LANE)

    def tiles(x):
        return [x[:, t * LANE:(t + 1) * LANE] for t in range(nt)]

    qstack = jnp.concatenate([qi_ref[:, h * LANE:(h + 1) * LANE] for h in range(IDX_HEADS)], axis=0)
    for h in range(IDX_HEADS):
        wbc_ref[h] = jnp.broadcast_to(wi_ref[:, h:h + 1], (tq, LANE))
    cmax_ref[...] = jnp.full((tq, ck), -jnp.inf, F32)

    def idx_dot(c, dst):
        _, r = key_rows(c)
        s_ref[dst] = _nt(qstack, ki_ref[pl.ds(r, ck), :])

    def idx_keys(c, src):
        kpos0, _ = key_rows(c)
        for b in range(tq // rbk):
            r0 = b * rbk
            sc = None
            for h in range(IDX_HEADS):
                sh = jnp.maximum(s_ref[src, h * tq + r0:h * tq + r0 + rbk, :], 0.0)
                term = jnp.concatenate([wbc_ref[h, r0:r0 + rbk, :]] * nt, axis=1) * sh
                sc = term if sc is None else sc + term
            kpos = kpos0 + lane_rb
            allowed = (kpos >= 0) & (kpos < bound[r0:r0 + rbk])
            keys_ref[c, r0:r0 + rbk, :] = jnp.where(allowed, _sort_key(sc), INT_MIN)
            cmax_ref[r0:r0 + rbk, :] = jnp.maximum(cmax_ref[r0:r0 + rbk, :], jnp.where(allowed, sc, -jnp.inf))

    idx_dot(0, 0)

    def p1(j, carry):
        c = 2 * j
        idx_dot(c + 1, 1)
        idx_keys(c, 0)
        idx_dot(c + 2, 0)
        idx_keys(c + 1, 1)
        return carry

    lax.fori_loop(0, (n_chunks + 1) // 2, p1, 0)

    def count_ge(cand):
        def body(c, acc):
            u = keys_ref[c]
            for t in range(nt):
                acc = acc + jnp.where(u[:, t * LANE:(t + 1) * LANE] >= cand, 1.0, 0.0)
            return acc
        acc = lax.fori_loop(0, n_chunks, body, jnp.zeros((tq, LANE), F32))
        return jnp.sum(acc, axis=1, keepdims=True)

    cm = tiles(cmax_ref[...])
    if n_sel <= LANE:
        fold = [functools.reduce(jnp.maximum, cm)]
    elif n_sel <= 2 * LANE:
        fold = [jnp.maximum(cm[0], cm[2]), jnp.maximum(cm[1], cm[3])]
    else:
        fold = cm
    fmin = jnp.min(functools.reduce(jnp.minimum, fold), axis=1, keepdims=True)
    fmax = jnp.max(functools.reduce(jnp.maximum, cm), axis=1, keepdims=True)
    lo0 = _sort_key(fmin) - 1
    hi0 = _sort_key(fmax) + 2
    small = bound <= n_sel
    pos_side = count_ge(jnp.full((tq, 1), 1, I32)) >= n_sel
    at_zero = jnp.logical_not(pos_side) & (count_ge(jnp.zeros((tq, 1), I32)) >= n_sel)
    lo1 = jnp.where(pos_side, jnp.maximum(lo0, 1), lo0)
    top1 = jnp.where(pos_side, hi0, jnp.minimum(hi0, 0)) - 1
    fixed = small | at_zero
    nbits = jnp.where(fixed, 0, 32 - lax.clz(lo1 ^ top1))
    low_mask = (jnp.int32(1) << jnp.minimum(nbits, 31)) - 1
    t0 = jnp.where(small, INT_MIN + 1,
                   jnp.where(at_zero, 0, jnp.where(nbits >= 32, INT_MIN, lo1 & ~low_mask)))
    max_bits = jnp.max(nbits.astype(F32)).astype(I32)

    def bis(it, t):
        b = nbits - 1 - it
        cand = t + (jnp.int32(1) << jnp.maximum(b, 0))
        return jnp.where((b >= 0) & (count_ge(cand) >= n_sel), cand, t)

    thr = lax.fori_loop(0, max_bits, bis, t0)
    thr = jnp.maximum(thr, INT_MIN + 1)
    n_gt = count_ge(thr + 1)
    need = n_sel - n_gt
    excess = (count_ge(thr) - n_gt) > need

    cut_ref[...] = jnp.full((tq, LANE), 2 ** 30, I32)

    @pl.when(jnp.max(jnp.where(excess, 1.0, 0.0)) > 0.0)
    def _():
        rr = lax.broadcasted_iota(I32, (ck, ck), 0)
        cc = lax.broadcasted_iota(I32, (ck, ck), 1)
        upto = jnp.where(rr <= cc, 1.0, 0.0).astype(BF16)

        def tally(j, run):
            c = n_chunks - 1 - j
            seen_ref[c] = run
            u = keys_ref[c]
            for t in range(nt):
                run = run + jnp.where(u[:, t * LANE:(t + 1) * LANE] == thr, 1.0, 0.0)
            return run

        lax.fori_loop(0, n_chunks, tally, jnp.zeros((tq, LANE), F32))

        def locate(c, cut):
            kpos0, _ = key_rows(c)
            seen = jnp.sum(seen_ref[c], axis=1, keepdims=True)
            eq = jnp.where(keys_ref[c] == thr, 1.0, 0.0).astype(BF16)
            incl = jnp.dot(eq, upto, preferred_element_type=F32)
            below = jnp.sum(jnp.where(seen + incl < need, 1.0, 0.0), axis=1, keepdims=True)
            here = excess & (seen < need) & (seen + incl[:, ck - 1:ck] >= need)
            return jnp.where(here, kpos0 + below.astype(I32) + 1, cut)

        def locate4(j, cut):
            for k in range(4):
                cut = locate(jnp.minimum(4 * j + k, n_chunks - 1), cut)
            return cut

        cut = lax.fori_loop(0, (n_chunks + 3) // 4, locate4, jnp.full((tq, 1), 2 ** 30, I32))
        cut_ref[...] = jnp.broadcast_to(cut, (tq, LANE))

    cut = cut_ref[:, 0:1]

    r4 = A_HEADS // A_KV_HEADS
    qg = []
    for g in range(A_KV_HEADS):
        rows = []
        for j in range(r4):
            slot = qa_ref[:, j * LANE:(j + 1) * LANE]
            half = (lane128 < HEAD_DIM) if g == 0 else (lane128 >= HEAD_DIM)
            rows.append(jnp.where(half, slot, jnp.zeros_like(slot)))
        qg.append(jnp.concatenate(rows, axis=0))
    farb = [jnp.concatenate([jnp.full((tq, 1), rb_ref[g * r4 + j], F32) for j in range(r4)], axis=0)
            for g in range(A_KV_HEADS)]
    l_ref[...] = jnp.zeros(l_ref.shape, F32)
    acc_ref[...] = jnp.zeros(acc_ref.shape, F32)

    def logits_dot(c, g):
        _, r = key_rows(c)
        s_ref[g] = _nt(qg[g], ka_ref[pl.ds(r, ck), :])

    def select_mask(c):
        kpos0, _ = key_rows(c)
        for b in range(tq // rbk):
            r0 = b * rbk
            u = keys_ref[c, r0:r0 + rbk, :]
            t_b = thr[r0:r0 + rbk]
            sel = (u > t_b) | ((u == t_b) & (kpos0 + lane_rb < cut[r0:r0 + rbk]))
            madd_ref[r0:r0 + rbk, :] = jnp.where(sel, 0.0, NEG)

    def softmax_passes(g, m_old, near):
        m_rows = []
        for b in range(r4 * tq // rbk):
            r0 = b * rbk
            q0 = r0 % tq
            sm = s_ref[g, r0:r0 + rbk, :] + madd_ref[q0:q0 + rbk, :]
            if near:
                sm = sm + bn_ref[g, r0:r0 + rbk, :]
            m_blk = jnp.max(functools.reduce(jnp.maximum, tiles(sm)), axis=1, keepdims=True)
            far_b = farb[g][r0:r0 + rbk]
            if not near:
                m_blk = m_blk + far_b
            m_prev = m_old[r0:r0 + rbk]
            m_new = jnp.maximum(m_prev, m_blk)
            alpha = jnp.exp2(m_prev - m_new)
            alpha_ref[g, r0:r0 + rbk, :] = jnp.broadcast_to(alpha, (rbk, LANE))
            p = jnp.exp2(sm - (m_new if near else m_new - far_b))
            l_ref[g, r0:r0 + rbk, :] = alpha * l_ref[g, r0:r0 + rbk, :] + functools.reduce(jnp.add, tiles(p))
            p_ref[g, r0:r0 + rbk, :] = p.astype(BF16)
            m_rows.append(m_new)
        return jnp.concatenate(m_rows, axis=0)

    def value_dot(c, g):
        _, r = key_rows(c)
        acc_ref[g] = alpha_ref[g] * acc_ref[g] + jnp.dot(p_ref[g], va_ref[pl.ds(r, ck), :],
                                                        preferred_element_type=F32)

    m0 = jnp.full((r4 * tq, 1), NEG, F32)
    logits_dot(0, 0)
    logits_dot(0, 1)
    select_mask(0)
    m0n = softmax_passes(0, m0, True)
    value_dot(0, 0)
    logits_dot(1, 0)
    m1n = softmax_passes(1, m0, True)

    def p3(c, ms):
        value_dot(c - 1, 1)
        logits_dot(c, 1)
        select_mask(c)
        m_a = softmax_passes(0, ms[0], False)
        value_dot(c, 0)
        logits_dot(c + 1, 0)
        m_b = softmax_passes(1, ms[1], False)
        return (m_a, m_b)

    lax.fori_loop(1, n_chunks, p3, (m0n, m1n))
    value_dot(n_chunks - 1, 1)
    outs = [acc_ref[g] / jnp.sum(l_ref[g], axis=1, keepdims=True) for g in range(A_KV_HEADS)]
    for j in range(r4):
        lo = outs[0][j * tq:(j + 1) * tq]
        hi = outs[1][j * tq:(j + 1) * tq]
        o_ref[:, j * LANE:(j + 1) * LANE] = jnp.where(lane128 < HEAD_DIM, lo, hi).astype(BF16)


def _dsa(rb_far, qi, wi, qa, ki, ka, va, bn, *, nb, nq, tq, row0, n_sel, e_base, e_step,
         q_base, q_step, coff, n_valid, padf, n_chunks_max):
    rb0 = row0 // tq
    r4 = A_HEADS // A_KV_HEADS
    assert IDX_HEADS == r4 and tq % DSA_RB == 0
    qmap = lambda b, i: (rb0 + b * nq + i, 0)
    kmap = lambda b, i: (b, 0, 0)
    kern = functools.partial(_dsa_kernel, tq=tq, n_sel=n_sel, e_base=e_base, e_step=e_step,
                             q_base=q_base, q_step=q_step, coff=coff, n_valid=n_valid, padf=padf)
    return pl.pallas_call(
        kern,
        grid=(nb, nq),
        in_specs=[pl.BlockSpec(memory_space=pltpu.SMEM),
                  pl.BlockSpec((tq, qi.shape[1]), qmap),
                  pl.BlockSpec((tq, LANE), qmap),
                  pl.BlockSpec((tq, qa.shape[1]), qmap),
                  pl.BlockSpec((None,) + ki.shape[1:], kmap),
                  pl.BlockSpec((None,) + ka.shape[1:], kmap),
                  pl.BlockSpec((None,) + va.shape[1:], kmap),
                  pl.BlockSpec(bn.shape, lambda b, i: (0, 0, 0))],
        out_specs=pl.BlockSpec((tq, qa.shape[1]), lambda b, i: (b * nq + i, 0)),
        out_shape=jax.ShapeDtypeStruct((nb * nq * tq, qa.shape[1]), BF16),
        scratch_shapes=[pltpu.VMEM((n_chunks_max + 1, tq, DSA_CK), I32),
                        pltpu.VMEM((tq, LANE), I32),
                        pltpu.VMEM((2, r4 * tq, DSA_CK), F32),
                        pltpu.VMEM((A_KV_HEADS, r4 * tq, DSA_CK), BF16),
                        pltpu.VMEM((tq, DSA_CK), F32),
                        pltpu.VMEM((IDX_HEADS, tq, LANE), F32),
                        pltpu.VMEM((tq, DSA_CK), F32),
                        pltpu.VMEM((A_KV_HEADS, r4 * tq, LANE), F32),
                        pltpu.VMEM((A_KV_HEADS, r4 * tq, LANE), F32),
                        pltpu.VMEM((A_KV_HEADS, r4 * tq, LANE), F32),
                        pltpu.VMEM((n_chunks_max, tq, LANE), F32)],
        compiler_params=_cparams(("parallel", "arbitrary")),
        name="dsa",
    )(rb_far, qi, wi, qa, ki, ka, va, bn)


def _sb_kernel(tri_ref, q_ref, k_ref, v_ref, o_ref, *, tq, e_base, e_step, q_base, q_step, padf, n_valid):
    i = pl.program_id(2)
    bw = SB_BAND
    npp = q_ref.shape[1] // LANE
    nh = 2 * npp
    e_end = e_base + e_step * i
    n_bands = (e_end + bw - 1) // bw
    qpos = q_base + q_step * i + lax.broadcasted_iota(I32, (tq, 1), 0)
    lane = lax.broadcasted_iota(I32, (tq, LANE), 1)
    qh = []
    for p in range(npp):
        q = q_ref[:, p * LANE:(p + 1) * LANE]
        zero = jnp.zeros_like(q)
        qh += [jnp.where(lane < HEAD_DIM, q, zero), jnp.where(lane >= HEAD_DIM, q, zero)]
    lane_bw = lax.broadcasted_iota(I32, (tq, bw), 1)

    def cond(st):
        return (st[0] < n_bands) & (st[1] > SB_STOP)

    def body(st):
        m, _, carry, acc = st
        kpos0 = e_end - bw * (m + 1)
        r = pl.multiple_of(kpos0 + padf, LANE)
        kpos = kpos0 + lane_bw
        before = (kpos < qpos) & (kpos >= 0) & (kpos < n_valid)
        zs, lss, lks, parts = [], [], [], []
        for h in range(nh):
            kt = k_ref[pl.ds(r, bw), (h // 2) * LANE:(h // 2 + 1) * LANE]
            z = _nt(qh[h], kt)
            ls = -(jnp.maximum(z, 0.0) + jnp.log1p(jnp.exp(-jnp.abs(z))))
            lk = jnp.where(before, ls, 0.0)
            hi = lk.astype(BF16)
            parts += [hi, (lk - hi.astype(F32)).astype(BF16)]
            zs.append(z)
            lss.append(ls)
            lks.append(lk)
        sums = jnp.dot(jnp.concatenate(parts, axis=0), tri_ref[...], preferred_element_type=F32)
        new_c, new_a = [], []
        worst = jnp.float32(-jnp.inf)
        for h in range(nh):
            vt = v_ref[pl.ds(r, bw), (h // 2) * LANE:(h // 2 + 1) * LANE]
            bl = sums[2 * h * tq:(2 * h + 1) * tq] + sums[(2 * h + 1) * tq:(2 * h + 2) * tq]
            w = jnp.where(before, jnp.exp(lss[h] + zs[h] + bl + carry[h]), 0.0)
            new_a.append(acc[h] + jnp.dot(w.astype(BF16), vt, preferred_element_type=F32))
            c_n = carry[h] + bl[:, 0:1] + lks[h][:, 0:1]
            new_c.append(c_n)
            worst = jnp.maximum(worst, jnp.max(c_n))
        return (m + 1, worst, tuple(new_c), tuple(new_a))

    init = (jnp.int32(0), jnp.float32(0.0),
            tuple(jnp.zeros((tq, 1), F32) for _ in range(nh)),
            tuple(jnp.zeros((tq, LANE), F32) for _ in range(nh)))
    _, _, _, acc = lax.while_loop(cond, body, init)
    for p in range(npp):
        o_ref[:, p * LANE:(p + 1) * LANE] = jnp.where(lane < HEAD_DIM, acc[2 * p], acc[2 * p + 1]).astype(BF16)


def _stick_break(q, k, v, *, nb, nq, tq, row0, e_base, e_step, q_base, q_step, padf, n_valid):
    rb0 = row0 // tq
    wq = SB_PAIRS * LANE
    npair = q.shape[1] // wq
    kern = functools.partial(_sb_kernel, tq=tq, e_base=e_base, e_step=e_step, q_base=q_base,
                             q_step=q_step, padf=padf, n_valid=n_valid)
    rr = lax.broadcasted_iota(I32, (SB_BAND, SB_BAND), 0)
    cc = lax.broadcasted_iota(I32, (SB_BAND, SB_BAND), 1)
    tri = jnp.where(rr > cc, 1.0, 0.0).astype(BF16)
    kspec = pl.BlockSpec((None, k.shape[1], wq), lambda b, p, i: (b, 0, p))
    return pl.pallas_call(
        kern,
        grid=(nb, npair, nq),
        in_specs=[pl.BlockSpec(tri.shape, lambda b, p, i: (0, 0)),
                  pl.BlockSpec((tq, wq), lambda b, p, i: (rb0 + b * nq + i, p)), kspec, kspec],
        out_specs=pl.BlockSpec((tq, wq), lambda b, p, i: (b * nq + i, p)),
        out_shape=jax.ShapeDtypeStruct((nb * nq * tq, q.shape[1]), BF16),
        compiler_params=_cparams(("parallel", "parallel", "arbitrary")),
        name="stick_break",
    )(tri, q, k, v)


def _swa_kernel(sink_ref, q_ref, k_ref, v_ref, bias_ref, o_ref, s_ref, p_ref, madd_ref,
                *, tq, ck, r_base, r_step, p_base, p_step, q_base, q_step, coff, n_valid):
    i = pl.program_id(1)
    r0 = pl.multiple_of(r_base + r_step * i, 16)
    kpos0 = p_base + p_step * i
    kt = k_ref[pl.ds(r0, ck), :]
    vt = v_ref[pl.ds(r0, ck), :]
    qpos = q_base + q_step * i + lax.broadcasted_iota(I32, (tq, 1), 0)
    kpos = kpos0 + lax.broadcasted_iota(I32, (tq, ck), 1)
    qc = (qpos + coff) >> CHUNK_SHIFT
    hi_b = jnp.minimum(((qc + 1) << CHUNK_SHIFT) - coff, n_valid)
    lo_b = jnp.maximum(((qc - WIN_CHUNKS) << CHUNK_SHIFT) - coff, 0)
    r8 = C_HEADS // C_KV_HEADS
    rbk = DSA_RB
    nt = ck // LANE
    madd_ref[...] = jnp.where((kpos >= lo_b) & (kpos < hi_b), 0.0, NEG)
    lane = lax.broadcasted_iota(I32, (tq, LANE), 1)

    def tiles(x):
        return [x[:, t * LANE:(t + 1) * LANE] for t in range(nt)]

    outs = []
    for g in range(C_KV_HEADS):
        rows = []
        for j in range(r8):
            slot = q_ref[:, j * LANE:(j + 1) * LANE]
            half = (lane < HEAD_DIM) if g == 0 else (lane >= HEAD_DIM)
            rows.append(jnp.where(half, slot, jnp.zeros_like(slot)))
        s_ref[g] = _nt(jnp.concatenate(rows, axis=0), kt)
        dens = []
        for b in range(r8 * tq // rbk):
            rs = slice(b * rbk, (b + 1) * rbk)
            q0 = (b * rbk) % tq
            sink = jnp.full((rbk, 1), sink_ref[g * r8 + (b * rbk) // tq], F32)
            sm = s_ref[g, rs, :] + bias_ref[g, rs, :] + madd_ref[q0:q0 + rbk, :]
            m = jnp.maximum(jnp.max(functools.reduce(jnp.maximum, tiles(sm)), axis=1, keepdims=True), sink)
            e = jnp.exp2(sm - m)
            dens.append(jnp.sum(functools.reduce(jnp.add, tiles(e)), axis=1, keepdims=True)
                        + jnp.exp2(sink - m))
            p_ref[g, rs, :] = e.astype(BF16)
        outs.append(jnp.dot(p_ref[g], vt, preferred_element_type=F32) / jnp.concatenate(dens, axis=0))
    for j in range(r8):
        lo = outs[0][j * tq:(j + 1) * tq]
        hi = outs[1][j * tq:(j + 1) * tq]
        o_ref[:, j * LANE:(j + 1) * LANE] = jnp.where(lane < HEAD_DIM, lo, hi).astype(BF16)


def _swa(sinks, q, k, v, bias, *, nb, nq, tq, row0, ck, r_base, r_step, p_base, p_step,
         q_base, q_step, coff, n_valid):
    rb0 = row0 // tq
    r8 = C_HEADS // C_KV_HEADS
    assert tq % DSA_RB == 0 and ck % LANE == 0
    kern = functools.partial(_swa_kernel, tq=tq, ck=ck, r_base=r_base, r_step=r_step, p_base=p_base,
                             p_step=p_step, q_base=q_base, q_step=q_step, coff=coff, n_valid=n_valid)
    kmap = lambda b, i: (b, 0, 0)
    return pl.pallas_call(
        kern,
        grid=(nb, nq),
        in_specs=[pl.BlockSpec(memory_space=pltpu.SMEM),
                  pl.BlockSpec((tq, q.shape[1]), lambda b, i: (rb0 + b * nq + i, 0)),
                  pl.BlockSpec((None,) + k.shape[1:], kmap),
                  pl.BlockSpec((None,) + v.shape[1:], kmap),
                  pl.BlockSpec(bias.shape, lambda b, i: (0, 0, 0))],
        out_specs=pl.BlockSpec((tq, q.shape[1]), lambda b, i: (b * nq + i, 0)),
        out_shape=jax.ShapeDtypeStruct((nb * nq * tq, q.shape[1]), BF16),
        scratch_shapes=[pltpu.VMEM((C_KV_HEADS, r8 * tq, ck), F32),
                        pltpu.VMEM((C_KV_HEADS, r8 * tq, ck), BF16),
                        pltpu.VMEM((tq, ck), F32)],
        compiler_params=_cparams(("parallel", "arbitrary")),
        name="swa",
    )(sinks, q, k, v, bias)


def _router_kernel(y_ref, rh_ref, rl_ref, idx_ref, gate_ref):
    y = y_ref[...]
    yh = y.astype(BF16)
    yl = (y - yh.astype(F32)).astype(BF16)
    rh = rh_ref[...]
    logits = (jnp.dot(yh, rh, preferred_element_type=F32) + jnp.dot(yl, rh, preferred_element_type=F32)
              + jnp.dot(yh, rl_ref[...], preferred_element_type=F32))
    lane = lax.broadcasted_iota(I32, logits.shape, 1)
    logits = jnp.where(lane < N_EXPERTS, logits, -jnp.inf)
    m1 = jnp.max(logits, axis=1, keepdims=True)
    i1 = jnp.min(jnp.where(logits == m1, lane, LANE), axis=1, keepdims=True)
    rest = jnp.where(lane == i1, -jnp.inf, logits)
    m2 = jnp.max(rest, axis=1, keepdims=True)
    i2 = jnp.min(jnp.where(rest == m2, lane, LANE), axis=1, keepdims=True)
    e2 = jnp.exp(m2 - m1)
    den = 1.0 + e2
    idx_ref[...] = jnp.where(lane == 0, i1, jnp.where(lane == 1, i2, 0))
    gate_ref[...] = jnp.where(lane == 0, 1.0 / den, jnp.where(lane == 1, e2 / den, 0.0))


def _router(y, router, tm):
    n, d = y.shape
    rpad = jnp.pad(router.astype(F32), ((0, 0), (0, LANE - router.shape[1])))
    rh = rpad.astype(BF16)
    rl = (rpad - rh.astype(F32)).astype(BF16)
    return pl.pallas_call(
        _router_kernel,
        grid=(pl.cdiv(n, tm),),
        in_specs=[pl.BlockSpec((tm, d), lambda i: (i, 0)),
                  pl.BlockSpec((d, LANE), lambda i: (0, 0)), pl.BlockSpec((d, LANE), lambda i: (0, 0))],
        out_specs=[pl.BlockSpec((tm, LANE), lambda i: (i, 0)), pl.BlockSpec((tm, LANE), lambda i: (i, 0))],
        out_shape=[jax.ShapeDtypeStruct((n, LANE), I32), jax.ShapeDtypeStruct((n, LANE), F32)],
        compiler_params=_cparams(("parallel",)),
        name="router",
    )(y, rh, rl)


def _moe_kernel(be_ref, nu_ref, x_ref, wg_ref, wu_ref, wd_ref, o_ref, acc_ref, xb_ref):
    t = pl.program_id(0)
    f = pl.program_id(1)

    @pl.when(f == 0)
    def _():
        acc_ref[...] = jnp.zeros_like(acc_ref)
        xb_ref[...] = x_ref[...].astype(BF16)

    @pl.when(t < nu_ref[0])
    def _():
        xb = xb_ref[...]
        hg = jnp.dot(xb, wg_ref[...], preferred_element_type=F32)
        hu = jnp.dot(xb, wu_ref[...], preferred_element_type=F32)
        h = (hg * jax.nn.sigmoid(hg) * hu).astype(BF16)
        acc_ref[...] += jnp.dot(h, wd_ref[...], preferred_element_type=F32)

    @pl.when(f == pl.num_programs(1) - 1)
    def _():
        o_ref[...] = acc_ref[...]


def _moe_experts(block_expert, n_used, xs, wg, wu, wd, tmb, tf):
    n_slots, d = xs.shape
    de = wg.shape[2]
    grid_spec = pltpu.PrefetchScalarGridSpec(
        num_scalar_prefetch=2,
        grid=(n_slots // tmb, de // tf),
        in_specs=[pl.BlockSpec((tmb, d), lambda t, f, be, nu: (t, 0)),
                  pl.BlockSpec((None, d, tf), lambda t, f, be, nu: (be[t], 0, f)),
                  pl.BlockSpec((None, d, tf), lambda t, f, be, nu: (be[t], 0, f)),
                  pl.BlockSpec((None, tf, d), lambda t, f, be, nu: (be[t], f, 0))],
        out_specs=pl.BlockSpec((tmb, d), lambda t, f, be, nu: (t, 0)),
        scratch_shapes=[pltpu.VMEM((tmb, d), F32), pltpu.VMEM((tmb, d), BF16)],
    )
    return pl.pallas_call(
        _moe_kernel,
        grid_spec=grid_spec,
        out_shape=jax.ShapeDtypeStruct((n_slots, d), F32),
        compiler_params=_cparams(("parallel", "arbitrary")),
        name="moe_experts",
    )(block_expert, n_used, xs, wg, wu, wd)


def _row_copy(src_hbm, src_row, dst, dst_row, sem):
    return pltpu.make_async_copy(src_hbm.at[pl.ds(src_row, 1)], dst.at[pl.ds(dst_row, 1)], sem)


def _dispatch_kernel(slot_ref, x_ref, init_hbm, xs_hbm, sem, *, tt):
    del init_hbm

    def start(j, carry):
        for k in range(TOP_K):
            _row_copy(x_ref, j, xs_hbm, slot_ref[0, 0, TOP_K * j + k], sem).start(priority=k % 2)
        return carry

    def wait(j, carry):
        _row_copy(x_ref, 0, xs_hbm, 0, sem).wait()
        return carry

    lax.fori_loop(0, tt, start, 0, unroll=8)
    lax.fori_loop(0, TOP_K * tt, wait, 0, unroll=8)


def _dispatch(x, slot3, n_slots, tt):
    n, d = x.shape
    init = jnp.zeros((n_slots, d), x.dtype)
    return pl.pallas_call(
        functools.partial(_dispatch_kernel, tt=tt),
        grid=(n // tt,),
        in_specs=[pl.BlockSpec((1, 1, TOP_K * tt), lambda i: (i, 0, 0), memory_space=pltpu.SMEM),
                  pl.BlockSpec((tt, d), lambda i: (i, 0)), pl.BlockSpec(memory_space=pl.ANY)],
        out_specs=pl.BlockSpec(memory_space=pl.ANY),
        out_shape=jax.ShapeDtypeStruct((n_slots, d), x.dtype),
        scratch_shapes=[pltpu.SemaphoreType.DMA],
        input_output_aliases={2: 0},
        compiler_params=_cparams(("arbitrary",)),
        name="moe_dispatch",
    )(slot3, x, init)


def _combine_ln_kernel(slot_ref, y_ref, gate_ref, g_ref, b_ref, outs_hbm, o_ref, buf_ref, sem, *, tt):
    def start(j, carry):
        for k in range(TOP_K):
            _row_copy(outs_hbm, slot_ref[0, 0, TOP_K * j + k], buf_ref.at[k], j, sem).start(priority=k % 2)
        return carry

    def wait(j, carry):
        _row_copy(outs_hbm, 0, buf_ref.at[0], 0, sem).wait()
        return carry

    lax.fori_loop(0, tt, start, 0, unroll=8)
    lax.fori_loop(0, TOP_K * tt, wait, 0, unroll=8)
    mo = gate_ref[:, 0:1] * buf_ref[0]
    for k in range(1, TOP_K):
        mo = mo + gate_ref[:, k:k + 1] * buf_ref[k]
    o_ref[...] = _layer_norm(DN_ALPHA * y_ref[...] + mo, g_ref[...], b_ref[...])


def _combine_ln(y, outs, slot3, gate_p, g, b, tt):
    n, d = y.shape
    row = pl.BlockSpec((tt, d), lambda i: (i, 0))
    vec = pl.BlockSpec((1, d), lambda i: (0, 0))
    return pl.pallas_call(
        functools.partial(_combine_ln_kernel, tt=tt),
        grid=(n // tt,),
        in_specs=[pl.BlockSpec((1, 1, TOP_K * tt), lambda i: (i, 0, 0), memory_space=pltpu.SMEM),
                  row, pl.BlockSpec((tt, LANE), lambda i: (i, 0)), vec, vec,
                  pl.BlockSpec(memory_space=pl.ANY)],
        out_specs=row,
        out_shape=jax.ShapeDtypeStruct((n, d), F32),
        scratch_shapes=[pltpu.VMEM((TOP_K, tt, d), F32), pltpu.SemaphoreType.DMA],
        compiler_params=_cparams(("arbitrary",)),
        name="moe_combine_ln",
    )(slot3, y, gate_p, g.reshape(1, d), b.reshape(1, d), outs)


def _moe(y, router, w_gate, w_up, w_down, g, b, tm, tmb, tf, tt):
    n, d = y.shape
    assert n % tt == 0
    idx_p, gate_p = _router(y, router, tm)
    e_flat = idx_p[:, :TOP_K].reshape(-1)
    onehot = (e_flat[:, None] == jnp.arange(N_EXPERTS, dtype=I32)[None, :]).astype(I32)
    rank = jnp.sum((jnp.cumsum(onehot, axis=0) - onehot) * onehot, axis=1)
    counts = jnp.sum(onehot, axis=0)
    padded = (counts + tmb - 1) // tmb * tmb
    pad_end = jnp.cumsum(padded)
    slot = ((pad_end - padded)[e_flat] + rank).astype(I32)
    n_blocks = -(-(TOP_K * n) // tmb) + N_EXPERTS
    n_slots = n_blocks * tmb
    block_expert = jnp.minimum(
        jnp.searchsorted(pad_end, jnp.arange(n_blocks, dtype=I32) * tmb, side="right"),
        N_EXPERTS - 1).astype(I32)
    n_used = (pad_end[-1] // tmb).astype(I32).reshape(1)
    slot3 = slot.reshape(n // tt, 1, TOP_K * tt)
    xs = _dispatch(y, slot3, n_slots, tt)
    outs = _moe_experts(block_expert, n_used, xs, w_gate.astype(BF16), w_up.astype(BF16),
                        w_down.astype(BF16), tmb, tf)
    return _combine_ln(y, outs, slot3, gate_p, g, b, tt)


def _t5_bucket(rel):
    half = NUM_BUCKETS // 2
    exact = half // 2
    n = jnp.abs(rel)
    far = exact + sum((n >= t).astype(I32) for t in (12, 16, 23, 32, 46, 64, 91))
    return jnp.where(rel > 0, half, 0) + jnp.where(n < exact, n, far)


def _bias_tile(rel_bias, n_groups, per_group, tq, ck, d0):
    d = d0 + jnp.arange(ck, dtype=I32)[None, :] - jnp.arange(tq, dtype=I32)[:, None]
    onehot = (_t5_bucket(d)[..., None] == jnp.arange(NUM_BUCKETS, dtype=I32)).astype(F32)
    tile = jnp.einsum("qkb,bh->qkh", onehot, rel_bias.astype(F32),
                      precision=lax.Precision.HIGHEST)
    tile = jnp.transpose(tile[:, :, :n_groups * per_group], (2, 0, 1))
    return tile.reshape(n_groups, per_group * tq, ck)


def _cols(w, ranges):
    parts = []
    for r in ranges:
        if isinstance(r, int):
            parts.append(jnp.zeros((w.shape[0], r), w.dtype))
        else:
            parts.append(w[:, r[0]:r[1]])
    return jnp.concatenate(parts, axis=1)


def _head_pair_order(n_heads):
    half = n_heads // 2
    order = []
    for j in range(half):
        order += [j, half + j]
    return order


L0_SPECS = ((0, 512, ("bf16e",)), (512, 128, ("f32", "bf16")), (640, 128, ("f32", "bf16")),
            (768, 512, ("hilo",)), (1280, 128, ("f32", "bf16")), (1408, 128, ("wi",)),
            (1536, 512, ("bf16s",)), (2048, 512, ("f32", "bf16")), (2560, 512, ("f32", "bf16")))
L1_SPECS = ((0, 1024, ("bf16e",)), (1024, 128, ("f32", "bf16")), (1152, 128, ("f32", "bf16")))


def _l0_weight(w_in):
    hd = HEAD_DIM
    rng = [(hd * h, hd * h + hd) for h in _head_pair_order(A_HEADS)]
    rng += [(512, 640), (640, 768)]
    for h in range(IDX_HEADS):
        rng += [(768 + hd * h, 768 + hd * h + hd)] * 2
    rng += [(1024, 1088)] * 2
    rng += [(1088, 1092), LANE - IDX_HEADS]
    rng += [(1092, 1604), (1604, 2116), (2116, 2628)]
    return _cols(w_in, rng).astype(BF16)


def _l1_weight(w_in):
    hd = HEAD_DIM
    rng = [(hd * h, hd * h + hd) for h in _head_pair_order(C_HEADS)]
    rng += [(1024, 1152), (1152, 1280)]
    return _cols(w_in, rng).astype(BF16)


def _perm_rows(w, n_heads):
    return jnp.concatenate([w[HEAD_DIM * h:HEAD_DIM * (h + 1)] for h in _head_pair_order(n_heads)], axis=0)


def _front_pad(a, nb, rows_in, front, rows_out):
    a = a.reshape(nb, rows_in, a.shape[-1])
    return jnp.pad(a, ((0, 0), (front, rows_out - front - rows_in), (0, 0)))


def _layer0_attention(x_prompt, x_sample, cache_a_k, cache_a_v, cache_a_idx_k, cache_b_k, cache_b_v,
                      meta_tokens, rel_bias, l0_w_in):
    nb, seq, d = x_prompt.shape
    t = N_META + seq
    nq = -(-t // QBLK)
    tp = nq * QBLK
    db, ds, _ = x_sample.shape
    past = cache_a_k.shape[1]
    assert d == D_MODEL and past % LANE == 0 and ds % DSA_RB == 0 and ds <= 64
    np_, ns = nb * tp, db * ds
    tm = 512

    meta = jnp.broadcast_to(meta_tokens[None].astype(x_prompt.dtype), (nb, N_META, d))
    hp = jnp.concatenate([meta, x_prompt, jnp.zeros((nb, tp - t, d), x_prompt.dtype)], axis=1)
    x_all = jnp.concatenate([hp.reshape(np_, d), x_sample.reshape(ns, d)], axis=0)
    rel_bias = rel_bias.astype(F32)
    rb_far = rel_bias[NUM_BUCKETS // 2 - 1] * LOG2E

    (qa16, ka32, ka16, va32, va16, qi16, ki32, ki16, wi32, qb16, kb32, kb16, vb32, vb16) = _project(
        x_all, _l0_weight(l0_w_in), L0_SPECS, tm)

    n_sel_p = min(TOPK_MAX, (t - N_META) // 4)
    kr = QBLK * (nq + 1) + DSA_PADF
    ncm = -(-(QBLK * (nq + 1)) // DSA_CK)
    pk = [_front_pad(a[:np_], nb, tp, DSA_PADF, kr) for a in (ki16, ka16, va16)]
    bn_p = _bias_tile(rel_bias, A_KV_HEADS, A_HEADS // A_KV_HEADS, QBLK, DSA_CK, -(DSA_CK // 2)) * LOG2E
    oa_p = _dsa(rb_far, qi16, wi32, qa16, *pk, bn_p, nb=nb, nq=nq, tq=QBLK, row0=0, n_sel=n_sel_p,
                e_base=2 * QBLK, e_step=QBLK, q_base=0, q_step=QBLK, coff=PROMPT_COFF,
                n_valid=t, padf=DSA_PADF, n_chunks_max=ncm)

    n_keys = past + ds
    n_sel_s = min(TOPK_MAX, n_keys // 4)
    e_s = -(-n_keys // LANE) * LANE
    ncs = -(-e_s // DSA_CK)
    krs = ncs * DSA_CK
    fs = krs - e_s

    def cat_keys(cache, new, width_dup):
        c = cache.reshape(db, past, -1).astype(BF16)
        if width_dup:
            c = jnp.concatenate([c, c], axis=-1)
        a = jnp.concatenate([c, new[np_:].reshape(db, ds, -1)], axis=1)
        return jnp.pad(a, ((0, 0), (fs, krs - fs - n_keys), (0, 0)))

    sk = [cat_keys(cache_a_idx_k, ki16, True), cat_keys(cache_a_k, ka16, False),
          cat_keys(cache_a_v, va16, False)]
    bn_s = _bias_tile(rel_bias, A_KV_HEADS, A_HEADS // A_KV_HEADS, ds, DSA_CK, e_s - DSA_CK - past) * LOG2E
    oa_s = _dsa(rb_far, qi16, wi32, qa16, *sk, bn_s, nb=db, nq=1, tq=ds, row0=np_, n_sel=n_sel_s,
                e_base=e_s, e_step=0, q_base=past, q_step=0, coff=0, n_valid=n_keys, padf=fs,
                n_chunks_max=ncs)

    pfb = SB_BAND - QBLK
    kb_p = _front_pad(kb16[:np_], nb, tp, pfb, tp + pfb)
    vb_p = _front_pad(vb16[:np_], nb, tp, pfb, tp + pfb)
    ob_p = _stick_break(qb16, kb_p, vb_p, nb=nb, nq=nq, tq=QBLK, row0=0, e_base=QBLK, e_step=QBLK,
                        q_base=0, q_step=QBLK, padf=pfb, n_valid=t)
    rows_s = -(-e_s // SB_BAND) * SB_BAND
    pfs = rows_s - e_s

    def cat_b(cache, new):
        a = jnp.concatenate([cache.reshape(db, past, -1).astype(BF16), new[np_:].reshape(db, ds, -1)], axis=1)
        return jnp.pad(a, ((0, 0), (pfs, rows_s - pfs - n_keys), (0, 0)))

    ob_s = _stick_break(qb16, cat_b(cache_b_k, kb16), cat_b(cache_b_v, vb16), nb=db, nq=1, tq=ds,
                        row0=np_, e_base=e_s, e_step=0, q_base=past, q_step=0, padf=pfs, n_valid=n_keys)

    oa = jnp.concatenate([oa_p, oa_s], axis=0)
    ob = jnp.concatenate([ob_p, ob_s], axis=0)
    return dict(x_all=x_all, oa=oa, ob=ob, np=np_, tp=tp, t=t, nq=nq, rel_bias=rel_bias,
                ka32=ka32, va32=va32, ki32=ki32, kb32=kb32, vb32=vb32)


def kernel(x_prompt, x_sample, cache_a_k, cache_a_v, cache_a_idx_k, cache_b_k, cache_b_v, cache_c_k, cache_c_v, meta_tokens, rel_bias, l0_w_in, l0_w_out, l0_ln1_g, l0_ln1_b, l0_w_gate, l0_w_up, l0_w_down, l0_ln2_g, l0_ln2_b, l1_w_in, l1_sinks, l1_w_out, l1_ln1_g, l1_ln1_b, l1_router, l1_w_gate, l1_w_up, l1_w_down, l1_ln2_g, l1_ln2_b):
    a0 = _layer0_attention(x_prompt, x_sample, cache_a_k, cache_a_v, cache_a_idx_k, cache_b_k, cache_b_v,
                           meta_tokens, rel_bias, l0_w_in)
    x_all, oa, ob, np_, tp, t, nq, rel_bias = (a0[k] for k in ("x_all", "oa", "ob", "np", "tp", "t", "nq", "rel_bias"))
    ka32, va32, ki32, kb32, vb32 = (a0[k] for k in ("ka32", "va32", "ki32", "kb32", "vb32"))
    nb, _, d = x_prompt.shape
    db, ds, _ = x_sample.shape
    past = cache_a_k.shape[1]
    tm = 512
    w_out0 = l0_w_out.astype(BF16)
    half0 = A_HEADS * HEAD_DIM
    y0 = _mix_ln(x_all, [oa, ob], [_perm_rows(w_out0[:half0], A_HEADS), w_out0[half0:]],
                 l0_ln1_g, l0_ln1_b, tm)
    h1 = _ffn_ln(y0, l0_w_gate, l0_w_up, l0_w_down, l0_ln2_g, l0_ln2_b, tm, 256)

    oc, kc32, vc32 = _layer1_attention(h1, cache_c_k, cache_c_v, rel_bias, l1_w_in, l1_sinks,
                                       nb=nb, nq=nq, t=t, db=db, ds=ds, past=past, tm=tm)
    y1 = _mix_ln(h1, [oc], [_perm_rows(l1_w_out.astype(BF16), C_HEADS)], l1_ln1_g, l1_ln1_b, tm)
    n_all = y1.shape[0]
    tt = next(c for c in (256, 128, 64, 32, 16, 8) if n_all % c == 0)
    h2 = _moe(y1, l1_router, l1_w_gate, l1_w_up, l1_w_down, l1_ln2_g, l1_ln2_b, tm, 1024, 512, tt)
    return _assemble(h2, ka32, va32, ki32, kb32, vb32, kc32, vc32, cache_c_k, cache_c_v,
                     nb=nb, tp=tp, t=t, db=db, ds=ds)


def _layer1_attention(h1, cache_c_k, cache_c_v, rel_bias, l1_w_in, l1_sinks, *, nb, nq, t, db, ds, past, tm):
    tp = nq * QBLK
    np_ = nb * tp
    qc16, kc32, kc16, vc32, vc16 = _project(h1, _l1_weight(l1_w_in), L1_SPECS, tm)
    ckp = 4 * QBLK
    krc = QBLK * (nq - 1) + ckp
    kc_p = _front_pad(kc16[:np_], nb, tp, 2 * QBLK, max(krc, tp + 2 * QBLK))
    vc_p = _front_pad(vc16[:np_], nb, tp, 2 * QBLK, max(krc, tp + 2 * QBLK))
    bc_p = _bias_tile(rel_bias, C_KV_HEADS, C_HEADS // C_KV_HEADS, QBLK, ckp, -2 * QBLK) * LOG2E
    sinks = l1_sinks.astype(F32) * LOG2E
    oc_p = _swa(sinks, qc16, kc_p, vc_p, bc_p, nb=nb, nq=nq, tq=QBLK, row0=0, ck=ckp, r_base=0,
                r_step=QBLK, p_base=-2 * QBLK, p_step=QBLK, q_base=0, q_step=QBLK, coff=PROMPT_COFF, n_valid=t)

    buf = cache_c_k.shape[1]
    cks = -(-(buf + ds) // LANE) * LANE

    def cat_c(cache, new):
        a = jnp.concatenate([cache.reshape(db, buf, -1).astype(BF16), new[np_:].reshape(db, ds, -1)], axis=1)
        return jnp.pad(a, ((0, 0), (0, cks - buf - ds), (0, 0)))

    bc_s = _bias_tile(rel_bias, C_KV_HEADS, C_HEADS // C_KV_HEADS, ds, cks, -buf) * LOG2E
    oc_s = _swa(sinks, qc16, cat_c(cache_c_k, kc16), cat_c(cache_c_v, vc16), bc_s, nb=db, nq=1, tq=ds,
                row0=np_, ck=cks, r_base=0, r_step=0, p_base=past - buf, p_step=0, q_base=past,
                q_step=0, coff=0, n_valid=past + ds)
    return jnp.concatenate([oc_p, oc_s], axis=0), kc32, vc32


def _assemble(h2, ka32, va32, ki32, kb32, vb32, kc32, vc32, cache_c_k, cache_c_v, *, nb, tp, t, db, ds):
    np_ = nb * tp
    d = h2.shape[1]

    def pr(a, heads):
        a = a[:np_].reshape(nb, tp, -1)[:, :t]
        return a.reshape(nb, t, heads, HEAD_DIM) if heads else a[..., :HEAD_DIM]

    def sm(a, heads):
        a = a[np_:].reshape(db, ds, -1)
        return a.reshape(db, ds, heads, HEAD_DIM) if heads else a[..., :HEAD_DIM]

    y_prompt = h2[:np_].reshape(nb, tp, d)[:, N_META:t]
    y_sample = h2[np_:].reshape(db, ds, d)
    bufp = min(WINDOW, t)
    p_ck = pr(kc32, C_KV_HEADS)[:, t - bufp:]
    p_cv = pr(vc32, C_KV_HEADS)[:, t - bufp:]
    s_ck = jnp.concatenate([cache_c_k, sm(kc32, C_KV_HEADS)], axis=1)[:, ds:]
    s_cv = jnp.concatenate([cache_c_v, sm(vc32, C_KV_HEADS)], axis=1)[:, ds:]
    return (y_prompt, y_sample,
            pr(ka32, A_KV_HEADS), pr(va32, A_KV_HEADS), pr(ki32, 0), pr(kb32, B_HEADS), pr(vb32, B_HEADS),
            p_ck, p_cv,
            sm(ka32, A_KV_HEADS), sm(va32, A_KV_HEADS), sm(ki32, 0), sm(kb32, B_HEADS), sm(vb32, B_HEADS),
            s_ck, s_cv)
```

```python
import functools

import jax
import jax.numpy as jnp
from jax import lax
from jax.experimental import pallas as pl
from jax.experimental.pallas import tpu as pltpu

F32 = jnp.float32
BF16 = jnp.bfloat16
I32 = jnp.int32

D_MODEL = 1024
CHUNK_SHIFT = 6
N_META = 16
HEAD_DIM = 64
A_HEADS = 8
A_KV_HEADS = 2
IDX_HEADS = 4
TOPK_MAX = 256
B_HEADS = 8
C_HEADS = 16
C_KV_HEADS = 2
WINDOW = 128
WIN_CHUNKS = 2
NUM_BUCKETS = 32
N_EXPERTS = 8
TOP_K = 2
LN_EPS = 1e-5
DEPTH = 2
DN_ALPHA = (2.0 * DEPTH) ** 0.25

LANE = 128
QBLK = 128
DSA_CK = 512
DSA_PADF = 384
DSA_RB = 32
LOG2E = 1.4426950408889634
SB_BAND = 512
SB_PAIRS = 2
SB_STOP = -120.0
NEG = -1e30
PROMPT_COFF = 64 - N_META
INT_MIN = -2 ** 31
VMEM_LIMIT = 56 * 1024 * 1024


def _cparams(sem):
    return pltpu.CompilerParams(dimension_semantics=sem, vmem_limit_bytes=VMEM_LIMIT)


def _nt(a, b):
    return lax.dot_general(a, b, (((1,), (1,)), ((), ())), preferred_element_type=F32)


def _layer_norm(v, g, b):
    mu = jnp.mean(v, axis=-1, keepdims=True)
    c = v - mu
    var = jnp.mean(c * c, axis=-1, keepdims=True)
    return c * lax.rsqrt(var + LN_EPS) * g + b


def _proj_kernel(x_ref, w_ref, *out_refs, specs):
    xb = x_ref[...].astype(BF16)
    k = 0
    for c0, width, kinds in specs:
        acc = jnp.dot(xb, w_ref[:, c0:c0 + width], preferred_element_type=F32)
        for kind in kinds:
            if kind == "f32":
                val = acc
            elif kind == "bf16":
                val = acc.astype(BF16)
            elif kind == "bf16s":
                val = (acc * 0.125).astype(BF16)
            elif kind == "bf16e":
                val = (acc * (0.125 * LOG2E)).astype(BF16)
            elif kind == "hilo":
                hi = acc.astype(BF16)
                lo = (acc - hi.astype(F32)).astype(BF16)
                lane = lax.broadcasted_iota(I32, acc.shape, 1) & (LANE - 1)
                val = jnp.where(lane < HEAD_DIM, hi, lo)
            elif kind == "wi":
                val = acc * 0.0625
            else:
                raise ValueError(kind)
            out_refs[k][...] = val
            k += 1


def _project(x, w16, specs, tm):
    n = x.shape[0]
    out_shape, out_specs = [], []
    for _, width, kinds in specs:
        for kind in kinds:
            dt = F32 if kind in ("f32", "wi") else BF16
            out_shape.append(jax.ShapeDtypeStruct((n, width), dt))
            out_specs.append(pl.BlockSpec((tm, width), lambda i: (i, 0)))
    return pl.pallas_call(
        functools.partial(_proj_kernel, specs=specs),
        grid=(pl.cdiv(n, tm),),
        in_specs=[pl.BlockSpec((tm, x.shape[1]), lambda i: (i, 0)),
                  pl.BlockSpec(w16.shape, lambda i: (0, 0))],
        out_specs=out_specs,
        out_shape=out_shape,
        compiler_params=_cparams(("parallel",)),
        name="proj",
    )(x, w16)


def _mix_ln_kernel(x_ref, *refs, n_pairs):
    o_refs = refs[:n_pairs]
    w_refs = refs[n_pairs:2 * n_pairs]
    g_ref, b_ref, y_ref = refs[2 * n_pairs:]
    acc = DN_ALPHA * x_ref[...]
    for o_ref, w_ref in zip(o_refs, w_refs):
        acc = acc + jnp.dot(o_ref[...], w_ref[...], preferred_element_type=F32)
    y_ref[...] = _layer_norm(acc, g_ref[...], b_ref[...])


def _mix_ln(x, os_, ws, g, b, tm):
    n, d = x.shape
    in_specs = [pl.BlockSpec((tm, d), lambda i: (i, 0))]
    in_specs += [pl.BlockSpec((tm, o.shape[1]), lambda i: (i, 0)) for o in os_]
    in_specs += [pl.BlockSpec(w.shape, lambda i: (0, 0)) for w in ws]
    in_specs += [pl.BlockSpec((1, d), lambda i: (0, 0))] * 2
    return pl.pallas_call(
        functools.partial(_mix_ln_kernel, n_pairs=len(os_)),
        grid=(pl.cdiv(n, tm),),
        in_specs=in_specs,
        out_specs=pl.BlockSpec((tm, d), lambda i: (i, 0)),
        out_shape=jax.ShapeDtypeStruct((n, d), F32),
        compiler_params=_cparams(("parallel",)),
        name="mix_ln",
    )(x, *os_, *ws, g.reshape(1, d), b.reshape(1, d))


def _ffn_ln_kernel(y_ref, wg_ref, wu_ref, wd_ref, g_ref, b_ref, o_ref, *, nf):
    y = y_ref[...]
    yb = y.astype(BF16)
    acc = DN_ALPHA * y
    for f in range(nf):
        hg = jnp.dot(yb, wg_ref[f], preferred_element_type=F32)
        hu = jnp.dot(yb, wu_ref[f], preferred_element_type=F32)
        h = (hg * jax.nn.sigmoid(hg) * hu).astype(BF16)
        acc = acc + jnp.dot(h, wd_ref[f], preferred_element_type=F32)
    o_ref[...] = _layer_norm(acc, g_ref[...], b_ref[...])


def _ffn_ln(y, wg, wu, wd, g, b, tm, tf):
    n, d = y.shape
    dff = wg.shape[1]
    nf = dff // tf
    wg3 = wg.astype(BF16).reshape(d, nf, tf).transpose(1, 0, 2)
    wu3 = wu.astype(BF16).reshape(d, nf, tf).transpose(1, 0, 2)
    wd3 = wd.astype(BF16).reshape(nf, tf, d)
    full3 = lambda i: (0, 0, 0)
    return pl.pallas_call(
        functools.partial(_ffn_ln_kernel, nf=nf),
        grid=(pl.cdiv(n, tm),),
        in_specs=[pl.BlockSpec((tm, d), lambda i: (i, 0)),
                  pl.BlockSpec(wg3.shape, full3), pl.BlockSpec(wu3.shape, full3),
                  pl.BlockSpec(wd3.shape, full3),
                  pl.BlockSpec((1, d), lambda i: (0, 0)), pl.BlockSpec((1, d), lambda i: (0, 0))],
        out_specs=pl.BlockSpec((tm, d), lambda i: (i, 0)),
        out_shape=jax.ShapeDtypeStruct((n, d), F32),
        compiler_params=_cparams(("parallel",)),
        name="ffn_ln",
    )(y, wg3, wu3, wd3, g.reshape(1, d), b.reshape(1, d))


def _sort_key(x):
    bits = lax.bitcast_convert_type(x, I32)
    return bits ^ ((bits >> 31) & 0x7FFFFFFF)


def _dsa_kernel(rb_ref, qi_ref, wi_ref, qa_ref, ki_ref, ka_ref, va_ref, bn_ref, o_ref,
                keys_ref, cut_ref, s_ref, p_ref, madd_ref, wbc_ref, cmax_ref, alpha_ref,
                l_ref, acc_ref, seen_ref, *, tq, n_sel, e_base, e_step, q_base, q_step, coff, n_valid, padf):
    i = pl.program_id(1)
    ck = DSA_CK
    rbk = DSA_RB
    nt = ck // LANE
    e_end = e_base + e_step * i
    n_chunks = (e_end + ck - 1) // ck
    qpos = q_base + q_step * i + lax.broadcasted_iota(I32, (tq, 1), 0)
    bound = jnp.minimum(((((qpos + coff) >> CHUNK_SHIFT) + 1) << CHUNK_SHIFT) - coff, n_valid)
    lane_rb = lax.broadcasted_iota(I32, (rbk, ck), 1)
    lane128 = lax.broadcasted_iota(I32, (tq, LANE), 1)

    def key_rows(c):
        kpos0 = e_end - ck * (c + 1)
        return kpos0, pl.multiple_of(jnp.maximum(kpos0 + padf, 0), LANE)

    def tiles(x):
        return [x[:, t * LANE:(t + 1) * LANE] for t in range(nt)]

    qstack = jnp.concatenate([qi_ref[:, h * LANE:(h + 1) * LANE] for h in range(IDX_HEADS)], axis=0)
    for h in range(IDX_HEADS):
        wbc_ref[h] = jnp.broadcast_to(wi_ref[:, h:h + 1], (tq, LANE))
    cmax_ref[...] = jnp.full((tq, ck), -jnp.inf, F32)

    def idx_dot(c, dst):
        _, r = key_rows(c)
        s_ref[dst] = _nt(qstack, ki_ref[pl.ds(r, ck), :])

    def idx_keys(c, src):
        kpos0, _ = key_rows(c)
        for b in range(tq // rbk):
            r0 = b * rbk
            sc = None
            for h in range(IDX_HEADS):
                sh = jnp.maximum(s_ref[src, h * tq + r0:h * tq + r0 + rbk, :], 0.0)
                term = jnp.concatenate([wbc_ref[h, r0:r0 + rbk, :]] * nt, axis=1) * sh
                sc = term if sc is None else sc + term
            kpos = kpos0 + lane_rb
            allowed = (kpos >= 0) & (kpos < bound[r0:r0 + rbk])
            keys_ref[c, r0:r0 + rbk, :] = jnp.where(allowed, _sort_key(sc), INT_MIN)
            cmax_ref[r0:r0 + rbk, :] = jnp.maximum(cmax_ref[r0:r0 + rbk, :], jnp.where(allowed, sc, -jnp.inf))

    idx_dot(0, 0)

    def p1(j, carry):
        c = 2 * j
        idx_dot(c + 1, 1)
        idx_keys(c, 0)
        idx_dot(c + 2, 0)
        idx_keys(c + 1, 1)
        return carry

    lax.fori_loop(0, (n_chunks + 1) // 2, p1, 0)

    def count_ge(cand):
        def body(c, acc):
            u = keys_ref[c]
            for t in range(nt):
                acc = acc + jnp.where(u[:, t * LANE:(t + 1) * LANE] >= cand, 1.0, 0.0)
            return acc
        acc = lax.fori_loop(0, n_chunks, body, jnp.zeros((tq, LANE), F32))
        return jnp.sum(acc, axis=1, keepdims=True)

    cm = tiles(cmax_ref[...])
    if n_sel <= LANE:
        fold = [functools.reduce(jnp.maximum, cm)]
    elif n_sel <= 2 * LANE:
        fold = [jnp.maximum(cm[0], cm[2]), jnp.maximum(cm[1], cm[3])]
    else:
        fold = cm
    fmin = jnp.min(functools.reduce(jnp.minimum, fold), axis=1, keepdims=True)
    fmax = jnp.max(functools.reduce(jnp.maximum, cm), axis=1, keepdims=True)
    lo0 = _sort_key(fmin) - 1
    hi0 = _sort_key(fmax) + 2
    small = bound <= n_sel
    c_pos = count_ge(jnp.full((tq, 1), 1, I32))
    c_nn = count_ge(jnp.zeros((tq, 1), I32))
    pos_side = c_pos >= n_sel
    at_zero = jnp.logical_not(pos_side) & (c_nn >= n_sel)
    lo1 = jnp.where(pos_side, jnp.maximum(lo0, 1), lo0)
    top1 = jnp.where(pos_side, hi0, jnp.minimum(hi0, 0)) - 1
    fixed = small | at_zero
    nbits = jnp.where(fixed, 0, 32 - lax.clz(lo1 ^ top1))
    low_mask = (jnp.int32(1) << jnp.minimum(nbits, 31)) - 1
    t0 = jnp.where(small, INT_MIN + 1,
                   jnp.where(at_zero, 0, jnp.where(nbits >= 32, INT_MIN, lo1 & ~low_mask)))
    max_bits = jnp.max(nbits.astype(F32)).astype(I32)
    n_adm = bound.astype(F32)
    c_up0 = jnp.where(small, n_adm, jnp.where(at_zero, c_pos,
                                              jnp.where(pos_side | (hi0 <= 0), 0.0, c_nn)))
    c_at0 = jnp.where(small, n_adm, jnp.where(at_zero, c_nn, -1.0))

    def bis(it, st):
        t, c_at, c_up = st
        b = nbits - 1 - it
        cand = t + (jnp.int32(1) << jnp.maximum(b, 0))
        cnt = count_ge(cand)
        live = b >= 0
        take = live & (cnt >= n_sel)
        drop = live & (cnt < n_sel)
        return (jnp.where(take, cand, t), jnp.where(take, cnt, c_at), jnp.where(drop, cnt, c_up))

    thr, c_at, n_gt = lax.fori_loop(0, max_bits, bis, (t0, c_at0, c_up0))
    thr = jnp.maximum(thr, INT_MIN + 1)
    need = n_sel - n_gt
    excess = (c_at < 0.0) | ((c_at - n_gt) > need)

    cut_ref[...] = jnp.full((tq, LANE), 2 ** 30, I32)

    @pl.when(jnp.max(jnp.where(excess, 1.0, 0.0)) > 0.0)
    def _():
        rr = lax.broadcasted_iota(I32, (ck, ck), 0)
        cc = lax.broadcasted_iota(I32, (ck, ck), 1)
        upto = jnp.where(rr <= cc, 1.0, 0.0).astype(BF16)

        def tally(j, run):
            c = n_chunks - 1 - j
            seen_ref[c] = run
            u = keys_ref[c]
            for t in range(nt):
                run = run + jnp.where(u[:, t * LANE:(t + 1) * LANE] == thr, 1.0, 0.0)
            return run

        lax.fori_loop(0, n_chunks, tally, jnp.zeros((tq, LANE), F32))

        def locate(c, cut):
            kpos0, _ = key_rows(c)
            seen = jnp.sum(seen_ref[c], axis=1, keepdims=True)
            eq = jnp.where(keys_ref[c] == thr, 1.0, 0.0).astype(BF16)
            incl = jnp.dot(eq, upto, preferred_element_type=F32)
            below = jnp.sum(jnp.where(seen + incl < need, 1.0, 0.0), axis=1, keepdims=True)
            here = excess & (seen < need) & (seen + incl[:, ck - 1:ck] >= need)
            return jnp.where(here, kpos0 + below.astype(I32) + 1, cut)

        def locate4(j, cut):
            for k in range(4):
                cut = locate(jnp.minimum(4 * j + k, n_chunks - 1), cut)
            return cut

        cut = lax.fori_loop(0, (n_chunks + 3) // 4, locate4, jnp.full((tq, 1), 2 ** 30, I32))
        cut_ref[...] = jnp.broadcast_to(cut, (tq, LANE))

    cut = cut_ref[:, 0:1]

    r4 = A_HEADS // A_KV_HEADS
    qg = []
    for g in range(A_KV_HEADS):
        rows = []
        for j in range(r4):
            slot = qa_ref[:, j * LANE:(j + 1) * LANE]
            half = (lane128 < HEAD_DIM) if g == 0 else (lane128 >= HEAD_DIM)
            rows.append(jnp.where(half, slot, jnp.zeros_like(slot)))
        qg.append(jnp.concatenate(rows, axis=0))
    farb = [jnp.concatenate([jnp.full((tq, 1), rb_ref[g * r4 + j], F32) for j in range(r4)], axis=0)
            for g in range(A_KV_HEADS)]
    l_ref[...] = jnp.zeros(l_ref.shape, F32)
    acc_ref[...] = jnp.zeros(acc_ref.shape, F32)

    def logits_dot(c, g):
        _, r = key_rows(c)
        s_ref[g] = _nt(qg[g], ka_ref[pl.ds(r, ck), :])

    def select_mask(c):
        kpos0, _ = key_rows(c)
        for b in range(tq // rbk):
            r0 = b * rbk
            u = keys_ref[c, r0:r0 + rbk, :]
            t_b = thr[r0:r0 + rbk]
            sel = (u > t_b) | ((u == t_b) & (kpos0 + lane_rb < cut[r0:r0 + rbk]))
            madd_ref[r0:r0 + rbk, :] = jnp.where(sel, 0.0, NEG)

    def softmax_passes(g, m_old, near):
        m_rows = []
        for b in range(r4 * tq // rbk):
            r0 = b * rbk
            q0 = r0 % tq
            sm = s_ref[g, r0:r0 + rbk, :] + madd_ref[q0:q0 + rbk, :]
            if near:
                sm = sm + bn_ref[g, r0:r0 + rbk, :]
            m_blk = jnp.max(functools.reduce(jnp.maximum, tiles(sm)), axis=1, keepdims=True)
            far_b = farb[g][r0:r0 + rbk]
            if not near:
                m_blk = m_blk + far_b
            m_prev = m_old[r0:r0 + rbk]
            m_new = jnp.maximum(m_prev, m_blk)
            alpha = jnp.exp2(m_prev - m_new)
            alpha_ref[g, r0:r0 + rbk, :] = jnp.broadcast_to(alpha, (rbk, LANE))
            p = jnp.exp2(sm - (m_new if near else m_new - far_b))
            l_ref[g, r0:r0 + rbk, :] = alpha * l_ref[g, r0:r0 + rbk, :] + functools.reduce(jnp.add, tiles(p))
            p_ref[g, r0:r0 + rbk, :] = p.astype(BF16)
            m_rows.append(m_new)
        return jnp.concatenate(m_rows, axis=0)

    def value_dot(c, g):
        _, r = key_rows(c)
        acc_ref[g] = alpha_ref[g] * acc_ref[g] + jnp.dot(p_ref[g], va_ref[pl.ds(r, ck), :],
                                                        preferred_element_type=F32)

    m0 = jnp.full((r4 * tq, 1), NEG, F32)
    logits_dot(0, 0)
    logits_dot(0, 1)
    select_mask(0)
    m0n = softmax_passes(0, m0, True)
    value_dot(0, 0)
    logits_dot(1, 0)
    m1n = softmax_passes(1, m0, True)

    def p3(c, ms):
        value_dot(c - 1, 1)
        logits_dot(c, 1)
        select_mask(c)
        m_a = softmax_passes(0, ms[0], False)
        value_dot(c, 0)
        logits_dot(c + 1, 0)
        m_b = softmax_passes(1, ms[1], False)
        return (m_a, m_b)

    lax.fori_loop(1, n_chunks, p3, (m0n, m1n))
    value_dot(n_chunks - 1, 1)
    outs = [acc_ref[g] / jnp.sum(l_ref[g], axis=1, keepdims=True) for g in range(A_KV_HEADS)]
    for j in range(r4):
        lo = outs[0][j * tq:(j + 1) * tq]
        hi = outs[1][j * tq:(j + 1) * tq]
        o_ref[:, j * LANE:(j + 1) * LANE] = jnp.where(lane128 < HEAD_DIM, lo, hi).astype(BF16)


def _dsa(rb_far, qi, wi, qa, ki, ka, va, bn, *, nb, nq, tq, row0, n_sel, e_base, e_step,
         q_base, q_step, coff, n_valid, padf, n_chunks_max):
    rb0 = row0 // tq
    r4 = A_HEADS // A_KV_HEADS
    assert IDX_HEADS == r4 and tq % DSA_RB == 0
    qmap = lambda b, i: (rb0 + b * nq + i, 0)
    kmap = lambda b, i: (b, 0, 0)
    kern = functools.partial(_dsa_kernel, tq=tq, n_sel=n_sel, e_base=e_base, e_step=e_step,
                             q_base=q_base, q_step=q_step, coff=coff, n_valid=n_valid, padf=padf)
    return pl.pallas_call(
        kern,
        grid=(nb, nq),
        in_specs=[pl.BlockSpec(memory_space=pltpu.SMEM),
                  pl.BlockSpec((tq, qi.shape[1]), qmap),
                  pl.BlockSpec((tq, LANE), qmap),
                  pl.BlockSpec((tq, qa.shape[1]), qmap),
                  pl.BlockSpec((None,) + ki.shape[1:], kmap),
                  pl.BlockSpec((None,) + ka.shape[1:], kmap),
                  pl.BlockSpec((None,) + va.shape[1:], kmap),
                  pl.BlockSpec(bn.shape, lambda b, i: (0, 0, 0))],
        out_specs=pl.BlockSpec((tq, qa.shape[1]), lambda b, i: (b * nq + i, 0)),
        out_shape=jax.ShapeDtypeStruct((nb * nq * tq, qa.shape[1]), BF16),
        scratch_shapes=[pltpu.VMEM((n_chunks_max + 1, tq, DSA_CK), I32),
                        pltpu.VMEM((tq, LANE), I32),
                        pltpu.VMEM((2, r4 * tq, DSA_CK), F32),
                        pltpu.VMEM((A_KV_HEADS, r4 * tq, DSA_CK), BF16),
                        pltpu.VMEM((tq, DSA_CK), F32),
                        pltpu.VMEM((IDX_HEADS, tq, LANE), F32),
                        pltpu.VMEM((tq, DSA_CK), F32),
                        pltpu.VMEM((A_KV_HEADS, r4 * tq, LANE), F32),
                        pltpu.VMEM((A_KV_HEADS, r4 * tq, LANE), F32),
                        pltpu.VMEM((A_KV_HEADS, r4 * tq, LANE), F32),
                        pltpu.VMEM((n_chunks_max, tq, LANE), F32)],
        compiler_params=_cparams(("parallel", "arbitrary")),
        name="dsa",
    )(rb_far, qi, wi, qa, ki, ka, va, bn)


def _sb_kernel(tri_ref, q_ref, k_ref, v_ref, o_ref, *, tq, e_base, e_step, q_base, q_step, padf, n_valid):
    i = pl.program_id(2)
    bw = SB_BAND
    npp = q_ref.shape[1] // LANE
    nh = 2 * npp
    e_end = e_base + e_step * i
    n_bands = (e_end + bw - 1) // bw
    qpos = q_base + q_step * i + lax.broadcasted_iota(I32, (tq, 1), 0)
    lane = lax.broadcasted_iota(I32, (tq, LANE), 1)
    qh = []
    for p in range(npp):
        q = q_ref[:, p * LANE:(p + 1) * LANE]
        zero = jnp.zeros_like(q)
        qh += [jnp.where(lane < HEAD_DIM, q, zero), jnp.where(lane >= HEAD_DIM, q, zero)]
    lane_bw = lax.broadcasted_iota(I32, (tq, bw), 1)

    def cond(st):
        return (st[0] < n_bands) & (st[1] > SB_STOP)

    def body(st):
        m, _, carry, acc = st
        kpos0 = e_end - bw * (m + 1)
        r = pl.multiple_of(kpos0 + padf, LANE)
        kpos = kpos0 + lane_bw
        before = (kpos < qpos) & (kpos >= 0) & (kpos < n_valid)
        zs, lss, lks, parts = [], [], [], []
        for h in range(nh):
            kt = k_ref[pl.ds(r, bw), (h // 2) * LANE:(h // 2 + 1) * LANE]
            z = _nt(qh[h], kt)
            ls = -(jnp.maximum(z, 0.0) + jnp.log1p(jnp.exp(-jnp.abs(z))))
            lk = jnp.where(before, ls, 0.0)
            hi = lk.astype(BF16)
            parts += [hi, (lk - hi.astype(F32)).astype(BF16)]
            zs.append(z)
            lss.append(ls)
            lks.append(lk)
        sums = jnp.dot(jnp.concatenate(parts, axis=0), tri_ref[...], preferred_element_type=F32)
        new_c, new_a = [], []
        worst = jnp.float32(-jnp.inf)
        for h in range(nh):
            vt = v_ref[pl.ds(r, bw), (h // 2) * LANE:(h // 2 + 1) * LANE]
            bl = sums[2 * h * tq:(2 * h + 1) * tq] + sums[(2 * h + 1) * tq:(2 * h + 2) * tq]
            w = jnp.where(before, jnp.exp(lss[h] + zs[h] + bl + carry[h]), 0.0)
            new_a.append(acc[h] + jnp.dot(w.astype(BF16), vt, preferred_element_type=F32))
            c_n = carry[h] + bl[:, 0:1] + lks[h][:, 0:1]
            new_c.append(c_n)
            worst = jnp.maximum(worst, jnp.max(c_n))
        return (m + 1, worst, tuple(new_c), tuple(new_a))

    init = (jnp.int32(0), jnp.float32(0.0),
            tuple(jnp.zeros((tq, 1), F32) for _ in range(nh)),
            tuple(jnp.zeros((tq, LANE), F32) for _ in range(nh)))
    _, _, _, acc = lax.while_loop(cond, body, init)
    for p in range(npp):
        o_ref[:, p * LANE:(p + 1) * LANE] = jnp.where(lane < HEAD_DIM, acc[2 * p], acc[2 * p + 1]).astype(BF16)


def _stick_break(q, k, v, *, nb, nq, tq, row0, e_base, e_step, q_base, q_step, padf, n_valid):
    rb0 = row0 // tq
    wq = SB_PAIRS * LANE
    npair = q.shape[1] // wq
    kern = functools.partial(_sb_kernel, tq=tq, e_base=e_base, e_step=e_step, q_base=q_base,
                             q_step=q_step, padf=padf, n_valid=n_valid)
    rr = lax.broadcasted_iota(I32, (SB_BAND, SB_BAND), 0)
    cc = lax.broadcasted_iota(I32, (SB_BAND, SB_BAND), 1)
    tri = jnp.where(rr > cc, 1.0, 0.0).astype(BF16)
    kspec = pl.BlockSpec((None, k.shape[1], wq), lambda b, p, i: (b, 0, p))
    return pl.pallas_call(
        kern,
        grid=(nb, npair, nq),
        in_specs=[pl.BlockSpec(tri.shape, lambda b, p, i: (0, 0)),
                  pl.BlockSpec((tq, wq), lambda b, p, i: (rb0 + b * nq + i, p)), kspec, kspec],
        out_specs=pl.BlockSpec((tq, wq), lambda b, p, i: (b * nq + i, p)),
        out_shape=jax.ShapeDtypeStruct((nb * nq * tq, q.shape[1]), BF16),
        compiler_params=_cparams(("parallel", "parallel", "arbitrary")),
        name="stick_break",
    )(tri, q, k, v)


def _swa_kernel(sink_ref, q_ref, k_ref, v_ref, bias_ref, o_ref, s_ref, p_ref, madd_ref,
                *, tq, ck, r_base, r_step, p_base, p_step, q_base, q_step, coff, n_valid):
    i = pl.program_id(1)
    r0 = pl.multiple_of(r_base + r_step * i, 16)
    kpos0 = p_base + p_step * i
    kt = k_ref[pl.ds(r0, ck), :]
    vt = v_ref[pl.ds(r0, ck), :]
    qpos = q_base + q_step * i + lax.broadcasted_iota(I32, (tq, 1), 0)
    kpos = kpos0 + lax.broadcasted_iota(I32, (tq, ck), 1)
    qc = (qpos + coff) >> CHUNK_SHIFT
    hi_b = jnp.minimum(((qc + 1) << CHUNK_SHIFT) - coff, n_valid)
    lo_b = jnp.maximum(((qc - WIN_CHUNKS) << CHUNK_SHIFT) - coff, 0)
    r8 = C_HEADS // C_KV_HEADS
    rbk = DSA_RB
    nt = ck // LANE
    madd_ref[...] = jnp.where((kpos >= lo_b) & (kpos < hi_b), 0.0, NEG)
    lane = lax.broadcasted_iota(I32, (tq, LANE), 1)

    def tiles(x):
        return [x[:, t * LANE:(t + 1) * LANE] for t in range(nt)]

    outs = []
    for g in range(C_KV_HEADS):
        rows = []
        for j in range(r8):
            slot = q_ref[:, j * LANE:(j + 1) * LANE]
            half = (lane < HEAD_DIM) if g == 0 else (lane >= HEAD_DIM)
            rows.append(jnp.where(half, slot, jnp.zeros_like(slot)))
        s_ref[g] = _nt(jnp.concatenate(rows, axis=0), kt)
        dens = []
        for b in range(r8 * tq // rbk):
            rs = slice(b * rbk, (b + 1) * rbk)
            q0 = (b * rbk) % tq
            sink = jnp.full((rbk, 1), sink_ref[g * r8 + (b * rbk) // tq], F32)
            sm = s_ref[g, rs, :] + bias_ref[g, rs, :] + madd_ref[q0:q0 + rbk, :]
            m = jnp.maximum(jnp.max(functools.reduce(jnp.maximum, tiles(sm)), axis=1, keepdims=True), sink)
            e = jnp.exp2(sm - m)
            dens.append(jnp.sum(functools.reduce(jnp.add, tiles(e)), axis=1, keepdims=True)
                        + jnp.exp2(sink - m))
            p_ref[g, rs, :] = e.astype(BF16)
        outs.append(jnp.dot(p_ref[g], vt, preferred_element_type=F32) / jnp.concatenate(dens, axis=0))
    for j in range(r8):
        lo = outs[0][j * tq:(j + 1) * tq]
        hi = outs[1][j * tq:(j + 1) * tq]
        o_ref[:, j * LANE:(j + 1) * LANE] = jnp.where(lane < HEAD_DIM, lo, hi).astype(BF16)


def _swa(sinks, q, k, v, bias, *, nb, nq, tq, row0, ck, r_base, r_step, p_base, p_step,
         q_base, q_step, coff, n_valid):
    rb0 = row0 // tq
    r8 = C_HEADS // C_KV_HEADS
    assert tq % DSA_RB == 0 and ck % LANE == 0
    kern = functools.partial(_swa_kernel, tq=tq, ck=ck, r_base=r_base, r_step=r_step, p_base=p_base,
                             p_step=p_step, q_base=q_base, q_step=q_step, coff=coff, n_valid=n_valid)
    kmap = lambda b, i: (b, 0, 0)
    return pl.pallas_call(
        kern,
        grid=(nb, nq),
        in_specs=[pl.BlockSpec(memory_space=pltpu.SMEM),
                  pl.BlockSpec((tq, q.shape[1]), lambda b, i: (rb0 + b * nq + i, 0)),
                  pl.BlockSpec((None,) + k.shape[1:], kmap),
                  pl.BlockSpec((None,) + v.shape[1:], kmap),
                  pl.BlockSpec(bias.shape, lambda b, i: (0, 0, 0))],
        out_specs=pl.BlockSpec((tq, q.shape[1]), lambda b, i: (b * nq + i, 0)),
        out_shape=jax.ShapeDtypeStruct((nb * nq * tq, q.shape[1]), BF16),
        scratch_shapes=[pltpu.VMEM((C_KV_HEADS, r8 * tq, ck), F32),
                        pltpu.VMEM((C_KV_HEADS, r8 * tq, ck), BF16),
                        pltpu.VMEM((tq, ck), F32)],
        compiler_params=_cparams(("parallel", "arbitrary")),
        name="swa",
    )(sinks, q, k, v, bias)


def _router_kernel(y_ref, rh_ref, rl_ref, idx_ref, gate_ref):
    y = y_ref[...]
    yh = y.astype(BF16)
    yl = (y - yh.astype(F32)).astype(BF16)
    rh = rh_ref[...]
    logits = (jnp.dot(yh, rh, preferred_element_type=F32) + jnp.dot(yl, rh, preferred_element_type=F32)
              + jnp.dot(yh, rl_ref[...], preferred_element_type=F32))
    lane = lax.broadcasted_iota(I32, logits.shape, 1)
    logits = jnp.where(lane < N_EXPERTS, logits, -jnp.inf)
    m1 = jnp.max(logits, axis=1, keepdims=True)
    i1 = jnp.min(jnp.where(logits == m1, lane, LANE), axis=1, keepdims=True)
    rest = jnp.where(lane == i1, -jnp.inf, logits)
    m2 = jnp.max(rest, axis=1, keepdims=True)
    i2 = jnp.min(jnp.where(rest == m2, lane, LANE), axis=1, keepdims=True)
    e2 = jnp.exp(m2 - m1)
    den = 1.0 + e2
    idx_ref[...] = jnp.where(lane == 0, i1, jnp.where(lane == 1, i2, 0))
    gate_ref[...] = jnp.where(lane == 0, 1.0 / den, jnp.where(lane == 1, e2 / den, 0.0))


def _router(y, router, tm):
    n, d = y.shape
    rpad = jnp.pad(router.astype(F32), ((0, 0), (0, LANE - router.shape[1])))
    rh = rpad.astype(BF16)
    rl = (rpad - rh.astype(F32)).astype(BF16)
    return pl.pallas_call(
        _router_kernel,
        grid=(pl.cdiv(n, tm),),
        in_specs=[pl.BlockSpec((tm, d), lambda i: (i, 0)),
                  pl.BlockSpec((d, LANE), lambda i: (0, 0)), pl.BlockSpec((d, LANE), lambda i: (0, 0))],
        out_specs=[pl.BlockSpec((tm, LANE), lambda i: (i, 0)), pl.BlockSpec((tm, LANE), lambda i: (i, 0))],
        out_shape=[jax.ShapeDtypeStruct((n, LANE), I32), jax.ShapeDtypeStruct((n, LANE), F32)],
        compiler_params=_cparams(("parallel",)),
        name="router",
    )(y, rh, rl)


def _moe_kernel(be_ref, nu_ref, x_ref, wg_ref, wu_ref, wd_ref, o_ref, acc_ref, xb_ref):
    t = pl.program_id(0)
    f = pl.program_id(1)

    @pl.when(f == 0)
    def _():
        acc_ref[...] = jnp.zeros_like(acc_ref)
        xb_ref[...] = x_ref[...].astype(BF16)

    @pl.when(t < nu_ref[0])
    def _():
        xb = xb_ref[...]
        hg = jnp.dot(xb, wg_ref[...].astype(BF16), preferred_element_type=F32)
        hu = jnp.dot(xb, wu_ref[...].astype(BF16), preferred_element_type=F32)
        h = (hg * jax.nn.sigmoid(hg) * hu).astype(BF16)
        acc_ref[...] += jnp.dot(h, wd_ref[...].astype(BF16), preferred_element_type=F32)

    @pl.when(f == pl.num_programs(1) - 1)
    def _():
        o_ref[...] = acc_ref[...]


def _moe_experts(block_expert, n_used, xs, wg, wu, wd, tmb, tf):
    n_slots, d = xs.shape
    de = wg.shape[2]
    grid_spec = pltpu.PrefetchScalarGridSpec(
        num_scalar_prefetch=2,
        grid=(n_slots // tmb, de // tf),
        in_specs=[pl.BlockSpec((tmb, d), lambda t, f, be, nu: (t, 0)),
                  pl.BlockSpec((None, d, tf), lambda t, f, be, nu: (be[t], 0, f)),
                  pl.BlockSpec((None, d, tf), lambda t, f, be, nu: (be[t], 0, f)),
                  pl.BlockSpec((None, tf, d), lambda t, f, be, nu: (be[t], f, 0))],
        out_specs=pl.BlockSpec((tmb, d), lambda t, f, be, nu: (t, 0)),
        scratch_shapes=[pltpu.VMEM((tmb, d), F32), pltpu.VMEM((tmb, d), BF16)],
    )
    return pl.pallas_call(
        _moe_kernel,
        grid_spec=grid_spec,
        out_shape=jax.ShapeDtypeStruct((n_slots, d), F32),
        compiler_params=_cparams(("parallel", "arbitrary")),
        name="moe_experts",
    )(block_expert, n_used, xs, wg, wu, wd)


def _row_copy(src_hbm, src_row, dst, dst_row, sem):
    return pltpu.make_async_copy(src_hbm.at[pl.ds(src_row, 1)], dst.at[pl.ds(dst_row, 1)], sem)


def _dispatch_kernel(slot_ref, x_ref, init_hbm, xs_hbm, sem, *, tt):
    del init_hbm

    def start(j, carry):
        for k in range(TOP_K):
            _row_copy(x_ref, j, xs_hbm, slot_ref[0, 0, TOP_K * j + k], sem).start(priority=k % 2)
        return carry

    def wait(j, carry):
        _row_copy(x_ref, 0, xs_hbm, 0, sem).wait()
        return carry

    lax.fori_loop(0, tt, start, 0, unroll=8)
    lax.fori_loop(0, TOP_K * tt, wait, 0, unroll=8)


def _dispatch(x, slot3, n_slots, tt):
    n, d = x.shape
    init = jnp.zeros((n_slots, d), x.dtype)
    return pl.pallas_call(
        functools.partial(_dispatch_kernel, tt=tt),
        grid=(n // tt,),
        in_specs=[pl.BlockSpec((1, 1, TOP_K * tt), lambda i: (i, 0, 0), memory_space=pltpu.SMEM),
                  pl.BlockSpec((tt, d), lambda i: (i, 0)), pl.BlockSpec(memory_space=pl.ANY)],
        out_specs=pl.BlockSpec(memory_space=pl.ANY),
        out_shape=jax.ShapeDtypeStruct((n_slots, d), x.dtype),
        scratch_shapes=[pltpu.SemaphoreType.DMA],
        input_output_aliases={2: 0},
        compiler_params=_cparams(("arbitrary",)),
        name="moe_dispatch",
    )(slot3, x, init)


def _combine_ln_kernel(slot_ref, y_ref, gate_ref, g_ref, b_ref, outs_hbm, o_ref, buf_ref, sem, *, tt):
    def start(j, carry):
        for k in range(TOP_K):
            _row_copy(outs_hbm, slot_ref[0, 0, TOP_K * j + k], buf_ref.at[k], j, sem).start(priority=k % 2)
        return carry

    def wait(j, carry):
        _row_copy(outs_hbm, 0, buf_ref.at[0], 0, sem).wait()
        return carry

    lax.fori_loop(0, tt, start, 0, unroll=8)
    lax.fori_loop(0, TOP_K * tt, wait, 0, unroll=8)
    mo = gate_ref[:, 0:1] * buf_ref[0]
    for k in range(1, TOP_K):
        mo = mo + gate_ref[:, k:k + 1] * buf_ref[k]
    o_ref[...] = _layer_norm(DN_ALPHA * y_ref[...] + mo, g_ref[...], b_ref[...])


def _combine_ln(y, outs, slot3, gate_p, g, b, tt):
    n, d = y.shape
    row = pl.BlockSpec((tt, d), lambda i: (i, 0))
    vec = pl.BlockSpec((1, d), lambda i: (0, 0))
    return pl.pallas_call(
        functools.partial(_combine_ln_kernel, tt=tt),
        grid=(n // tt,),
        in_specs=[pl.BlockSpec((1, 1, TOP_K * tt), lambda i: (i, 0, 0), memory_space=pltpu.SMEM),
                  row, pl.BlockSpec((tt, LANE), lambda i: (i, 0)), vec, vec,
                  pl.BlockSpec(memory_space=pl.ANY)],
        out_specs=row,
        out_shape=jax.ShapeDtypeStruct((n, d), F32),
        scratch_shapes=[pltpu.VMEM((TOP_K, tt, d), F32), pltpu.SemaphoreType.DMA],
        compiler_params=_cparams(("arbitrary",)),
        name="moe_combine_ln",
    )(slot3, y, gate_p, g.reshape(1, d), b.reshape(1, d), outs)


def _moe(y, router, w_gate, w_up, w_down, g, b, tm, tmb, tf, tt):
    n, d = y.shape
    assert n % tt == 0
    idx_p, gate_p = _router(y, router, tm)
    e_flat = idx_p[:, :TOP_K].reshape(-1)
    onehot = (e_flat[:, None] == jnp.arange(N_EXPERTS, dtype=I32)[None, :]).astype(I32)
    rank = jnp.sum((jnp.cumsum(onehot, axis=0) - onehot) * onehot, axis=1)
    counts = jnp.sum(onehot, axis=0)
    padded = (counts + tmb - 1) // tmb * tmb
    pad_end = jnp.cumsum(padded)
    slot = ((pad_end - padded)[e_flat] + rank).astype(I32)
    n_blocks = -(-(TOP_K * n) // tmb) + N_EXPERTS
    n_slots = n_blocks * tmb
    block_expert = jnp.minimum(
        jnp.searchsorted(pad_end, jnp.arange(n_blocks, dtype=I32) * tmb, side="right"),
        N_EXPERTS - 1).astype(I32)
    n_used = (pad_end[-1] // tmb).astype(I32).reshape(1)
    slot3 = slot.reshape(n // tt, 1, TOP_K * tt)
    xs = _dispatch(y, slot3, n_slots, tt)
    outs = _moe_experts(block_expert, n_used, xs, w_gate, w_up, w_down, tmb, tf)
    return _combine_ln(y, outs, slot3, gate_p, g, b, tt)


def _t5_bucket(rel):
    half = NUM_BUCKETS // 2
    exact = half // 2
    n = jnp.abs(rel)
    far = exact + sum((n >= t).astype(I32) for t in (12, 16, 23, 32, 46, 64, 91))
    return jnp.where(rel > 0, half, 0) + jnp.where(n < exact, n, far)


def _bias_tile(rel_bias, n_groups, per_group, tq, ck, d0):
    d = d0 + jnp.arange(ck, dtype=I32)[None, :] - jnp.arange(tq, dtype=I32)[:, None]
    onehot = (_t5_bucket(d)[..., None] == jnp.arange(NUM_BUCKETS, dtype=I32)).astype(F32)
    tile = jnp.einsum("qkb,bh->qkh", onehot, rel_bias.astype(F32),
                      precision=lax.Precision.HIGHEST)
    tile = jnp.transpose(tile[:, :, :n_groups * per_group], (2, 0, 1))
    return tile.reshape(n_groups, per_group * tq, ck)


def _cols(w, ranges):
    parts = []
    for r in ranges:
        if isinstance(r, int):
            parts.append(jnp.zeros((w.shape[0], r), w.dtype))
        else:
            parts.append(w[:, r[0]:r[1]])
    return jnp.concatenate(parts, axis=1)


def _head_pair_order(n_heads):
    half = n_heads // 2
    order = []
    for j in range(half):
        order += [j, half + j]
    return order


L0_SPECS = ((0, 512, ("bf16e",)), (512, 128, ("f32", "bf16")), (640, 128, ("f32", "bf16")),
            (768, 512, ("hilo",)), (1280, 128, ("f32", "bf16")), (1408, 128, ("wi",)),
            (1536, 512, ("bf16s",)), (2048, 512, ("f32", "bf16")), (2560, 512, ("f32", "bf16")))
L1_SPECS = ((0, 1024, ("bf16e",)), (1024, 128, ("f32", "bf16")), (1152, 128, ("f32", "bf16")))


def _l0_weight(w_in):
    hd = HEAD_DIM
    rng = [(hd * h, hd * h + hd) for h in _head_pair_order(A_HEADS)]
    rng += [(512, 640), (640, 768)]
    for h in range(IDX_HEADS):
        rng += [(768 + hd * h, 768 + hd * h + hd)] * 2
    rng += [(1024, 1088)] * 2
    rng += [(1088, 1092), LANE - IDX_HEADS]
    rng += [(1092, 1604), (1604, 2116), (2116, 2628)]
    return _cols(w_in, rng).astype(BF16)


def _l1_weight(w_in):
    hd = HEAD_DIM
    rng = [(hd * h, hd * h + hd) for h in _head_pair_order(C_HEADS)]
    rng += [(1024, 1152), (1152, 1280)]
    return _cols(w_in, rng).astype(BF16)


def _perm_rows(w, n_heads):
    return jnp.concatenate([w[HEAD_DIM * h:HEAD_DIM * (h + 1)] for h in _head_pair_order(n_heads)], axis=0)


def _front_pad(a, nb, rows_in, front, rows_out):
    a = a.reshape(nb, rows_in, a.shape[-1])
    return jnp.pad(a, ((0, 0), (front, rows_out - front - rows_in), (0, 0)))


def _layer0_attention(x_prompt, x_sample, cache_a_k, cache_a_v, cache_a_idx_k, cache_b_k, cache_b_v,
                      meta_tokens, rel_bias, l0_w_in):
    nb, seq, d = x_prompt.shape
    t = N_META + seq
    nq = -(-t // QBLK)
    tp = nq * QBLK
    db, ds, _ = x_sample.shape
    past = cache_a_k.shape[1]
    assert d == D_MODEL and past % LANE == 0 and ds % DSA_RB == 0 and ds <= 64
    np_, ns = nb * tp, db * ds
    tm = 512

    meta = jnp.broadcast_to(meta_tokens[None].astype(x_prompt.dtype), (nb, N_META, d))
    hp = jnp.concatenate([meta, x_prompt, jnp.zeros((nb, tp - t, d), x_prompt.dtype)], axis=1)
    x_all = jnp.concatenate([hp.reshape(np_, d), x_sample.reshape(ns, d)], axis=0)
    rel_bias = rel_bias.astype(F32)
    rb_far = rel_bias[NUM_BUCKETS // 2 - 1] * LOG2E

    (qa16, ka32, ka16, va32, va16, qi16, ki32, ki16, wi32, qb16, kb32, kb16, vb32, vb16) = _project(
        x_all, _l0_weight(l0_w_in), L0_SPECS, tm)

    n_sel_p = min(TOPK_MAX, (t - N_META) // 4)
    kr = QBLK * (nq + 1) + DSA_PADF
    ncm = -(-(QBLK * (nq + 1)) // DSA_CK)
    pk = [_front_pad(a[:np_], nb, tp, DSA_PADF, kr) for a in (ki16, ka16, va16)]
    bn_p = _bias_tile(rel_bias, A_KV_HEADS, A_HEADS // A_KV_HEADS, QBLK, DSA_CK, -(DSA_CK // 2)) * LOG2E
    oa_p = _dsa(rb_far, qi16, wi32, qa16, *pk, bn_p, nb=nb, nq=nq, tq=QBLK, row0=0, n_sel=n_sel_p,
                e_base=2 * QBLK, e_step=QBLK, q_base=0, q_step=QBLK, coff=PROMPT_COFF,
                n_valid=t, padf=DSA_PADF, n_chunks_max=ncm)

    n_keys = past + ds
    n_sel_s = min(TOPK_MAX, n_keys // 4)
    e_s = -(-n_keys // LANE) * LANE
    ncs = -(-e_s // DSA_CK)
    krs = ncs * DSA_CK
    fs = krs - e_s

    def cat_keys(cache, new, width_dup):
        c = cache.reshape(db, past, -1).astype(BF16)
        if width_dup:
            c = jnp.concatenate([c, c], axis=-1)
        a = jnp.concatenate([c, new[np_:].reshape(db, ds, -1)], axis=1)
        return jnp.pad(a, ((0, 0), (fs, krs - fs - n_keys), (0, 0)))

    sk = [cat_keys(cache_a_idx_k, ki16, True), cat_keys(cache_a_k, ka16, False),
          cat_keys(cache_a_v, va16, False)]
    bn_s = _bias_tile(rel_bias, A_KV_HEADS, A_HEADS // A_KV_HEADS, ds, DSA_CK, e_s - DSA_CK - past) * LOG2E
    oa_s = _dsa(rb_far, qi16, wi32, qa16, *sk, bn_s, nb=db, nq=1, tq=ds, row0=np_, n_sel=n_sel_s,
                e_base=e_s, e_step=0, q_base=past, q_step=0, coff=0, n_valid=n_keys, padf=fs,
                n_chunks_max=ncs)

    pfb = SB_BAND - QBLK
    kb_p = _front_pad(kb16[:np_], nb, tp, pfb, tp + pfb)
    vb_p = _front_pad(vb16[:np_], nb, tp, pfb, tp + pfb)
    ob_p = _stick_break(qb16, kb_p, vb_p, nb=nb, nq=nq, tq=QBLK, row0=0, e_base=QBLK, e_step=QBLK,
                        q_base=0, q_step=QBLK, padf=pfb, n_valid=t)
    rows_s = -(-e_s // SB_BAND) * SB_BAND
    pfs = rows_s - e_s

    def cat_b(cache, new):
        a = jnp.concatenate([cache.reshape(db, past, -1).astype(BF16), new[np_:].reshape(db, ds, -1)], axis=1)
        return jnp.pad(a, ((0, 0), (pfs, rows_s - pfs - n_keys), (0, 0)))

    ob_s = _stick_break(qb16, cat_b(cache_b_k, kb16), cat_b(cache_b_v, vb16), nb=db, nq=1, tq=ds,
                        row0=np_, e_base=e_s, e_step=0, q_base=past, q_step=0, padf=pfs, n_valid=n_keys)

    oa = jnp.concatenate([oa_p, oa_s], axis=0)
    ob = jnp.concatenate([ob_p, ob_s], axis=0)
    return dict(x_all=x_all, oa=oa, ob=ob, np=np_, tp=tp, t=t, nq=nq, rel_bias=rel_bias,
                ka32=ka32, va32=va32, ki32=ki32, kb32=kb32, vb32=vb32)


def kernel(x_prompt, x_sample, cache_a_k, cache_a_v, cache_a_idx_k, cache_b_k, cache_b_v, cache_c_k, cache_c_v, meta_tokens, rel_bias, l0_w_in, l0_w_out, l0_ln1_g, l0_ln1_b, l0_w_gate, l0_w_up, l0_w_down, l0_ln2_g, l0_ln2_b, l1_w_in, l1_sinks, l1_w_out, l1_ln1_g, l1_ln1_b, l1_router, l1_w_gate, l1_w_up, l1_w_down, l1_ln2_g, l1_ln2_b):
    a0 = _layer0_attention(x_prompt, x_sample, cache_a_k, cache_a_v, cache_a_idx_k, cache_b_k, cache_b_v,
                           meta_tokens, rel_bias, l0_w_in)
    x_all, oa, ob, np_, tp, t, nq, rel_bias = (a0[k] for k in ("x_all", "oa", "ob", "np", "tp", "t", "nq", "rel_bias"))
    ka32, va32, ki32, kb32, vb32 = (a0[k] for k in ("ka32", "va32", "ki32", "kb32", "vb32"))
    nb, _, d = x_prompt.shape
    db, ds, _ = x_sample.shape
    past = cache_a_k.shape[1]
    tm = 512
    w_out0 = l0_w_out.astype(BF16)
    half0 = A_HEADS * HEAD_DIM
    y0 = _mix_ln(x_all, [oa, ob], [_perm_rows(w_out0[:half0], A_HEADS), w_out0[half0:]],
                 l0_ln1_g, l0_ln1_b, tm)
    h1 = _ffn_ln(y0, l0_w_gate, l0_w_up, l0_w_down, l0_ln2_g, l0_ln2_b, tm, 256)

    oc, kc32, vc32 = _layer1_attention(h1, cache_c_k, cache_c_v, rel_bias, l1_w_in, l1_sinks,
                                       nb=nb, nq=nq, t=t, db=db, ds=ds, past=past, tm=tm)
    y1 = _mix_ln(h1, [oc], [_perm_rows(l1_w_out.astype(BF16), C_HEADS)], l1_ln1_g, l1_ln1_b, tm)
    n_all = y1.shape[0]
    tt = next(c for c in (256, 128, 64, 32, 16, 8) if n_all % c == 0)
    h2 = _moe(y1, l1_router, l1_w_gate, l1_w_up, l1_w_down, l1_ln2_g, l1_ln2_b, tm, 1024, 512, tt)
    return _assemble(h2, ka32, va32, ki32, kb32, vb32, kc32, vc32, cache_c_k, cache_c_v,
                     nb=nb, tp=tp, t=t, db=db, ds=ds)


def _layer1_attention(h1, cache_c_k, cache_c_v, rel_bias, l1_w_in, l1_sinks, *, nb, nq, t, db, ds, past, tm):
    tp = nq * QBLK
    np_ = nb * tp
    qc16, kc32, kc16, vc32, vc16 = _project(h1, _l1_weight(l1_w_in), L1_SPECS, tm)
    ckp = 4 * QBLK
    krc = QBLK * (nq - 1) + ckp
    kc_p = _front_pad(kc16[:np_], nb, tp, 2 * QBLK, max(krc, tp + 2 * QBLK))
    vc_p = _front_pad(vc16[:np_], nb, tp, 2 * QBLK, max(krc, tp + 2 * QBLK))
    bc_p = _bias_tile(rel_bias, C_KV_HEADS, C_HEADS // C_KV_HEADS, QBLK, ckp, -2 * QBLK) * LOG2E
    sinks = l1_sinks.astype(F32) * LOG2E
    oc_p = _swa(sinks, qc16, kc_p, vc_p, bc_p, nb=nb, nq=nq, tq=QBLK, row0=0, ck=ckp, r_base=0,
                r_step=QBLK, p_base=-2 * QBLK, p_step=QBLK, q_base=0, q_step=QBLK, coff=PROMPT_COFF, n_valid=t)

    buf = cache_c_k.shape[1]
    cks = -(-(buf + ds) // LANE) * LANE

    def cat_c(cache, new):
        a = jnp.concatenate([cache.reshape(db, buf, -1).astype(BF16), new[np_:].reshape(db, ds, -1)], axis=1)
        return jnp.pad(a, ((0, 0), (0, cks - buf - ds), (0, 0)))

    bc_s = _bias_tile(rel_bias, C_KV_HEADS, C_HEADS // C_KV_HEADS, ds, cks, -buf) * LOG2E
    oc_s = _swa(sinks, qc16, cat_c(cache_c_k, kc16), cat_c(cache_c_v, vc16), bc_s, nb=db, nq=1, tq=ds,
                row0=np_, ck=cks, r_base=0, r_step=0, p_base=past - buf, p_step=0, q_base=past,
                q_step=0, coff=0, n_valid=past + ds)
    return jnp.concatenate([oc_p, oc_s], axis=0), kc32, vc32


def _assemble(h2, ka32, va32, ki32, kb32, vb32, kc32, vc32, cache_c_k, cache_c_v, *, nb, tp, t, db, ds):
    np_ = nb * tp
    d = h2.shape[1]

    def pr(a, heads):
        a = a[:np_].reshape(nb, tp, -1)[:, :t]
        return a.reshape(nb, t, heads, HEAD_DIM) if heads else a[..., :HEAD_DIM]

    def sm(a, heads):
        a = a[np_:].reshape(db, ds, -1)
        return a.reshape(db, ds, heads, HEAD_DIM) if heads else a[..., :HEAD_DIM]

    y_prompt = h2[:np_].reshape(nb, tp, d)[:, N_META:t]
    y_sample = h2[np_:].reshape(db, ds, d)
    bufp = min(WINDOW, t)
    p_ck = pr(kc32, C_KV_HEADS)[:, t - bufp:]
    p_cv = pr(vc32, C_KV_HEADS)[:, t - bufp:]
    s_ck = jnp.concatenate([cache_c_k, sm(kc32, C_KV_HEADS)], axis=1)[:, ds:]
    s_cv = jnp.concatenate([cache_c_v, sm(vc32, C_KV_HEADS)], axis=1)[:, ds:]
    return (y_prompt, y_sample,
            pr(ka32, A_KV_HEADS), pr(va32, A_KV_HEADS), pr(ki32, 0), pr(kb32, B_HEADS), pr(vb32, B_HEADS),
            p_ck, p_cv,
            sm(ka32, A_KV_HEADS), sm(va32, A_KV_HEADS), sm(ki32, 0), sm(kb32, B_HEADS), sm(vb32, B_HEADS),
            s_ck, s_cv)
```

```python
import functools

import jax
import jax.numpy as jnp
from jax import lax
from jax.experimental import pallas as pl
from jax.experimental.pallas import tpu as pltpu

F32 = jnp.float32
BF16 = jnp.bfloat16
I32 = jnp.int32

D_MODEL = 1024
CHUNK_SHIFT = 6
N_META = 16
HEAD_DIM = 64
A_HEADS = 8
A_KV_HEADS = 2
IDX_HEADS = 4
TOPK_MAX = 256
B_HEADS = 8
C_HEADS = 16
C_KV_HEADS = 2
WINDOW = 128
WIN_CHUNKS = 2
NUM_BUCKETS = 32
N_EXPERTS = 8
TOP_K = 2
LN_EPS = 1e-5
DEPTH = 2
DN_ALPHA = (2.0 * DEPTH) ** 0.25

LANE = 128
QBLK = 128
DSA_CK = 512
DSA_PADF = 384
DSA_RB = 32
LOG2E = 1.4426950408889634
SB_BAND = 512
SB_PAIRS = 2
SB_STOP = -120.0
NEG = -1e30
PROMPT_COFF = 64 - N_META
INT_MIN = -2 ** 31
VMEM_LIMIT = 56 * 1024 * 1024


def _cparams(sem):
    return pltpu.CompilerParams(dimension_semantics=sem, vmem_limit_bytes=VMEM_LIMIT)


def _nt(a, b):
    return lax.dot_general(a, b, (((1,), (1,)), ((), ())), preferred_element_type=F32)


def _layer_norm(v, g, b):
    mu = jnp.mean(v, axis=-1, keepdims=True)
    c = v - mu
    var = jnp.mean(c * c, axis=-1, keepdims=True)
    return c * lax.rsqrt(var + LN_EPS) * g + b


def _proj_kernel(x_ref, w_ref, *out_refs, specs):
    xb = x_ref[...].astype(BF16)
    k = 0
    for c0, width, kinds in specs:
        acc = jnp.dot(xb, w_ref[:, c0:c0 + width], preferred_element_type=F32)
        for kind in kinds:
            if kind == "f32":
                val = acc
            elif kind == "bf16":
                val = acc.astype(BF16)
            elif kind == "bf16s":
                val = (acc * 0.125).astype(BF16)
            elif kind == "bf16e":
                val = (acc * (0.125 * LOG2E)).astype(BF16)
            elif kind == "hilo":
                hi = acc.astype(BF16)
                lo = (acc - hi.astype(F32)).astype(BF16)
                lane = lax.broadcasted_iota(I32, acc.shape, 1) & (LANE - 1)
                val = jnp.where(lane < HEAD_DIM, hi, lo)
            elif kind == "wi":
                val = acc * 0.0625
            else:
                raise ValueError(kind)
            out_refs[k][...] = val
            k += 1


def _project(x, w16, specs, tm):
    n = x.shape[0]
    out_shape, out_specs = [], []
    for _, width, kinds in specs:
        for kind in kinds:
            dt = F32 if kind in ("f32", "wi") else BF16
            out_shape.append(jax.ShapeDtypeStruct((n, width), dt))
            out_specs.append(pl.BlockSpec((tm, width), lambda i: (i, 0)))
    return pl.pallas_call(
        functools.partial(_proj_kernel, specs=specs),
        grid=(pl.cdiv(n, tm),),
        in_specs=[pl.BlockSpec((tm, x.shape[1]), lambda i: (i, 0)),
                  pl.BlockSpec(w16.shape, lambda i: (0, 0))],
        out_specs=out_specs,
        out_shape=out_shape,
        compiler_params=_cparams(("parallel",)),
        name="proj",
    )(x, w16)


def _mix_ln_kernel(x_ref, *refs, n_pairs):
    o_refs = refs[:n_pairs]
    w_refs = refs[n_pairs:2 * n_pairs]
    g_ref, b_ref, y_ref = refs[2 * n_pairs:]
    acc = DN_ALPHA * x_ref[...]
    for o_ref, w_ref in zip(o_refs, w_refs):
        acc = acc + jnp.dot(o_ref[...], w_ref[...], preferred_element_type=F32)
    y_ref[...] = _layer_norm(acc, g_ref[...], b_ref[...])


def _mix_ln(x, os_, ws, g, b, tm):
    n, d = x.shape
    in_specs = [pl.BlockSpec((tm, d), lambda i: (i, 0))]
    in_specs += [pl.BlockSpec((tm, o.shape[1]), lambda i: (i, 0)) for o in os_]
    in_specs += [pl.BlockSpec(w.shape, lambda i: (0, 0)) for w in ws]
    in_specs += [pl.BlockSpec((1, d), lambda i: (0, 0))] * 2
    return pl.pallas_call(
        functools.partial(_mix_ln_kernel, n_pairs=len(os_)),
        grid=(pl.cdiv(n, tm),),
        in_specs=in_specs,
        out_specs=pl.BlockSpec((tm, d), lambda i: (i, 0)),
        out_shape=jax.ShapeDtypeStruct((n, d), F32),
        compiler_params=_cparams(("parallel",)),
        name="mix_ln",
    )(x, *os_, *ws, g.reshape(1, d), b.reshape(1, d))


def _ffn_ln_kernel(y_ref, wg_ref, wu_ref, wd_ref, g_ref, b_ref, o_ref, *, nf):
    y = y_ref[...]
    yb = y.astype(BF16)
    acc = DN_ALPHA * y
    for f in range(nf):
        hg = jnp.dot(yb, wg_ref[f], preferred_element_type=F32)
        hu = jnp.dot(yb, wu_ref[f], preferred_element_type=F32)
        h = (hg * jax.nn.sigmoid(hg) * hu).astype(BF16)
        acc = acc + jnp.dot(h, wd_ref[f], preferred_element_type=F32)
    o_ref[...] = _layer_norm(acc, g_ref[...], b_ref[...])


def _ffn_ln(y, wg, wu, wd, g, b, tm, tf):
    n, d = y.shape
    dff = wg.shape[1]
    nf = dff // tf
    wg3 = wg.astype(BF16).reshape(d, nf, tf).transpose(1, 0, 2)
    wu3 = wu.astype(BF16).reshape(d, nf, tf).transpose(1, 0, 2)
    wd3 = wd.astype(BF16).reshape(nf, tf, d)
    full3 = lambda i: (0, 0, 0)
    return pl.pallas_call(
        functools.partial(_ffn_ln_kernel, nf=nf),
        grid=(pl.cdiv(n, tm),),
        in_specs=[pl.BlockSpec((tm, d), lambda i: (i, 0)),
                  pl.BlockSpec(wg3.shape, full3), pl.BlockSpec(wu3.shape, full3),
                  pl.BlockSpec(wd3.shape, full3),
                  pl.BlockSpec((1, d), lambda i: (0, 0)), pl.BlockSpec((1, d), lambda i: (0, 0))],
        out_specs=pl.BlockSpec((tm, d), lambda i: (i, 0)),
        out_shape=jax.ShapeDtypeStruct((n, d), F32),
        compiler_params=_cparams(("parallel",)),
        name="ffn_ln",
    )(y, wg3, wu3, wd3, g.reshape(1, d), b.reshape(1, d))


def _sort_key(x):
    bits = lax.bitcast_convert_type(x, I32)
    return bits ^ ((bits >> 31) & 0x7FFFFFFF)


def _dsa_kernel(rb_ref, qi_ref, wi_ref, qa_ref, ki_ref, ka_ref, va_ref, bn_ref, o_ref,
                keys_ref, cut_ref, s_ref, p_ref, madd_ref, wbc_ref, cmax_ref, alpha_ref,
                l_ref, acc_ref, seen_ref, *, tq, n_sel, e_base, e_step, q_base, q_step, coff, n_valid, padf):
    i = pl.program_id(1)
    ck = DSA_CK
    rbk = DSA_RB
    nt = ck // LANE
    e_end = e_base + e_step * i
    n_chunks = (e_end + ck - 1) // ck
    qpos = q_base + q_step * i + lax.broadcasted_iota(I32, (tq, 1), 0)
    bound = jnp.minimum(((((qpos + coff) >> CHUNK_SHIFT) + 1) << CHUNK_SHIFT) - coff, n_valid)
    lane_rb = lax.broadcasted_iota(I32, (rbk, ck), 1)
    lane128 = lax.broadcasted_iota(I32, (tq, LANE), 1)

    def key_rows(c):
        kpos0 = e_end - ck * (c + 1)
        return kpos0, pl.multiple_of(jnp.maximum(kpos0 + padf, 0), LANE)

    def tiles(x):
        return [x[:, t * LANE:(t + 1) * LANE] for t in range(nt)]

    qstack = jnp.concatenate([qi_ref[:, h * LANE:(h + 1) * LANE] for h in range(IDX_HEADS)], axis=0)
    for h in range(IDX_HEADS):
        wbc_ref[h] = jnp.broadcast_to(wi_ref[:, h:h + 1], (tq, LANE))
    cmax_ref[...] = jnp.full((tq, ck), -jnp.inf, F32)

    def idx_dot(c, dst):
        _, r = key_rows(c)
        s_ref[dst] = _nt(qstack, ki_ref[pl.ds(r, ck), :])

    def idx_keys(c, src):
        kpos0, _ = key_rows(c)
        for b in range(tq // rbk):
            r0 = b * rbk
            sc = None
            for h in range(IDX_HEADS):
                sh = jnp.maximum(s_ref[src, h * tq + r0:h * tq + r0 + rbk, :], 0.0)
                term = jnp.concatenate([wbc_ref[h, r0:r0 + rbk, :]] * nt, axis=1) * sh
                sc = term if sc is None else sc + term
            kpos = kpos0 + lane_rb
            allowed = (kpos >= 0) & (kpos < bound[r0:r0 + rbk])
            keys_ref[c, r0:r0 + rbk, :] = jnp.where(allowed, _sort_key(sc), INT_MIN)
            cmax_ref[r0:r0 + rbk, :] = jnp.maximum(cmax_ref[r0:r0 + rbk, :], jnp.where(allowed, sc, -jnp.inf))

    idx_dot(0, 0)

    def p1(j, carry):
        c = 2 * j
        idx_dot(c + 1, 1)
        idx_keys(c, 0)
        idx_dot(c + 2, 0)
        idx_keys(c + 1, 1)
        return carry

    lax.fori_loop(0, (n_chunks + 1) // 2, p1, 0)

    def count_ge(cand):
        def body(c, acc):
            u = keys_ref[c]
            for t in range(nt):
                acc = acc + jnp.where(u[:, t * LANE:(t + 1) * LANE] >= cand, 1.0, 0.0)
            return acc
        acc = lax.fori_loop(0, n_chunks, body, jnp.zeros((tq, LANE), F32))
        return jnp.sum(acc, axis=1, keepdims=True)

    cm = tiles(cmax_ref[...])
    if n_sel <= LANE:
        fold = [functools.reduce(jnp.maximum, cm)]
    elif n_sel <= 2 * LANE:
        fold = [jnp.maximum(cm[0], cm[2]), jnp.maximum(cm[1], cm[3])]
    else:
        fold = cm
    fmin = jnp.min(functools.reduce(jnp.minimum, fold), axis=1, keepdims=True)
    fmax = jnp.max(functools.reduce(jnp.maximum, cm), axis=1, keepdims=True)
    lo0 = _sort_key(fmin) - 1
    hi0 = _sort_key(fmax) + 2
    small = bound <= n_sel
    c_pos = count_ge(jnp.full((tq, 1), 1, I32))
    c_nn = count_ge(jnp.zeros((tq, 1), I32))
    pos_side = c_pos >= n_sel
    at_zero = jnp.logical_not(pos_side) & (c_nn >= n_sel)
    lo1 = jnp.where(pos_side, jnp.maximum(lo0, 1), lo0)
    top1 = jnp.where(pos_side, hi0, jnp.minimum(hi0, 0)) - 1
    fixed = small | at_zero
    nbits = jnp.where(fixed, 0, 32 - lax.clz(lo1 ^ top1))
    low_mask = (jnp.int32(1) << jnp.minimum(nbits, 31)) - 1
    t0 = jnp.where(small, INT_MIN + 1,
                   jnp.where(at_zero, 0, jnp.where(nbits >= 32, INT_MIN, lo1 & ~low_mask)))
    max_bits = jnp.max(nbits.astype(F32)).astype(I32)
    n_adm = bound.astype(F32)
    c_up0 = jnp.where(small, n_adm, jnp.where(at_zero, c_pos,
                                              jnp.where(pos_side | (hi0 <= 0), 0.0, c_nn)))
    c_at0 = jnp.where(small, n_adm, jnp.where(at_zero, c_nn, -1.0))

    def bis(it, st):
        t, c_at, c_up = st
        b = nbits - 1 - it
        cand = t + (jnp.int32(1) << jnp.maximum(b, 0))
        cnt = count_ge(cand)
        live = b >= 0
        take = live & (cnt >= n_sel)
        drop = live & (cnt < n_sel)
        return (jnp.where(take, cand, t), jnp.where(take, cnt, c_at), jnp.where(drop, cnt, c_up))

    thr, c_at, n_gt = lax.fori_loop(0, max_bits, bis, (t0, c_at0, c_up0))
    thr = jnp.maximum(thr, INT_MIN + 1)
    need = n_sel - n_gt
    excess = (c_at < 0.0) | ((c_at - n_gt) > need)

    cut_ref[...] = jnp.full((tq, LANE), 2 ** 30, I32)

    @pl.when(jnp.max(jnp.where(excess, 1.0, 0.0)) > 0.0)
    def _():
        rr = lax.broadcasted_iota(I32, (ck, ck), 0)
        cc = lax.broadcasted_iota(I32, (ck, ck), 1)
        upto = jnp.where(rr <= cc, 1.0, 0.0).astype(BF16)

        def tally(j, run):
            c = n_chunks - 1 - j
            seen_ref[c] = run
            u = keys_ref[c]
            for t in range(nt):
                run = run + jnp.where(u[:, t * LANE:(t + 1) * LANE] == thr, 1.0, 0.0)
            return run

        lax.fori_loop(0, n_chunks, tally, jnp.zeros((tq, LANE), F32))

        def locate(c, cut):
            kpos0, _ = key_rows(c)
            seen = jnp.sum(seen_ref[c], axis=1, keepdims=True)
            eq = jnp.where(keys_ref[c] == thr, 1.0, 0.0).astype(BF16)
            incl = jnp.dot(eq, upto, preferred_element_type=F32)
            below = jnp.sum(jnp.where(seen + incl < need, 1.0, 0.0), axis=1, keepdims=True)
            here = excess & (seen < need) & (seen + incl[:, ck - 1:ck] >= need)
            return jnp.where(here, kpos0 + below.astype(I32) + 1, cut)

        def locate4(j, cut):
            for k in range(4):
                cut = locate(jnp.minimum(4 * j + k, n_chunks - 1), cut)
            return cut

        cut = lax.fori_loop(0, (n_chunks + 3) // 4, locate4, jnp.full((tq, 1), 2 ** 30, I32))
        cut_ref[...] = jnp.broadcast_to(cut, (tq, LANE))

    cut = cut_ref[:, 0:1]

    r4 = A_HEADS // A_KV_HEADS
    qg = []
    for g in range(A_KV_HEADS):
        rows = []
        for j in range(r4):
            slot = qa_ref[:, j * LANE:(j + 1) * LANE]
            half = (lane128 < HEAD_DIM) if g == 0 else (lane128 >= HEAD_DIM)
            rows.append(jnp.where(half, slot, jnp.zeros_like(slot)))
        qg.append(jnp.concatenate(rows, axis=0))
    farb = [jnp.concatenate([jnp.full((tq, 1), rb_ref[g * r4 + j], F32) for j in range(r4)], axis=0)
            for g in range(A_KV_HEADS)]
    l_ref[...] = jnp.zeros(l_ref.shape, F32)
    acc_ref[...] = jnp.zeros(acc_ref.shape, F32)

    def logits_dot(c, g):
        _, r = key_rows(c)
        s_ref[g] = _nt(qg[g], ka_ref[pl.ds(r, ck), :])

    def select_mask(c):
        kpos0, _ = key_rows(c)
        for b in range(tq // rbk):
            r0 = b * rbk
            u = keys_ref[c, r0:r0 + rbk, :]
            t_b = thr[r0:r0 + rbk]
            sel = (u > t_b) | ((u == t_b) & (kpos0 + lane_rb < cut[r0:r0 + rbk]))
            madd_ref[r0:r0 + rbk, :] = jnp.where(sel, 0.0, NEG)

    def softmax_passes(g, m_old, near):
        m_rows = []
        for b in range(r4 * tq // rbk):
            r0 = b * rbk
            q0 = r0 % tq
            sm = s_ref[g, r0:r0 + rbk, :] + madd_ref[q0:q0 + rbk, :]
            if near:
                sm = sm + bn_ref[g, r0:r0 + rbk, :]
            m_blk = jnp.max(functools.reduce(jnp.maximum, tiles(sm)), axis=1, keepdims=True)
            far_b = farb[g][r0:r0 + rbk]
            if not near:
                m_blk = m_blk + far_b
            m_prev = m_old[r0:r0 + rbk]
            m_new = jnp.maximum(m_prev, m_blk)
            alpha = jnp.exp2(m_prev - m_new)
            alpha_ref[g, r0:r0 + rbk, :] = jnp.broadcast_to(alpha, (rbk, LANE))
            p = jnp.exp2(sm - (m_new if near else m_new - far_b))
            l_ref[g, r0:r0 + rbk, :] = alpha * l_ref[g, r0:r0 + rbk, :] + functools.reduce(jnp.add, tiles(p))
            p_ref[g, r0:r0 + rbk, :] = p.astype(BF16)
            m_rows.append(m_new)
        return jnp.concatenate(m_rows, axis=0)

    def value_dot(c, g):
        _, r = key_rows(c)
        acc_ref[g] = alpha_ref[g] * acc_ref[g] + jnp.dot(p_ref[g], va_ref[pl.ds(r, ck), :],
                                                        preferred_element_type=F32)

    m0 = jnp.full((r4 * tq, 1), NEG, F32)
    logits_dot(0, 0)
    logits_dot(0, 1)
    select_mask(0)
    m0n = softmax_passes(0, m0, True)
    value_dot(0, 0)
    logits_dot(1, 0)
    m1n = softmax_passes(1, m0, True)

    def p3(c, ms):
        value_dot(c - 1, 1)
        logits_dot(c, 1)
        select_mask(c)
        m_a = softmax_passes(0, ms[0], False)
        value_dot(c, 0)
        logits_dot(c + 1, 0)
        m_b = softmax_passes(1, ms[1], False)
        return (m_a, m_b)

    lax.fori_loop(1, n_chunks, p3, (m0n, m1n))
    value_dot(n_chunks - 1, 1)
    outs = [acc_ref[g] / jnp.sum(l_ref[g], axis=1, keepdims=True) for g in range(A_KV_HEADS)]
    for j in range(r4):
        lo = outs[0][j * tq:(j + 1) * tq]
        hi = outs[1][j * tq:(j + 1) * tq]
        o_ref[:, j * LANE:(j + 1) * LANE] = jnp.where(lane128 < HEAD_DIM, lo, hi).astype(BF16)


def _dsa(rb_far, qi, wi, qa, ki, ka, va, bn, *, nb, nq, tq, row0, n_sel, e_base, e_step,
         q_base, q_step, coff, n_valid, padf, n_chunks_max):
    rb0 = row0 // tq
    r4 = A_HEADS // A_KV_HEADS
    assert IDX_HEADS == r4 and tq % DSA_RB == 0
    qmap = lambda b, i: (rb0 + b * nq + i, 0)
    kmap = lambda b, i: (b, 0, 0)
    kern = functools.partial(_dsa_kernel, tq=tq, n_sel=n_sel, e_base=e_base, e_step=e_step,
                             q_base=q_base, q_step=q_step, coff=coff, n_valid=n_valid, padf=padf)
    return pl.pallas_call(
        kern,
        grid=(nb, nq),
        in_specs=[pl.BlockSpec(memory_space=pltpu.SMEM),
                  pl.BlockSpec((tq, qi.shape[1]), qmap),
                  pl.BlockSpec((tq, LANE), qmap),
                  pl.BlockSpec((tq, qa.shape[1]), qmap),
                  pl.BlockSpec((None,) + ki.shape[1:], kmap),
                  pl.BlockSpec((None,) + ka.shape[1:], kmap),
                  pl.BlockSpec((None,) + va.shape[1:], kmap),
                  pl.BlockSpec(bn.shape, lambda b, i: (0, 0, 0))],
        out_specs=pl.BlockSpec((tq, qa.shape[1]), lambda b, i: (b * nq + i, 0)),
        out_shape=jax.ShapeDtypeStruct((nb * nq * tq, qa.shape[1]), BF16),
        scratch_shapes=[pltpu.VMEM((n_chunks_max + 1, tq, DSA_CK), I32),
                        pltpu.VMEM((tq, LANE), I32),
                        pltpu.VMEM((2, r4 * tq, DSA_CK), F32),
                        pltpu.VMEM((A_KV_HEADS, r4 * tq, DSA_CK), BF16),
                        pltpu.VMEM((tq, DSA_CK), F32),
                        pltpu.VMEM((IDX_HEADS, tq, LANE), F32),
                        pltpu.VMEM((tq, DSA_CK), F32),
                        pltpu.VMEM((A_KV_HEADS, r4 * tq, LANE), F32),
                        pltpu.VMEM((A_KV_HEADS, r4 * tq, LANE), F32),
                        pltpu.VMEM((A_KV_HEADS, r4 * tq, LANE), F32),
                        pltpu.VMEM((n_chunks_max, tq, LANE), F32)],
        compiler_params=_cparams(("parallel", "arbitrary")),
        name="dsa",
    )(rb_far, qi, wi, qa, ki, ka, va, bn)


def _sb_kernel(tri_ref, q_ref, k_ref, v_ref, *rest, tq, e_base, e_step, q_base, q_step, padf, n_valid,
               newest_separate):
    if newest_separate:
        k0_ref, v0_ref, o_ref = rest
    else:
        (o_ref,) = rest
    i = pl.program_id(2)
    bw = SB_BAND
    npp = q_ref.shape[1] // LANE
    nh = 2 * npp
    e_end = e_base + e_step * i
    n_bands = (e_end + bw - 1) // bw
    qpos = q_base + q_step * i + lax.broadcasted_iota(I32, (tq, 1), 0)
    lane = lax.broadcasted_iota(I32, (tq, LANE), 1)
    qh = []
    for p in range(npp):
        q = q_ref[:, p * LANE:(p + 1) * LANE]
        zero = jnp.zeros_like(q)
        qh += [jnp.where(lane < HEAD_DIM, q, zero), jnp.where(lane >= HEAD_DIM, q, zero)]
    lane_bw = lax.broadcasted_iota(I32, (tq, bw), 1)

    def cond(st):
        return (st[0] < n_bands) & (st[1] > SB_STOP)

    def body(st, newest=False):
        m, _, carry, acc = st
        kpos0 = e_end - bw * (m + 1)
        r = pl.multiple_of(kpos0 + padf, LANE)
        kpos = kpos0 + lane_bw
        before = (kpos < qpos) & (kpos >= 0) & (kpos < n_valid)

        def band(ref0, ref, p):
            cols = slice(p * LANE, (p + 1) * LANE)
            if newest and newest_separate:
                return ref0[:, cols]
            return ref[pl.ds(r, bw), cols].astype(BF16)

        zs, lss, lks, parts = [], [], [], []
        for h in range(nh):
            kt = band(k0_ref if newest_separate else None, k_ref, h // 2)
            z = _nt(qh[h], kt)
            ls = -(jnp.maximum(z, 0.0) + jnp.log1p(jnp.exp(-jnp.abs(z))))
            lk = jnp.where(before, ls, 0.0)
            hi = lk.astype(BF16)
            parts += [hi, (lk - hi.astype(F32)).astype(BF16)]
            zs.append(z)
            lss.append(ls)
            lks.append(lk)
        sums = jnp.dot(jnp.concatenate(parts, axis=0), tri_ref[...], preferred_element_type=F32)
        new_c, new_a = [], []
        worst = jnp.float32(-jnp.inf)
        for h in range(nh):
            vt = band(v0_ref if newest_separate else None, v_ref, h // 2)
            bl = sums[2 * h * tq:(2 * h + 1) * tq] + sums[(2 * h + 1) * tq:(2 * h + 2) * tq]
            w = jnp.where(before, jnp.exp(lss[h] + zs[h] + bl + carry[h]), 0.0)
            new_a.append(acc[h] + jnp.dot(w.astype(BF16), vt, preferred_element_type=F32))
            c_n = carry[h] + bl[:, 0:1] + lks[h][:, 0:1]
            new_c.append(c_n)
            worst = jnp.maximum(worst, jnp.max(c_n))
        return (m + 1, worst, tuple(new_c), tuple(new_a))

    init = (jnp.int32(0), jnp.float32(0.0),
            tuple(jnp.zeros((tq, 1), F32) for _ in range(nh)),
            tuple(jnp.zeros((tq, LANE), F32) for _ in range(nh)))
    _, _, _, acc = lax.while_loop(cond, body, body(init, newest=True))
    for p in range(npp):
        o_ref[:, p * LANE:(p + 1) * LANE] = jnp.where(lane < HEAD_DIM, acc[2 * p], acc[2 * p + 1]).astype(BF16)


def _stick_break(q, k, v, newest=None, *, nb, nq, tq, row0, e_base, e_step, q_base, q_step, padf, n_valid):
    rb0 = row0 // tq
    wq = SB_PAIRS * LANE
    npair = q.shape[1] // wq
    kern = functools.partial(_sb_kernel, tq=tq, e_base=e_base, e_step=e_step, q_base=q_base,
                             q_step=q_step, padf=padf, n_valid=n_valid, newest_separate=newest is not None)
    rr = lax.broadcasted_iota(I32, (SB_BAND, SB_BAND), 0)
    cc = lax.broadcasted_iota(I32, (SB_BAND, SB_BAND), 1)
    tri = jnp.where(rr > cc, 1.0, 0.0).astype(BF16)
    kspec = pl.BlockSpec((None, k.shape[1], wq), lambda b, p, i: (b, 0, p))
    in_specs = [pl.BlockSpec(tri.shape, lambda b, p, i: (0, 0)),
                pl.BlockSpec((tq, wq), lambda b, p, i: (rb0 + b * nq + i, p)), kspec, kspec]
    args = [tri, q, k, v]
    if newest is not None:
        in_specs += [pl.BlockSpec((None, SB_BAND, wq), lambda b, p, i: (b, 0, p))] * 2
        args += list(newest)
    return pl.pallas_call(
        kern,
        grid=(nb, npair, nq),
        in_specs=in_specs,
        out_specs=pl.BlockSpec((tq, wq), lambda b, p, i: (b * nq + i, p)),
        out_shape=jax.ShapeDtypeStruct((nb * nq * tq, q.shape[1]), BF16),
        compiler_params=_cparams(("parallel", "parallel", "arbitrary")),
        name="stick_break",
    )(*args)


def _swa_kernel(sink_ref, q_ref, k_ref, v_ref, bias_ref, o_ref, s_ref, p_ref, madd_ref,
                *, tq, ck, r_base, r_step, p_base, p_step, q_base, q_step, coff, n_valid):
    i = pl.program_id(1)
    r0 = pl.multiple_of(r_base + r_step * i, 16)
    kpos0 = p_base + p_step * i
    kt = k_ref[pl.ds(r0, ck), :]
    vt = v_ref[pl.ds(r0, ck), :]
    qpos = q_base + q_step * i + lax.broadcasted_iota(I32, (tq, 1), 0)
    kpos = kpos0 + lax.broadcasted_iota(I32, (tq, ck), 1)
    qc = (qpos + coff) >> CHUNK_SHIFT
    hi_b = jnp.minimum(((qc + 1) << CHUNK_SHIFT) - coff, n_valid)
    lo_b = jnp.maximum(((qc - WIN_CHUNKS) << CHUNK_SHIFT) - coff, 0)
    r8 = C_HEADS // C_KV_HEADS
    rbk = DSA_RB
    nt = ck // LANE
    madd_ref[...] = jnp.where((kpos >= lo_b) & (kpos < hi_b), 0.0, NEG)
    lane = lax.broadcasted_iota(I32, (tq, LANE), 1)

    def tiles(x):
        return [x[:, t * LANE:(t + 1) * LANE] for t in range(nt)]

    outs = []
    for g in range(C_KV_HEADS):
        rows = []
        for j in range(r8):
            slot = q_ref[:, j * LANE:(j + 1) * LANE]
            half = (lane < HEAD_DIM) if g == 0 else (lane >= HEAD_DIM)
            rows.append(jnp.where(half, slot, jnp.zeros_like(slot)))
        s_ref[g] = _nt(jnp.concatenate(rows, axis=0), kt)
        dens = []
        for b in range(r8 * tq // rbk):
            rs = slice(b * rbk, (b + 1) * rbk)
            q0 = (b * rbk) % tq
            sink = jnp.full((rbk, 1), sink_ref[g * r8 + (b * rbk) // tq], F32)
            sm = s_ref[g, rs, :] + bias_ref[g, rs, :] + madd_ref[q0:q0 + rbk, :]
            m = jnp.maximum(jnp.max(functools.reduce(jnp.maximum, tiles(sm)), axis=1, keepdims=True), sink)
            e = jnp.exp2(sm - m)
            dens.append(jnp.sum(functools.reduce(jnp.add, tiles(e)), axis=1, keepdims=True)
                        + jnp.exp2(sink - m))
            p_ref[g, rs, :] = e.astype(BF16)
        outs.append(jnp.dot(p_ref[g], vt, preferred_element_type=F32) / jnp.concatenate(dens, axis=0))
    for j in range(r8):
        lo = outs[0][j * tq:(j + 1) * tq]
        hi = outs[1][j * tq:(j + 1) * tq]
        o_ref[:, j * LANE:(j + 1) * LANE] = jnp.where(lane < HEAD_DIM, lo, hi).astype(BF16)


def _swa(sinks, q, k, v, bias, *, nb, nq, tq, row0, ck, r_base, r_step, p_base, p_step,
         q_base, q_step, coff, n_valid):
    rb0 = row0 // tq
    r8 = C_HEADS // C_KV_HEADS
    assert tq % DSA_RB == 0 and ck % LANE == 0
    kern = functools.partial(_swa_kernel, tq=tq, ck=ck, r_base=r_base, r_step=r_step, p_base=p_base,
                             p_step=p_step, q_base=q_base, q_step=q_step, coff=coff, n_valid=n_valid)
    kmap = lambda b, i: (b, 0, 0)
    return pl.pallas_call(
        kern,
        grid=(nb, nq),
        in_specs=[pl.BlockSpec(memory_space=pltpu.SMEM),
                  pl.BlockSpec((tq, q.shape[1]), lambda b, i: (rb0 + b * nq + i, 0)),
                  pl.BlockSpec((None,) + k.shape[1:], kmap),
                  pl.BlockSpec((None,) + v.shape[1:], kmap),
                  pl.BlockSpec(bias.shape, lambda b, i: (0, 0, 0))],
        out_specs=pl.BlockSpec((tq, q.shape[1]), lambda b, i: (b * nq + i, 0)),
        out_shape=jax.ShapeDtypeStruct((nb * nq * tq, q.shape[1]), BF16),
        scratch_shapes=[pltpu.VMEM((C_KV_HEADS, r8 * tq, ck), F32),
                        pltpu.VMEM((C_KV_HEADS, r8 * tq, ck), BF16),
                        pltpu.VMEM((tq, ck), F32)],
        compiler_params=_cparams(("parallel", "arbitrary")),
        name="swa",
    )(sinks, q, k, v, bias)


def _router_kernel(y_ref, rh_ref, rl_ref, idx_ref, gate_ref):
    y = y_ref[...]
    yh = y.astype(BF16)
    yl = (y - yh.astype(F32)).astype(BF16)
    rh = rh_ref[...]
    logits = (jnp.dot(yh, rh, preferred_element_type=F32) + jnp.dot(yl, rh, preferred_element_type=F32)
              + jnp.dot(yh, rl_ref[...], preferred_element_type=F32))
    lane = lax.broadcasted_iota(I32, logits.shape, 1)
    logits = jnp.where(lane < N_EXPERTS, logits, -jnp.inf)
    m1 = jnp.max(logits, axis=1, keepdims=True)
    i1 = jnp.min(jnp.where(logits == m1, lane, LANE), axis=1, keepdims=True)
    rest = jnp.where(lane == i1, -jnp.inf, logits)
    m2 = jnp.max(rest, axis=1, keepdims=True)
    i2 = jnp.min(jnp.where(rest == m2, lane, LANE), axis=1, keepdims=True)
    e2 = jnp.exp(m2 - m1)
    den = 1.0 + e2
    idx_ref[...] = jnp.where(lane == 0, i1, jnp.where(lane == 1, i2, 0))
    gate_ref[...] = jnp.where(lane == 0, 1.0 / den, jnp.where(lane == 1, e2 / den, 0.0))


def _router(y, router, tm):
    n, d = y.shape
    rpad = jnp.pad(router.astype(F32), ((0, 0), (0, LANE - router.shape[1])))
    rh = rpad.astype(BF16)
    rl = (rpad - rh.astype(F32)).astype(BF16)
    return pl.pallas_call(
        _router_kernel,
        grid=(pl.cdiv(n, tm),),
        in_specs=[pl.BlockSpec((tm, d), lambda i: (i, 0)),
                  pl.BlockSpec((d, LANE), lambda i: (0, 0)), pl.BlockSpec((d, LANE), lambda i: (0, 0))],
        out_specs=[pl.BlockSpec((tm, LANE), lambda i: (i, 0)), pl.BlockSpec((tm, LANE), lambda i: (i, 0))],
        out_shape=[jax.ShapeDtypeStruct((n, LANE), I32), jax.ShapeDtypeStruct((n, LANE), F32)],
        compiler_params=_cparams(("parallel",)),
        name="router",
    )(y, rh, rl)


def _moe_kernel(be_ref, nu_ref, x_ref, wg_ref, wu_ref, wd_ref, o_ref, acc_ref, xb_ref):
    t = pl.program_id(0)
    f = pl.program_id(1)

    @pl.when(f == 0)
    def _():
        acc_ref[...] = jnp.zeros_like(acc_ref)
        xb_ref[...] = x_ref[...].astype(BF16)

    @pl.when(t < nu_ref[0])
    def _():
        xb = xb_ref[...]
        hg = jnp.dot(xb, wg_ref[...].astype(BF16), preferred_element_type=F32)
        hu = jnp.dot(xb, wu_ref[...].astype(BF16), preferred_element_type=F32)
        h = (hg * jax.nn.sigmoid(hg) * hu).astype(BF16)
        acc_ref[...] += jnp.dot(h, wd_ref[...].astype(BF16), preferred_element_type=F32)

    @pl.when(f == pl.num_programs(1) - 1)
    def _():
        o_ref[...] = acc_ref[...]


def _moe_experts(block_expert, n_used, xs, wg, wu, wd, tmb, tf):
    n_slots, d = xs.shape
    de = wg.shape[2]
    grid_spec = pltpu.PrefetchScalarGridSpec(
        num_scalar_prefetch=2,
        grid=(n_slots // tmb, de // tf),
        in_specs=[pl.BlockSpec((tmb, d), lambda t, f, be, nu: (t, 0)),
                  pl.BlockSpec((None, d, tf), lambda t, f, be, nu: (be[t], 0, f)),
                  pl.BlockSpec((None, d, tf), lambda t, f, be, nu: (be[t], 0, f)),
                  pl.BlockSpec((None, tf, d), lambda t, f, be, nu: (be[t], f, 0))],
        out_specs=pl.BlockSpec((tmb, d), lambda t, f, be, nu: (t, 0)),
        scratch_shapes=[pltpu.VMEM((tmb, d), F32), pltpu.VMEM((tmb, d), BF16)],
    )
    return pl.pallas_call(
        _moe_kernel,
        grid_spec=grid_spec,
        out_shape=jax.ShapeDtypeStruct((n_slots, d), F32),
        compiler_params=_cparams(("parallel", "arbitrary")),
        name="moe_experts",
    )(block_expert, n_used, xs, wg, wu, wd)


def _row_copy(src_hbm, src_row, dst, dst_row, sem):
    return pltpu.make_async_copy(src_hbm.at[pl.ds(src_row, 1)], dst.at[pl.ds(dst_row, 1)], sem)


def _dispatch_kernel(slot_ref, x_ref, init_hbm, xs_hbm, sem, *, tt):
    del init_hbm

    def start(j, carry):
        for k in range(TOP_K):
            _row_copy(x_ref, j, xs_hbm, slot_ref[0, 0, TOP_K * j + k], sem).start(priority=k % 2)
        return carry

    def wait(j, carry):
        _row_copy(x_ref, 0, xs_hbm, 0, sem).wait()
        return carry

    lax.fori_loop(0, tt, start, 0, unroll=8)
    lax.fori_loop(0, TOP_K * tt, wait, 0, unroll=8)


def _dispatch(x, slot3, n_slots, tt):
    n, d = x.shape
    init = jnp.zeros((n_slots, d), x.dtype)
    return pl.pallas_call(
        functools.partial(_dispatch_kernel, tt=tt),
        grid=(n // tt,),
        in_specs=[pl.BlockSpec((1, 1, TOP_K * tt), lambda i: (i, 0, 0), memory_space=pltpu.SMEM),
                  pl.BlockSpec((tt, d), lambda i: (i, 0)), pl.BlockSpec(memory_space=pl.ANY)],
        out_specs=pl.BlockSpec(memory_space=pl.ANY),
        out_shape=jax.ShapeDtypeStruct((n_slots, d), x.dtype),
        scratch_shapes=[pltpu.SemaphoreType.DMA],
        input_output_aliases={2: 0},
        compiler_params=_cparams(("arbitrary",)),
        name="moe_dispatch",
    )(slot3, x, init)


def _combine_ln_kernel(slot_ref, y_ref, gate_ref, g_ref, b_ref, outs_hbm, o_ref, buf_ref, sem, *, tt):
    def start(j, carry):
        for k in range(TOP_K):
            _row_copy(outs_hbm, slot_ref[0, 0, TOP_K * j + k], buf_ref.at[k], j, sem).start(priority=k % 2)
        return carry

    def wait(j, carry):
        _row_copy(outs_hbm, 0, buf_ref.at[0], 0, sem).wait()
        return carry

    lax.fori_loop(0, tt, start, 0, unroll=8)
    lax.fori_loop(0, TOP_K * tt, wait, 0, unroll=8)
    mo = gate_ref[:, 0:1] * buf_ref[0]
    for k in range(1, TOP_K):
        mo = mo + gate_ref[:, k:k + 1] * buf_ref[k]
    o_ref[...] = _layer_norm(DN_ALPHA * y_ref[...] + mo, g_ref[...], b_ref[...])


def _combine_ln(y, outs, slot3, gate_p, g, b, tt):
    n, d = y.shape
    row = pl.BlockSpec((tt, d), lambda i: (i, 0))
    vec = pl.BlockSpec((1, d), lambda i: (0, 0))
    return pl.pallas_call(
        functools.partial(_combine_ln_kernel, tt=tt),
        grid=(n // tt,),
        in_specs=[pl.BlockSpec((1, 1, TOP_K * tt), lambda i: (i, 0, 0), memory_space=pltpu.SMEM),
                  row, pl.BlockSpec((tt, LANE), lambda i: (i, 0)), vec, vec,
                  pl.BlockSpec(memory_space=pl.ANY)],
        out_specs=row,
        out_shape=jax.ShapeDtypeStruct((n, d), F32),
        scratch_shapes=[pltpu.VMEM((TOP_K, tt, d), F32), pltpu.SemaphoreType.DMA],
        compiler_params=_cparams(("arbitrary",)),
        name="moe_combine_ln",
    )(slot3, y, gate_p, g.reshape(1, d), b.reshape(1, d), outs)


def _moe(y, router, w_gate, w_up, w_down, g, b, tm, tmb, tf, tt):
    n, d = y.shape
    assert n % tt == 0
    idx_p, gate_p = _router(y, router, tm)
    e_flat = idx_p[:, :TOP_K].reshape(-1)
    onehot = (e_flat[:, None] == jnp.arange(N_EXPERTS, dtype=I32)[None, :]).astype(I32)
    rank = jnp.sum((jnp.cumsum(onehot, axis=0) - onehot) * onehot, axis=1)
    counts = jnp.sum(onehot, axis=0)
    padded = (counts + tmb - 1) // tmb * tmb
    pad_end = jnp.cumsum(padded)
    slot = ((pad_end - padded)[e_flat] + rank).astype(I32)
    n_blocks = -(-(TOP_K * n) // tmb) + N_EXPERTS
    n_slots = n_blocks * tmb
    block_expert = jnp.minimum(
        jnp.searchsorted(pad_end, jnp.arange(n_blocks, dtype=I32) * tmb, side="right"),
        N_EXPERTS - 1).astype(I32)
    n_used = (pad_end[-1] // tmb).astype(I32).reshape(1)
    slot3 = slot.reshape(n // tt, 1, TOP_K * tt)
    xs = _dispatch(y, slot3, n_slots, tt)
    outs = _moe_experts(block_expert, n_used, xs, w_gate, w_up, w_down, tmb, tf)
    return _combine_ln(y, outs, slot3, gate_p, g, b, tt)


def _t5_bucket(rel):
    half = NUM_BUCKETS // 2
    exact = half // 2
    n = jnp.abs(rel)
    far = exact + sum((n >= t).astype(I32) for t in (12, 16, 23, 32, 46, 64, 91))
    return jnp.where(rel > 0, half, 0) + jnp.where(n < exact, n, far)


def _bias_tile(rel_bias, n_groups, per_group, tq, ck, d0):
    d = d0 + jnp.arange(ck, dtype=I32)[None, :] - jnp.arange(tq, dtype=I32)[:, None]
    onehot = (_t5_bucket(d)[..., None] == jnp.arange(NUM_BUCKETS, dtype=I32)).astype(F32)
    tile = jnp.einsum("qkb,bh->qkh", onehot, rel_bias.astype(F32),
                      precision=lax.Precision.HIGHEST)
    tile = jnp.transpose(tile[:, :, :n_groups * per_group], (2, 0, 1))
    return tile.reshape(n_groups, per_group * tq, ck)


def _cols(w, ranges):
    parts = []
    for r in ranges:
        if isinstance(r, int):
            parts.append(jnp.zeros((w.shape[0], r), w.dtype))
        else:
            parts.append(w[:, r[0]:r[1]])
    return jnp.concatenate(parts, axis=1)


def _head_pair_order(n_heads):
    half = n_heads // 2
    order = []
    for j in range(half):
        order += [j, half + j]
    return order


L0_SPECS = ((0, 512, ("bf16e",)), (512, 128, ("f32", "bf16")), (640, 128, ("f32", "bf16")),
            (768, 512, ("hilo",)), (1280, 128, ("f32", "bf16")), (1408, 128, ("wi",)),
            (1536, 512, ("bf16s",)), (2048, 512, ("f32", "bf16")), (2560, 512, ("f32", "bf16")))
L1_SPECS = ((0, 1024, ("bf16e",)), (1024, 128, ("f32", "bf16")), (1152, 128, ("f32", "bf16")))


def _l0_weight(w_in):
    hd = HEAD_DIM
    rng = [(hd * h, hd * h + hd) for h in _head_pair_order(A_HEADS)]
    rng += [(512, 640), (640, 768)]
    for h in range(IDX_HEADS):
        rng += [(768 + hd * h, 768 + hd * h + hd)] * 2
    rng += [(1024, 1088)] * 2
    rng += [(1088, 1092), LANE - IDX_HEADS]
    rng += [(1092, 1604), (1604, 2116), (2116, 2628)]
    return _cols(w_in, rng).astype(BF16)


def _l1_weight(w_in):
    hd = HEAD_DIM
    rng = [(hd * h, hd * h + hd) for h in _head_pair_order(C_HEADS)]
    rng += [(1024, 1152), (1152, 1280)]
    return _cols(w_in, rng).astype(BF16)


def _perm_rows(w, n_heads):
    return jnp.concatenate([w[HEAD_DIM * h:HEAD_DIM * (h + 1)] for h in _head_pair_order(n_heads)], axis=0)


def _front_pad(a, nb, rows_in, front, rows_out):
    a = a.reshape(nb, rows_in, a.shape[-1])
    return jnp.pad(a, ((0, 0), (front, rows_out - front - rows_in), (0, 0)))


def _layer0_attention(x_prompt, x_sample, cache_a_k, cache_a_v, cache_a_idx_k, cache_b_k, cache_b_v,
                      meta_tokens, rel_bias, l0_w_in):
    nb, seq, d = x_prompt.shape
    t = N_META + seq
    nq = -(-t // QBLK)
    tp = nq * QBLK
    db, ds, _ = x_sample.shape
    past = cache_a_k.shape[1]
    assert d == D_MODEL and past % LANE == 0 and ds % DSA_RB == 0 and ds <= 64
    np_, ns = nb * tp, db * ds
    tm = 512

    meta = jnp.broadcast_to(meta_tokens[None].astype(x_prompt.dtype), (nb, N_META, d))
    hp = jnp.concatenate([meta, x_prompt, jnp.zeros((nb, tp - t, d), x_prompt.dtype)], axis=1)
    x_all = jnp.concatenate([hp.reshape(np_, d), x_sample.reshape(ns, d)], axis=0)
    rel_bias = rel_bias.astype(F32)
    rb_far = rel_bias[NUM_BUCKETS // 2 - 1] * LOG2E

    (qa16, ka32, ka16, va32, va16, qi16, ki32, ki16, wi32, qb16, kb32, kb16, vb32, vb16) = _project(
        x_all, _l0_weight(l0_w_in), L0_SPECS, tm)

    n_sel_p = min(TOPK_MAX, (t - N_META) // 4)
    kr = QBLK * (nq + 1) + DSA_PADF
    ncm = -(-(QBLK * (nq + 1)) // DSA_CK)
    pk = [_front_pad(a[:np_], nb, tp, DSA_PADF, kr) for a in (ki16, ka16, va16)]
    bn_p = _bias_tile(rel_bias, A_KV_HEADS, A_HEADS // A_KV_HEADS, QBLK, DSA_CK, -(DSA_CK // 2)) * LOG2E
    oa_p = _dsa(rb_far, qi16, wi32, qa16, *pk, bn_p, nb=nb, nq=nq, tq=QBLK, row0=0, n_sel=n_sel_p,
                e_base=2 * QBLK, e_step=QBLK, q_base=0, q_step=QBLK, coff=PROMPT_COFF,
                n_valid=t, padf=DSA_PADF, n_chunks_max=ncm)

    n_keys = past + ds
    n_sel_s = min(TOPK_MAX, n_keys // 4)
    e_s = -(-n_keys // LANE) * LANE
    ncs = -(-e_s // DSA_CK)
    krs = ncs * DSA_CK
    fs = krs - e_s

    def cat_keys(cache, new, width_dup):
        c = cache.reshape(db, past, -1).astype(BF16)
        if width_dup:
            c = jnp.concatenate([c, c], axis=-1)
        a = jnp.concatenate([c, new[np_:].reshape(db, ds, -1)], axis=1)
        return jnp.pad(a, ((0, 0), (fs, krs - fs - n_keys), (0, 0)))

    sk = [cat_keys(cache_a_idx_k, ki16, True), cat_keys(cache_a_k, ka16, False),
          cat_keys(cache_a_v, va16, False)]
    bn_s = _bias_tile(rel_bias, A_KV_HEADS, A_HEADS // A_KV_HEADS, ds, DSA_CK, e_s - DSA_CK - past) * LOG2E
    oa_s = _dsa(rb_far, qi16, wi32, qa16, *sk, bn_s, nb=db, nq=1, tq=ds, row0=np_, n_sel=n_sel_s,
                e_base=e_s, e_step=0, q_base=past, q_step=0, coff=0, n_valid=n_keys, padf=fs,
                n_chunks_max=ncs)

    pfb = SB_BAND - QBLK
    kb_p = _front_pad(kb16[:np_], nb, tp, pfb, tp + pfb)
    vb_p = _front_pad(vb16[:np_], nb, tp, pfb, tp + pfb)
    ob_p = _stick_break(qb16, kb_p, vb_p, nb=nb, nq=nq, tq=QBLK, row0=0, e_base=QBLK, e_step=QBLK,
                        q_base=0, q_step=QBLK, padf=pfb, n_valid=t)
    assert past % SB_BAND == 0 and ds <= SB_BAND

    def newest_band(new):
        return jnp.pad(new[np_:].reshape(db, ds, -1), ((0, 0), (0, SB_BAND - ds), (0, 0)))

    ob_s = _stick_break(qb16, cache_b_k.reshape(db, past, -1), cache_b_v.reshape(db, past, -1),
                        (newest_band(kb16), newest_band(vb16)), nb=db, nq=1, tq=ds, row0=np_,
                        e_base=past + SB_BAND, e_step=0, q_base=past, q_step=0, padf=0, n_valid=n_keys)

    oa = jnp.concatenate([oa_p, oa_s], axis=0)
    ob = jnp.concatenate([ob_p, ob_s], axis=0)
    return dict(x_all=x_all, oa=oa, ob=ob, np=np_, tp=tp, t=t, nq=nq, rel_bias=rel_bias,
                ka32=ka32, va32=va32, ki32=ki32, kb32=kb32, vb32=vb32)


def kernel(x_prompt, x_sample, cache_a_k, cache_a_v, cache_a_idx_k, cache_b_k, cache_b_v, cache_c_k, cache_c_v, meta_tokens, rel_bias, l0_w_in, l0_w_out, l0_ln1_g, l0_ln1_b, l0_w_gate, l0_w_up, l0_w_down, l0_ln2_g, l0_ln2_b, l1_w_in, l1_sinks, l1_w_out, l1_ln1_g, l1_ln1_b, l1_router, l1_w_gate, l1_w_up, l1_w_down, l1_ln2_g, l1_ln2_b):
    a0 = _layer0_attention(x_prompt, x_sample, cache_a_k, cache_a_v, cache_a_idx_k, cache_b_k, cache_b_v,
                           meta_tokens, rel_bias, l0_w_in)
    x_all, oa, ob, np_, tp, t, nq, rel_bias = (a0[k] for k in ("x_all", "oa", "ob", "np", "tp", "t", "nq", "rel_bias"))
    ka32, va32, ki32, kb32, vb32 = (a0[k] for k in ("ka32", "va32", "ki32", "kb32", "vb32"))
    nb, _, d = x_prompt.shape
    db, ds, _ = x_sample.shape
    past = cache_a_k.shape[1]
    tm = 512
    w_out0 = l0_w_out.astype(BF16)
    half0 = A_HEADS * HEAD_DIM
    y0 = _mix_ln(x_all, [oa, ob], [_perm_rows(w_out0[:half0], A_HEADS), w_out0[half0:]],
                 l0_ln1_g, l0_ln1_b, tm)
    h1 = _ffn_ln(y0, l0_w_gate, l0_w_up, l0_w_down, l0_ln2_g, l0_ln2_b, tm, 256)

    oc, kc32, vc32 = _layer1_attention(h1, cache_c_k, cache_c_v, rel_bias, l1_w_in, l1_sinks,
                                       nb=nb, nq=nq, t=t, db=db, ds=ds, past=past, tm=tm)
    y1 = _mix_ln(h1, [oc], [_perm_rows(l1_w_out.astype(BF16), C_HEADS)], l1_ln1_g, l1_ln1_b, tm)
    n_all = y1.shape[0]
    tt = next(c for c in (256, 128, 64, 32, 16, 8) if n_all % c == 0)
    h2 = _moe(y1, l1_router, l1_w_gate, l1_w_up, l1_w_down, l1_ln2_g, l1_ln2_b, tm, 1024, 512, tt)
    return _assemble(h2, ka32, va32, ki32, kb32, vb32, kc32, vc32, cache_c_k, cache_c_v,
                     nb=nb, tp=tp, t=t, db=db, ds=ds)


def _layer1_attention(h1, cache_c_k, cache_c_v, rel_bias, l1_w_in, l1_sinks, *, nb, nq, t, db, ds, past, tm):
    tp = nq * QBLK
    np_ = nb * tp
    qc16, kc32, kc16, vc32, vc16 = _project(h1, _l1_weight(l1_w_in), L1_SPECS, tm)
    ckp = 4 * QBLK
    krc = QBLK * (nq - 1) + ckp
    kc_p = _front_pad(kc16[:np_], nb, tp, 2 * QBLK, max(krc, tp + 2 * QBLK))
    vc_p = _front_pad(vc16[:np_], nb, tp, 2 * QBLK, max(krc, tp + 2 * QBLK))
    bc_p = _bias_tile(rel_bias, C_KV_HEADS, C_HEADS // C_KV_HEADS, QBLK, ckp, -2 * QBLK) * LOG2E
    sinks = l1_sinks.astype(F32) * LOG2E
    oc_p = _swa(sinks, qc16, kc_p, vc_p, bc_p, nb=nb, nq=nq, tq=QBLK, row0=0, ck=ckp, r_base=0,
                r_step=QBLK, p_base=-2 * QBLK, p_step=QBLK, q_base=0, q_step=QBLK, coff=PROMPT_COFF, n_valid=t)

    buf = cache_c_k.shape[1]
    cks = -(-(buf + ds) // LANE) * LANE

    def cat_c(cache, new):
        a = jnp.concatenate([cache.reshape(db, buf, -1).astype(BF16), new[np_:].reshape(db, ds, -1)], axis=1)
        return jnp.pad(a, ((0, 0), (0, cks - buf - ds), (0, 0)))

    bc_s = _bias_tile(rel_bias, C_KV_HEADS, C_HEADS // C_KV_HEADS, ds, cks, -buf) * LOG2E
    oc_s = _swa(sinks, qc16, cat_c(cache_c_k, kc16), cat_c(cache_c_v, vc16), bc_s, nb=db, nq=1, tq=ds,
                row0=np_, ck=cks, r_base=0, r_step=0, p_base=past - buf, p_step=0, q_base=past,
                q_step=0, coff=0, n_valid=past + ds)
    return jnp.concatenate([oc_p, oc_s], axis=0), kc32, vc32


def _assemble(h2, ka32, va32, ki32, kb32, vb32, kc32, vc32, cache_c_k, cache_c_v, *, nb, tp, t, db, ds):
    np_ = nb * tp
    d = h2.shape[1]

    def pr(a, heads):
        a = a[:np_].reshape(nb, tp, -1)[:, :t]
        return a.reshape(nb, t, heads, HEAD_DIM) if heads else a[..., :HEAD_DIM]

    def sm(a, heads):
        a = a[np_:].reshape(db, ds, -1)
        return a.reshape(db, ds, heads, HEAD_DIM) if heads else a[..., :HEAD_DIM]

    y_prompt = h2[:np_].reshape(nb, tp, d)[:, N_META:t]
    y_sample = h2[np_:].reshape(db, ds, d)
    bufp = min(WINDOW, t)
    p_ck = pr(kc32, C_KV_HEADS)[:, t - bufp:]
    p_cv = pr(vc32, C_KV_HEADS)[:, t - bufp:]
    s_ck = jnp.concatenate([cache_c_k, sm(kc32, C_KV_HEADS)], axis=1)[:, ds:]
    s_cv = jnp.concatenate([cache_c_v, sm(vc32, C_KV_HEADS)], axis=1)[:, ds:]
    return (y_prompt, y_sample,
            pr(ka32, A_KV_HEADS), pr(va32, A_KV_HEADS), pr(ki32, 0), pr(kb32, B_HEADS), pr(vb32, B_HEADS),
            p_ck, p_cv,
            sm(ka32, A_KV_HEADS), sm(va32, A_KV_HEADS), sm(ki32, 0), sm(kb32, B_HEADS), sm(vb32, B_HEADS),
            s_ck, s_cv)
```

```python
import functools

import jax
import jax.numpy as jnp
from jax import lax
from jax.experimental import pallas as pl
from jax.experimental.pallas import tpu as pltpu

F32 = jnp.float32
BF16 = jnp.bfloat16
I32 = jnp.int32

D_MODEL = 1024
CHUNK_SHIFT = 6
N_META = 16
HEAD_DIM = 64
A_HEADS = 8
A_KV_HEADS = 2
IDX_HEADS = 4
TOPK_MAX = 256
B_HEADS = 8
C_HEADS = 16
C_KV_HEADS = 2
WINDOW = 128
WIN_CHUNKS = 2
NUM_BUCKETS = 32
N_EXPERTS = 8
TOP_K = 2
LN_EPS = 1e-5
DEPTH = 2
DN_ALPHA = (2.0 * DEPTH) ** 0.25

LANE = 128
QBLK = 128
DSA_CK = 512
DSA_PADF = 384
DSA_RB = 32
LOG2E = 1.4426950408889634
SB_BAND = 512
SB_BAND_PROMPT = 384
SB_PAIRS = 2
SB_STOP = -120.0
NEG = -1e30
PROMPT_COFF = 64 - N_META
INT_MIN = -2 ** 31
VMEM_LIMIT = 56 * 1024 * 1024


def _cparams(sem):
    return pltpu.CompilerParams(dimension_semantics=sem, vmem_limit_bytes=VMEM_LIMIT)


def _nt(a, b):
    return lax.dot_general(a, b, (((1,), (1,)), ((), ())), preferred_element_type=F32)


def _layer_norm(v, g, b):
    mu = jnp.mean(v, axis=-1, keepdims=True)
    c = v - mu
    var = jnp.mean(c * c, axis=-1, keepdims=True)
    return c * lax.rsqrt(var + LN_EPS) * g + b


def _proj_kernel(x_ref, w_ref, *out_refs, specs):
    xb = x_ref[...].astype(BF16)
    k = 0
    for c0, width, kinds in specs:
        acc = jnp.dot(xb, w_ref[:, c0:c0 + width], preferred_element_type=F32)
        for kind in kinds:
            if kind == "f32":
                val = acc
            elif kind == "bf16":
                val = acc.astype(BF16)
            elif kind == "bf16s":
                val = (acc * 0.125).astype(BF16)
            elif kind == "bf16e":
                val = (acc * (0.125 * LOG2E)).astype(BF16)
            elif kind == "hilo":
                hi = acc.astype(BF16)
                lo = (acc - hi.astype(F32)).astype(BF16)
                lane = lax.broadcasted_iota(I32, acc.shape, 1) & (LANE - 1)
                val = jnp.where(lane < HEAD_DIM, hi, lo)
            elif kind == "wi":
                val = acc * 0.0625
            else:
                raise ValueError(kind)
            out_refs[k][...] = val
            k += 1


def _project(x, w16, specs, tm):
    n = x.shape[0]
    out_shape, out_specs = [], []
    for _, width, kinds in specs:
        for kind in kinds:
            dt = F32 if kind in ("f32", "wi") else BF16
            out_shape.append(jax.ShapeDtypeStruct((n, width), dt))
            out_specs.append(pl.BlockSpec((tm, width), lambda i: (i, 0)))
    return pl.pallas_call(
        functools.partial(_proj_kernel, specs=specs),
        grid=(pl.cdiv(n, tm),),
        in_specs=[pl.BlockSpec((tm, x.shape[1]), lambda i: (i, 0)),
                  pl.BlockSpec(w16.shape, lambda i: (0, 0))],
        out_specs=out_specs,
        out_shape=out_shape,
        compiler_params=_cparams(("parallel",)),
        name="proj",
    )(x, w16)


def _mix_ln_kernel(x_ref, *refs, n_pairs):
    o_refs = refs[:n_pairs]
    w_refs = refs[n_pairs:2 * n_pairs]
    g_ref, b_ref, y_ref = refs[2 * n_pairs:]
    acc = DN_ALPHA * x_ref[...]
    for o_ref, w_ref in zip(o_refs, w_refs):
        acc = acc + jnp.dot(o_ref[...], w_ref[...], preferred_element_type=F32)
    y_ref[...] = _layer_norm(acc, g_ref[...], b_ref[...])


def _mix_ln(x, os_, ws, g, b, tm):
    n, d = x.shape
    in_specs = [pl.BlockSpec((tm, d), lambda i: (i, 0))]
    in_specs += [pl.BlockSpec((tm, o.shape[1]), lambda i: (i, 0)) for o in os_]
    in_specs += [pl.BlockSpec(w.shape, lambda i: (0, 0)) for w in ws]
    in_specs += [pl.BlockSpec((1, d), lambda i: (0, 0))] * 2
    return pl.pallas_call(
        functools.partial(_mix_ln_kernel, n_pairs=len(os_)),
        grid=(pl.cdiv(n, tm),),
        in_specs=in_specs,
        out_specs=pl.BlockSpec((tm, d), lambda i: (i, 0)),
        out_shape=jax.ShapeDtypeStruct((n, d), F32),
        compiler_params=_cparams(("parallel",)),
        name="mix_ln",
    )(x, *os_, *ws, g.reshape(1, d), b.reshape(1, d))


def _ffn_ln_kernel(y_ref, wg_ref, wu_ref, wd_ref, g_ref, b_ref, o_ref, *, nf):
    y = y_ref[...]
    yb = y.astype(BF16)
    acc = DN_ALPHA * y
    for f in range(nf):
        hg = jnp.dot(yb, wg_ref[f], preferred_element_type=F32)
        hu = jnp.dot(yb, wu_ref[f], preferred_element_type=F32)
        h = (hg * jax.nn.sigmoid(hg) * hu).astype(BF16)
        acc = acc + jnp.dot(h, wd_ref[f], preferred_element_type=F32)
    o_ref[...] = _layer_norm(acc, g_ref[...], b_ref[...])


def _ffn_ln(y, wg, wu, wd, g, b, tm, tf):
    n, d = y.shape
    dff = wg.shape[1]
    nf = dff // tf
    wg3 = wg.astype(BF16).reshape(d, nf, tf).transpose(1, 0, 2)
    wu3 = wu.astype(BF16).reshape(d, nf, tf).transpose(1, 0, 2)
    wd3 = wd.astype(BF16).reshape(nf, tf, d)
    full3 = lambda i: (0, 0, 0)
    return pl.pallas_call(
        functools.partial(_ffn_ln_kernel, nf=nf),
        grid=(pl.cdiv(n, tm),),
        in_specs=[pl.BlockSpec((tm, d), lambda i: (i, 0)),
                  pl.BlockSpec(wg3.shape, full3), pl.BlockSpec(wu3.shape, full3),
                  pl.BlockSpec(wd3.shape, full3),
                  pl.BlockSpec((1, d), lambda i: (0, 0)), pl.BlockSpec((1, d), lambda i: (0, 0))],
        out_specs=pl.BlockSpec((tm, d), lambda i: (i, 0)),
        out_shape=jax.ShapeDtypeStruct((n, d), F32),
        compiler_params=_cparams(("parallel",)),
        name="ffn_ln",
    )(y, wg3, wu3, wd3, g.reshape(1, d), b.reshape(1, d))


def _sort_key(x):
    bits = lax.bitcast_convert_type(x, I32)
    return bits ^ ((bits >> 31) & 0x7FFFFFFF)


def _dsa_kernel(rb_ref, qi_ref, wi_ref, qa_ref, ki_ref, ka_ref, va_ref, bn_ref, o_ref,
                keys_ref, cut_ref, s_ref, p_ref, madd_ref, wbc_ref, cmax_ref, alpha_ref,
                l_ref, acc_ref, seen_ref, *, tq, n_sel, e_base, e_step, q_base, q_step, coff, n_valid, padf):
    i = pl.program_id(1)
    ck = DSA_CK
    rbk = DSA_RB
    nt = ck // LANE
    e_end = e_base + e_step * i
    n_chunks = (e_end + ck - 1) // ck
    qpos = q_base + q_step * i + lax.broadcasted_iota(I32, (tq, 1), 0)
    bound = jnp.minimum(((((qpos + coff) >> CHUNK_SHIFT) + 1) << CHUNK_SHIFT) - coff, n_valid)
    lane_rb = lax.broadcasted_iota(I32, (rbk, ck), 1)
    lane128 = lax.broadcasted_iota(I32, (tq, LANE), 1)

    def key_rows(c):
        kpos0 = e_end - ck * (c + 1)
        return kpos0, pl.multiple_of(jnp.maximum(kpos0 + padf, 0), LANE)

    def tiles(x):
        return [x[:, t * LANE:(t + 1) * LANE] for t in range(nt)]

    qstack = jnp.concatenate([qi_ref[:, h * LANE:(h + 1) * LANE] for h in range(IDX_HEADS)], axis=0)
    for h in range(IDX_HEADS):
        wbc_ref[h] = jnp.broadcast_to(wi_ref[:, h:h + 1], (tq, LANE))
    cmax_ref[...] = jnp.full((tq, ck), -jnp.inf, F32)

    def idx_dot(c, dst):
        _, r = key_rows(c)
        s_ref[dst] = _nt(qstack, ki_ref[pl.ds(r, ck), :])

    def idx_keys(c, src):
        kpos0, _ = key_rows(c)
        for b in range(tq // rbk):
            r0 = b * rbk
            sc = None
            for h in range(IDX_HEADS):
                sh = jnp.maximum(s_ref[src, h * tq + r0:h * tq + r0 + rbk, :], 0.0)
                term = jnp.concatenate([wbc_ref[h, r0:r0 + rbk, :]] * nt, axis=1) * sh
                sc = term if sc is None else sc + term
            kpos = kpos0 + lane_rb
            allowed = (kpos >= 0) & (kpos < bound[r0:r0 + rbk])
            keys_ref[c, r0:r0 + rbk, :] = jnp.where(allowed, _sort_key(sc), INT_MIN)
            cmax_ref[r0:r0 + rbk, :] = jnp.maximum(cmax_ref[r0:r0 + rbk, :], jnp.where(allowed, sc, -jnp.inf))

    idx_dot(0, 0)

    def p1(j, carry):
        c = 2 * j
        idx_dot(c + 1, 1)
        idx_keys(c, 0)
        idx_dot(c + 2, 0)
        idx_keys(c + 1, 1)
        return carry

    lax.fori_loop(0, (n_chunks + 1) // 2, p1, 0)

    def count_ge(cand):
        def body(c, acc):
            u = keys_ref[c]
            for t in range(nt):
                acc = acc + jnp.where(u[:, t * LANE:(t + 1) * LANE] >= cand, 1.0, 0.0)
            return acc
        acc = lax.fori_loop(0, n_chunks, body, jnp.zeros((tq, LANE), F32))
        return jnp.sum(acc, axis=1, keepdims=True)

    cm = tiles(cmax_ref[...])
    if n_sel <= LANE:
        fold = [functools.reduce(jnp.maximum, cm)]
    elif n_sel <= 2 * LANE:
        fold = [jnp.maximum(cm[0], cm[2]), jnp.maximum(cm[1], cm[3])]
    else:
        fold = cm
    fmin = jnp.min(functools.reduce(jnp.minimum, fold), axis=1, keepdims=True)
    fmax = jnp.max(functools.reduce(jnp.maximum, cm), axis=1, keepdims=True)
    lo0 = _sort_key(fmin) - 1
    hi0 = _sort_key(fmax) + 2
    small = bound <= n_sel
    c_pos = count_ge(jnp.full((tq, 1), 1, I32))
    c_nn = count_ge(jnp.zeros((tq, 1), I32))
    pos_side = c_pos >= n_sel
    at_zero = jnp.logical_not(pos_side) & (c_nn >= n_sel)
    lo1 = jnp.where(pos_side, jnp.maximum(lo0, 1), lo0)
    top1 = jnp.where(pos_side, hi0, jnp.minimum(hi0, 0)) - 1
    fixed = small | at_zero
    nbits = jnp.where(fixed, 0, 32 - lax.clz(lo1 ^ top1))
    low_mask = (jnp.int32(1) << jnp.minimum(nbits, 31)) - 1
    t0 = jnp.where(small, INT_MIN + 1,
                   jnp.where(at_zero, 0, jnp.where(nbits >= 32, INT_MIN, lo1 & ~low_mask)))
    max_bits = jnp.max(nbits.astype(F32)).astype(I32)
    n_adm = bound.astype(F32)
    c_up0 = jnp.where(small, n_adm, jnp.where(at_zero, c_pos,
                                              jnp.where(pos_side | (hi0 <= 0), 0.0, c_nn)))
    c_at0 = jnp.where(small, n_adm, jnp.where(at_zero, c_nn, -1.0))

    def bis(it, st):
        t, c_at, c_up = st
        b = nbits - 1 - it
        cand = t + (jnp.int32(1) << jnp.maximum(b, 0))
        cnt = count_ge(cand)
        live = b >= 0
        take = live & (cnt >= n_sel)
        drop = live & (cnt < n_sel)
        return (jnp.where(take, cand, t), jnp.where(take, cnt, c_at), jnp.where(drop, cnt, c_up))

    thr, c_at, n_gt = lax.fori_loop(0, max_bits, bis, (t0, c_at0, c_up0))
    thr = jnp.maximum(thr, INT_MIN + 1)
    need = n_sel - n_gt
    excess = (c_at < 0.0) | ((c_at - n_gt) > need)

    cut_ref[...] = jnp.full((tq, LANE), 2 ** 30, I32)

    @pl.when(jnp.max(jnp.where(excess, 1.0, 0.0)) > 0.0)
    def _():
        rr = lax.broadcasted_iota(I32, (ck, ck), 0)
        cc = lax.broadcasted_iota(I32, (ck, ck), 1)
        upto = jnp.where(rr <= cc, 1.0, 0.0).astype(BF16)

        def tally(j, run):
            c = n_chunks - 1 - j
            seen_ref[c] = run
            u = keys_ref[c]
            for t in range(nt):
                run = run + jnp.where(u[:, t * LANE:(t + 1) * LANE] == thr, 1.0, 0.0)
            return run

        lax.fori_loop(0, n_chunks, tally, jnp.zeros((tq, LANE), F32))

        def locate(c, cut):
            kpos0, _ = key_rows(c)
            seen = jnp.sum(seen_ref[c], axis=1, keepdims=True)
            eq = jnp.where(keys_ref[c] == thr, 1.0, 0.0).astype(BF16)
            incl = jnp.dot(eq, upto, preferred_element_type=F32)
            below = jnp.sum(jnp.where(seen + incl < need, 1.0, 0.0), axis=1, keepdims=True)
            here = excess & (seen < need) & (seen + incl[:, ck - 1:ck] >= need)
            return jnp.where(here, kpos0 + below.astype(I32) + 1, cut)

        def locate4(j, cut):
            for k in range(4):
                cut = locate(jnp.minimum(4 * j + k, n_chunks - 1), cut)
            return cut

        cut = lax.fori_loop(0, (n_chunks + 3) // 4, locate4, jnp.full((tq, 1), 2 ** 30, I32))
        cut_ref[...] = jnp.broadcast_to(cut, (tq, LANE))

    cut = cut_ref[:, 0:1]

    r4 = A_HEADS // A_KV_HEADS
    qg = []
    for g in range(A_KV_HEADS):
        rows = []
        for j in range(r4):
            slot = qa_ref[:, j * LANE:(j + 1) * LANE]
            half = (lane128 < HEAD_DIM) if g == 0 else (lane128 >= HEAD_DIM)
            rows.append(jnp.where(half, slot, jnp.zeros_like(slot)))
        qg.append(jnp.concatenate(rows, axis=0))
    farb = [jnp.concatenate([jnp.full((tq, 1), rb_ref[g * r4 + j], F32) for j in range(r4)], axis=0)
            for g in range(A_KV_HEADS)]
    l_ref[...] = jnp.zeros(l_ref.shape, F32)
    acc_ref[...] = jnp.zeros(acc_ref.shape, F32)

    def logits_dot(c, g):
        _, r = key_rows(c)
        s_ref[g] = _nt(qg[g], ka_ref[pl.ds(r, ck), :])

    def select_mask(c):
        kpos0, _ = key_rows(c)
        for b in range(tq // rbk):
            r0 = b * rbk
            u = keys_ref[c, r0:r0 + rbk, :]
            t_b = thr[r0:r0 + rbk]
            tie = jnp.where(kpos0 + lane_rb < cut[r0:r0 + rbk], 0.0, NEG)
            madd_ref[r0:r0 + rbk, :] = jnp.where(u > t_b, 0.0, jnp.where(u == t_b, tie, NEG))

    def softmax_passes(g, m_old, near):
        m_rows = []
        for b in range(r4 * tq // rbk):
            r0 = b * rbk
            q0 = r0 % tq
            sm = s_ref[g, r0:r0 + rbk, :] + madd_ref[q0:q0 + rbk, :]
            if near:
                sm = sm + bn_ref[g, r0:r0 + rbk, :]
            m_blk = jnp.max(functools.reduce(jnp.maximum, tiles(sm)), axis=1, keepdims=True)
            far_b = farb[g][r0:r0 + rbk]
            if not near:
                m_blk = m_blk + far_b
            m_prev = m_old[r0:r0 + rbk]
            m_new = jnp.maximum(m_prev, m_blk)
            alpha = jnp.exp2(m_prev - m_new)
            alpha_ref[g, r0:r0 + rbk, :] = jnp.broadcast_to(alpha, (rbk, LANE))
            p = jnp.exp2(sm - (m_new if near else m_new - far_b))
            l_ref[g, r0:r0 + rbk, :] = alpha * l_ref[g, r0:r0 + rbk, :] + functools.reduce(jnp.add, tiles(p))
            p_ref[g, r0:r0 + rbk, :] = p.astype(BF16)
            m_rows.append(m_new)
        return jnp.concatenate(m_rows, axis=0)

    def value_dot(c, g):
        _, r = key_rows(c)
        acc_ref[g] = alpha_ref[g] * acc_ref[g] + jnp.dot(p_ref[g], va_ref[pl.ds(r, ck), :],
                                                        preferred_element_type=F32)

    m0 = jnp.full((r4 * tq, 1), NEG, F32)
    logits_dot(0, 0)
    logits_dot(0, 1)
    select_mask(0)
    m0n = softmax_passes(0, m0, True)
    value_dot(0, 0)
    logits_dot(1, 0)
    m1n = softmax_passes(1, m0, True)

    def p3(c, ms):
        value_dot(c - 1, 1)
        logits_dot(c, 1)
        select_mask(c)
        m_a = softmax_passes(0, ms[0], False)
        value_dot(c, 0)
        logits_dot(c + 1, 0)
        m_b = softmax_passes(1, ms[1], False)
        return (m_a, m_b)

    lax.fori_loop(1, n_chunks, p3, (m0n, m1n))
    value_dot(n_chunks - 1, 1)
    outs = [acc_ref[g] / jnp.sum(l_ref[g], axis=1, keepdims=True) for g in range(A_KV_HEADS)]
    for j in range(r4):
        lo = outs[0][j * tq:(j + 1) * tq]
        hi = outs[1][j * tq:(j + 1) * tq]
        o_ref[:, j * LANE:(j + 1) * LANE] = jnp.where(lane128 < HEAD_DIM, lo, hi).astype(BF16)


def _dsa(rb_far, qi, wi, qa, ki, ka, va, bn, *, nb, nq, tq, row0, n_sel, e_base, e_step,
         q_base, q_step, coff, n_valid, padf, n_chunks_max):
    rb0 = row0 // tq
    r4 = A_HEADS // A_KV_HEADS
    assert IDX_HEADS == r4 and tq % DSA_RB == 0
    qmap = lambda b, i: (rb0 + b * nq + i, 0)
    kmap = lambda b, i: (b, 0, 0)
    kern = functools.partial(_dsa_kernel, tq=tq, n_sel=n_sel, e_base=e_base, e_step=e_step,
                             q_base=q_base, q_step=q_step, coff=coff, n_valid=n_valid, padf=padf)
    return pl.pallas_call(
        kern,
        grid=(nb, nq),
        in_specs=[pl.BlockSpec(memory_space=pltpu.SMEM),
                  pl.BlockSpec((tq, qi.shape[1]), qmap),
                  pl.BlockSpec((tq, LANE), qmap),
                  pl.BlockSpec((tq, qa.shape[1]), qmap),
                  pl.BlockSpec((None,) + ki.shape[1:], kmap),
                  pl.BlockSpec((None,) + ka.shape[1:], kmap),
                  pl.BlockSpec((None,) + va.shape[1:], kmap),
                  pl.BlockSpec(bn.shape, lambda b, i: (0, 0, 0))],
        out_specs=pl.BlockSpec((tq, qa.shape[1]), lambda b, i: (b * nq + i, 0)),
        out_shape=jax.ShapeDtypeStruct((nb * nq * tq, qa.shape[1]), BF16),
        scratch_shapes=[pltpu.VMEM((n_chunks_max + 1, tq, DSA_CK), I32),
                        pltpu.VMEM((tq, LANE), I32),
                        pltpu.VMEM((2, r4 * tq, DSA_CK), F32),
                        pltpu.VMEM((A_KV_HEADS, r4 * tq, DSA_CK), BF16),
                        pltpu.VMEM((tq, DSA_CK), F32),
                        pltpu.VMEM((IDX_HEADS, tq, LANE), F32),
                        pltpu.VMEM((tq, DSA_CK), F32),
                        pltpu.VMEM((A_KV_HEADS, r4 * tq, LANE), F32),
                        pltpu.VMEM((A_KV_HEADS, r4 * tq, LANE), F32),
                        pltpu.VMEM((A_KV_HEADS, r4 * tq, LANE), F32),
                        pltpu.VMEM((n_chunks_max, tq, LANE), F32)],
        compiler_params=_cparams(("parallel", "arbitrary")),
        name="dsa",
    )(rb_far, qi, wi, qa, ki, ka, va, bn)


def _sb_kernel(tri_ref, q_ref, k_ref, v_ref, *rest, tq, e_base, e_step, q_base, q_step, padf, n_valid,
               newest_separate):
    if newest_separate:
        k0_ref, v0_ref, o_ref = rest
    else:
        (o_ref,) = rest
    i = pl.program_id(2)
    bw = tri_ref.shape[0]
    npp = q_ref.shape[1] // LANE
    nh = 2 * npp
    e_end = e_base + e_step * i
    n_bands = (e_end + bw - 1) // bw
    qpos = q_base + q_step * i + lax.broadcasted_iota(I32, (tq, 1), 0)
    lane = lax.broadcasted_iota(I32, (tq, LANE), 1)
    qh = []
    for p in range(npp):
        q = q_ref[:, p * LANE:(p + 1) * LANE]
        zero = jnp.zeros_like(q)
        qh += [jnp.where(lane < HEAD_DIM, q, zero), jnp.where(lane >= HEAD_DIM, q, zero)]
    lane_bw = lax.broadcasted_iota(I32, (tq, bw), 1)

    def cond(st):
        return (st[0] < n_bands) & (st[1] > SB_STOP)

    def body(st, newest=False):
        m, _, carry, acc = st
        kpos0 = e_end - bw * (m + 1)
        r = pl.multiple_of(kpos0 + padf, LANE)
        kpos = kpos0 + lane_bw
        before = (kpos < qpos) & (kpos >= 0) & (kpos < n_valid)

        def band(ref0, ref, p):
            cols = slice(p * LANE, (p + 1) * LANE)
            if newest and newest_separate:
                return ref0[:, cols]
            return ref[pl.ds(r, bw), cols].astype(BF16)

        zs, lss, lks, parts = [], [], [], []
        for h in range(nh):
            kt = band(k0_ref if newest_separate else None, k_ref, h // 2)
            z = _nt(qh[h], kt)
            ls = -(jnp.maximum(z, 0.0) + jnp.log1p(jnp.exp(-jnp.abs(z))))
            lk = jnp.where(before, ls, 0.0)
            hi = lk.astype(BF16)
            parts += [hi, (lk - hi.astype(F32)).astype(BF16)]
            zs.append(z)
            lss.append(ls)
            lks.append(lk)
        sums = jnp.dot(jnp.concatenate(parts, axis=0), tri_ref[...], preferred_element_type=F32)
        new_c, new_a = [], []
        worst = jnp.float32(-jnp.inf)
        for h in range(nh):
            vt = band(v0_ref if newest_separate else None, v_ref, h // 2)
            bl = sums[2 * h * tq:(2 * h + 1) * tq] + sums[(2 * h + 1) * tq:(2 * h + 2) * tq]
            w = jnp.where(before, jnp.exp(lss[h] + zs[h] + bl + carry[h]), 0.0)
            new_a.append(acc[h] + jnp.dot(w.astype(BF16), vt, preferred_element_type=F32))
            c_n = carry[h] + bl[:, 0:1] + lks[h][:, 0:1]
            new_c.append(c_n)
            worst = jnp.maximum(worst, jnp.max(c_n))
        return (m + 1, worst, tuple(new_c), tuple(new_a))

    init = (jnp.int32(0), jnp.float32(0.0),
            tuple(jnp.zeros((tq, 1), F32) for _ in range(nh)),
            tuple(jnp.zeros((tq, LANE), F32) for _ in range(nh)))
    _, _, _, acc = lax.while_loop(cond, body, body(init, newest=True))
    for p in range(npp):
        o_ref[:, p * LANE:(p + 1) * LANE] = jnp.where(lane < HEAD_DIM, acc[2 * p], acc[2 * p + 1]).astype(BF16)


def _stick_break(q, k, v, newest=None, *, band, nb, nq, tq, row0, e_base, e_step, q_base, q_step, padf, n_valid):
    rb0 = row0 // tq
    wq = SB_PAIRS * LANE
    npair = q.shape[1] // wq
    kern = functools.partial(_sb_kernel, tq=tq, e_base=e_base, e_step=e_step, q_base=q_base,
                             q_step=q_step, padf=padf, n_valid=n_valid, newest_separate=newest is not None)
    rr = lax.broadcasted_iota(I32, (band, band), 0)
    cc = lax.broadcasted_iota(I32, (band, band), 1)
    tri = jnp.where(rr > cc, 1.0, 0.0).astype(BF16)
    kspec = pl.BlockSpec((None, k.shape[1], wq), lambda b, p, i: (b, 0, p))
    in_specs = [pl.BlockSpec(tri.shape, lambda b, p, i: (0, 0)),
                pl.BlockSpec((tq, wq), lambda b, p, i: (rb0 + b * nq + i, p)), kspec, kspec]
    args = [tri, q, k, v]
    if newest is not None:
        in_specs += [pl.BlockSpec((None, band, wq), lambda b, p, i: (b, 0, p))] * 2
        args += list(newest)
    return pl.pallas_call(
        kern,
        grid=(nb, npair, nq),
        in_specs=in_specs,
        out_specs=pl.BlockSpec((tq, wq), lambda b, p, i: (b * nq + i, p)),
        out_shape=jax.ShapeDtypeStruct((nb * nq * tq, q.shape[1]), BF16),
        compiler_params=_cparams(("parallel", "parallel", "arbitrary")),
        name="stick_break",
    )(*args)


def _swa_kernel(sink_ref, q_ref, k_ref, v_ref, bias_ref, o_ref, s_ref, p_ref, madd_ref,
                *, tq, ck, r_base, r_step, p_base, p_step, q_base, q_step, coff, n_valid):
    i = pl.program_id(1)
    r0 = pl.multiple_of(r_base + r_step * i, 16)
    kpos0 = p_base + p_step * i
    kt = k_ref[pl.ds(r0, ck), :]
    vt = v_ref[pl.ds(r0, ck), :]
    qpos = q_base + q_step * i + lax.broadcasted_iota(I32, (tq, 1), 0)
    kpos = kpos0 + lax.broadcasted_iota(I32, (tq, ck), 1)
    qc = (qpos + coff) >> CHUNK_SHIFT
    hi_b = jnp.minimum(((qc + 1) << CHUNK_SHIFT) - coff, n_valid)
    lo_b = jnp.maximum(((qc - WIN_CHUNKS) << CHUNK_SHIFT) - coff, 0)
    r8 = C_HEADS // C_KV_HEADS
    rbk = DSA_RB
    nt = ck // LANE
    madd_ref[...] = jnp.where((kpos >= lo_b) & (kpos < hi_b), 0.0, NEG)
    lane = lax.broadcasted_iota(I32, (tq, LANE), 1)

    def tiles(x):
        return [x[:, t * LANE:(t + 1) * LANE] for t in range(nt)]

    outs = []
    for g in range(C_KV_HEADS):
        rows = []
        for j in range(r8):
            slot = q_ref[:, j * LANE:(j + 1) * LANE]
            half = (lane < HEAD_DIM) if g == 0 else (lane >= HEAD_DIM)
            rows.append(jnp.where(half, slot, jnp.zeros_like(slot)))
        s_ref[g] = _nt(jnp.concatenate(rows, axis=0), kt)
        dens = []
        for b in range(r8 * tq // rbk):
            rs = slice(b * rbk, (b + 1) * rbk)
            q0 = (b * rbk) % tq
            sink = jnp.full((rbk, 1), sink_ref[g * r8 + (b * rbk) // tq], F32)
            sm = s_ref[g, rs, :] + bias_ref[g, rs, :] + madd_ref[q0:q0 + rbk, :]
            m = jnp.maximum(jnp.max(functools.reduce(jnp.maximum, tiles(sm)), axis=1, keepdims=True), sink)
            e = jnp.exp2(sm - m)
            dens.append(jnp.sum(functools.reduce(jnp.add, tiles(e)), axis=1, keepdims=True)
                        + jnp.exp2(sink - m))
            p_ref[g, rs, :] = e.astype(BF16)
        outs.append(jnp.dot(p_ref[g], vt, preferred_element_type=F32) / jnp.concatenate(dens, axis=0))
    for j in range(r8):
        lo = outs[0][j * tq:(j + 1) * tq]
        hi = outs[1][j * tq:(j + 1) * tq]
        o_ref[:, j * LANE:(j + 1) * LANE] = jnp.where(lane < HEAD_DIM, lo, hi).astype(BF16)


def _swa(sinks, q, k, v, bias, *, nb, nq, tq, row0, ck, r_base, r_step, p_base, p_step,
         q_base, q_step, coff, n_valid):
    rb0 = row0 // tq
    r8 = C_HEADS // C_KV_HEADS
    assert tq % DSA_RB == 0 and ck % LANE == 0
    kern = functools.partial(_swa_kernel, tq=tq, ck=ck, r_base=r_base, r_step=r_step, p_base=p_base,
                             p_step=p_step, q_base=q_base, q_step=q_step, coff=coff, n_valid=n_valid)
    kmap = lambda b, i: (b, 0, 0)
    return pl.pallas_call(
        kern,
        grid=(nb, nq),
        in_specs=[pl.BlockSpec(memory_space=pltpu.SMEM),
                  pl.BlockSpec((tq, q.shape[1]), lambda b, i: (rb0 + b * nq + i, 0)),
                  pl.BlockSpec((None,) + k.shape[1:], kmap),
                  pl.BlockSpec((None,) + v.shape[1:], kmap),
                  pl.BlockSpec(bias.shape, lambda b, i: (0, 0, 0))],
        out_specs=pl.BlockSpec((tq, q.shape[1]), lambda b, i: (b * nq + i, 0)),
        out_shape=jax.ShapeDtypeStruct((nb * nq * tq, q.shape[1]), BF16),
        scratch_shapes=[pltpu.VMEM((C_KV_HEADS, r8 * tq, ck), F32),
                        pltpu.VMEM((C_KV_HEADS, r8 * tq, ck), BF16),
                        pltpu.VMEM((tq, ck), F32)],
        compiler_params=_cparams(("parallel", "arbitrary")),
        name="swa",
    )(sinks, q, k, v, bias)


def _router_kernel(y_ref, rh_ref, rl_ref, idx_ref, gate_ref):
    y = y_ref[...]
    yh = y.astype(BF16)
    yl = (y - yh.astype(F32)).astype(BF16)
    rh = rh_ref[...]
    logits = (jnp.dot(yh, rh, preferred_element_type=F32) + jnp.dot(yl, rh, preferred_element_type=F32)
              + jnp.dot(yh, rl_ref[...], preferred_element_type=F32))
    lane = lax.broadcasted_iota(I32, logits.shape, 1)
    logits = jnp.where(lane < N_EXPERTS, logits, -jnp.inf)
    m1 = jnp.max(logits, axis=1, keepdims=True)
    i1 = jnp.min(jnp.where(logits == m1, lane, LANE), axis=1, keepdims=True)
    rest = jnp.where(lane == i1, -jnp.inf, logits)
    m2 = jnp.max(rest, axis=1, keepdims=True)
    i2 = jnp.min(jnp.where(rest == m2, lane, LANE), axis=1, keepdims=True)
    e2 = jnp.exp(m2 - m1)
    den = 1.0 + e2
    idx_ref[...] = jnp.where(lane == 0, i1, jnp.where(lane == 1, i2, 0))
    gate_ref[...] = jnp.where(lane == 0, 1.0 / den, jnp.where(lane == 1, e2 / den, 0.0))


def _router(y, router, tm):
    n, d = y.shape
    rpad = jnp.pad(router.astype(F32), ((0, 0), (0, LANE - router.shape[1])))
    rh = rpad.astype(BF16)
    rl = (rpad - rh.astype(F32)).astype(BF16)
    return pl.pallas_call(
        _router_kernel,
        grid=(pl.cdiv(n, tm),),
        in_specs=[pl.BlockSpec((tm, d), lambda i: (i, 0)),
                  pl.BlockSpec((d, LANE), lambda i: (0, 0)), pl.BlockSpec((d, LANE), lambda i: (0, 0))],
        out_specs=[pl.BlockSpec((tm, LANE), lambda i: (i, 0)), pl.BlockSpec((tm, LANE), lambda i: (i, 0))],
        out_shape=[jax.ShapeDtypeStruct((n, LANE), I32), jax.ShapeDtypeStruct((n, LANE), F32)],
        compiler_params=_cparams(("parallel",)),
        name="router",
    )(y, rh, rl)


def _moe_kernel(be_ref, nu_ref, x_ref, wg_ref, wu_ref, wd_ref, o_ref, acc_ref, xb_ref):
    t = pl.program_id(0)
    f = pl.program_id(1)

    @pl.when(f == 0)
    def _():
        acc_ref[...] = jnp.zeros_like(acc_ref)
        xb_ref[...] = x_ref[...].astype(BF16)

    @pl.when(t < nu_ref[0])
    def _():
        xb = xb_ref[...]
        hg = jnp.dot(xb, wg_ref[...].astype(BF16), preferred_element_type=F32)
        hu = jnp.dot(xb, wu_ref[...].astype(BF16), preferred_element_type=F32)
        h = (hg * jax.nn.sigmoid(hg) * hu).astype(BF16)
        acc_ref[...] += jnp.dot(h, wd_ref[...].astype(BF16), preferred_element_type=F32)

    @pl.when(f == pl.num_programs(1) - 1)
    def _():
        o_ref[...] = acc_ref[...]


def _moe_experts(block_expert, n_used, xs, wg, wu, wd, tmb, tf):
    n_slots, d = xs.shape
    de = wg.shape[2]
    grid_spec = pltpu.PrefetchScalarGridSpec(
        num_scalar_prefetch=2,
        grid=(n_slots // tmb, de // tf),
        in_specs=[pl.BlockSpec((tmb, d), lambda t, f, be, nu: (t, 0)),
                  pl.BlockSpec((None, d, tf), lambda t, f, be, nu: (be[t], 0, f)),
                  pl.BlockSpec((None, d, tf), lambda t, f, be, nu: (be[t], 0, f)),
                  pl.BlockSpec((None, tf, d), lambda t, f, be, nu: (be[t], f, 0))],
        out_specs=pl.BlockSpec((tmb, d), lambda t, f, be, nu: (t, 0)),
        scratch_shapes=[pltpu.VMEM((tmb, d), F32), pltpu.VMEM((tmb, d), BF16)],
    )
    return pl.pallas_call(
        _moe_kernel,
        grid_spec=grid_spec,
        out_shape=jax.ShapeDtypeStruct((n_slots, d), F32),
        compiler_params=_cparams(("parallel", "arbitrary")),
        name="moe_experts",
    )(block_expert, n_used, xs, wg, wu, wd)


def _row_copy(src_hbm, src_row, dst, dst_row, sem):
    return pltpu.make_async_copy(src_hbm.at[pl.ds(src_row, 1)], dst.at[pl.ds(dst_row, 1)], sem)


def _dispatch_kernel(slot_ref, x_ref, init_hbm, xs_hbm, sem, *, tt):
    del init_hbm

    def start(j, carry):
        for k in range(TOP_K):
            _row_copy(x_ref, j, xs_hbm, slot_ref[0, 0, TOP_K * j + k], sem).start(priority=k % 2)
        return carry

    def wait(j, carry):
        _row_copy(x_ref, 0, xs_hbm, 0, sem).wait()
        return carry

    lax.fori_loop(0, tt, start, 0, unroll=8)
    lax.fori_loop(0, TOP_K * tt, wait, 0, unroll=8)


def _dispatch(x, slot3, n_slots, tt):
    n, d = x.shape
    init = jnp.zeros((n_slots, d), x.dtype)
    return pl.pallas_call(
        functools.partial(_dispatch_kernel, tt=tt),
        grid=(n // tt,),
        in_specs=[pl.BlockSpec((1, 1, TOP_K * tt), lambda i: (i, 0, 0), memory_space=pltpu.SMEM),
                  pl.BlockSpec((tt, d), lambda i: (i, 0)), pl.BlockSpec(memory_space=pl.ANY)],
        out_specs=pl.BlockSpec(memory_space=pl.ANY),
        out_shape=jax.ShapeDtypeStruct((n_slots, d), x.dtype),
        scratch_shapes=[pltpu.SemaphoreType.DMA],
        input_output_aliases={2: 0},
        compiler_params=_cparams(("arbitrary",)),
        name="moe_dispatch",
    )(slot3, x, init)


def _combine_ln_kernel(slot_ref, y_ref, gate_ref, g_ref, b_ref, outs_hbm, o_ref, buf_ref, sem, *, tt):
    def start(j, carry):
        for k in range(TOP_K):
            _row_copy(outs_hbm, slot_ref[0, 0, TOP_K * j + k], buf_ref.at[k], j, sem).start(priority=k % 2)
        return carry

    def wait(j, carry):
        _row_copy(outs_hbm, 0, buf_ref.at[0], 0, sem).wait()
        return carry

    lax.fori_loop(0, tt, start, 0, unroll=8)
    lax.fori_loop(0, TOP_K * tt, wait, 0, unroll=8)
    mo = gate_ref[:, 0:1] * buf_ref[0]
    for k in range(1, TOP_K):
        mo = mo + gate_ref[:, k:k + 1] * buf_ref[k]
    o_ref[...] = _layer_norm(DN_ALPHA * y_ref[...] + mo, g_ref[...], b_ref[...])


def _combine_ln(y, outs, slot3, gate_p, g, b, tt):
    n, d = y.shape
    row = pl.BlockSpec((tt, d), lambda i: (i, 0))
    vec = pl.BlockSpec((1, d), lambda i: (0, 0))
    return pl.pallas_call(
        functools.partial(_combine_ln_kernel, tt=tt),
        grid=(n // tt,),
        in_specs=[pl.BlockSpec((1, 1, TOP_K * tt), lambda i: (i, 0, 0), memory_space=pltpu.SMEM),
                  row, pl.BlockSpec((tt, LANE), lambda i: (i, 0)), vec, vec,
                  pl.BlockSpec(memory_space=pl.ANY)],
        out_specs=row,
        out_shape=jax.ShapeDtypeStruct((n, d), F32),
        scratch_shapes=[pltpu.VMEM((TOP_K, tt, d), F32), pltpu.SemaphoreType.DMA],
        compiler_params=_cparams(("arbitrary",)),
        name="moe_combine_ln",
    )(slot3, y, gate_p, g.reshape(1, d), b.reshape(1, d), outs)


def _moe(y, router, w_gate, w_up, w_down, g, b, tm, tmb, tf, tt):
    n, d = y.shape
    assert n % tt == 0
    idx_p, gate_p = _router(y, router, tm)
    e_flat = idx_p[:, :TOP_K].reshape(-1)
    onehot = (e_flat[:, None] == jnp.arange(N_EXPERTS, dtype=I32)[None, :]).astype(I32)
    rank = jnp.sum((jnp.cumsum(onehot, axis=0) - onehot) * onehot, axis=1)
    counts = jnp.sum(onehot, axis=0)
    padded = (counts + tmb - 1) // tmb * tmb
    pad_end = jnp.cumsum(padded)
    slot = ((pad_end - padded)[e_flat] + rank).astype(I32)
    n_blocks = -(-(TOP_K * n) // tmb) + N_EXPERTS
    n_slots = n_blocks * tmb
    block_expert = jnp.minimum(
        jnp.searchsorted(pad_end, jnp.arange(n_blocks, dtype=I32) * tmb, side="right"),
        N_EXPERTS - 1).astype(I32)
    n_used = (pad_end[-1] // tmb).astype(I32).reshape(1)
    slot3 = slot.reshape(n // tt, 1, TOP_K * tt)
    xs = _dispatch(y, slot3, n_slots, tt)
    outs = _moe_experts(block_expert, n_used, xs, w_gate, w_up, w_down, tmb, tf)
    return _combine_ln(y, outs, slot3, gate_p, g, b, tt)


def _t5_bucket(rel):
    half = NUM_BUCKETS // 2
    exact = half // 2
    n = jnp.abs(rel)
    far = exact + sum((n >= t).astype(I32) for t in (12, 16, 23, 32, 46, 64, 91))
    return jnp.where(rel > 0, half, 0) + jnp.where(n < exact, n, far)


def _bias_tile(rel_bias, n_groups, per_group, tq, ck, d0):
    d = d0 + jnp.arange(ck, dtype=I32)[None, :] - jnp.arange(tq, dtype=I32)[:, None]
    onehot = (_t5_bucket(d)[..., None] == jnp.arange(NUM_BUCKETS, dtype=I32)).astype(F32)
    tile = jnp.einsum("qkb,bh->qkh", onehot, rel_bias.astype(F32),
                      precision=lax.Precision.HIGHEST)
    tile = jnp.transpose(tile[:, :, :n_groups * per_group], (2, 0, 1))
    return tile.reshape(n_groups, per_group * tq, ck)


def _cols(w, ranges):
    parts = []
    for r in ranges:
        if isinstance(r, int):
            parts.append(jnp.zeros((w.shape[0], r), w.dtype))
        else:
            parts.append(w[:, r[0]:r[1]])
    return jnp.concatenate(parts, axis=1)


def _head_pair_order(n_heads):
    half = n_heads // 2
    order = []
    for j in range(half):
        order += [j, half + j]
    return order


L0_SPECS = ((0, 512, ("bf16e",)), (512, 128, ("f32", "bf16")), (640, 128, ("f32", "bf16")),
            (768, 512, ("hilo",)), (1280, 128, ("f32", "bf16")), (1408, 128, ("wi",)),
            (1536, 512, ("bf16s",)), (2048, 512, ("f32", "bf16")), (2560, 512, ("f32", "bf16")))
L1_SPECS = ((0, 1024, ("bf16e",)), (1024, 128, ("f32", "bf16")), (1152, 128, ("f32", "bf16")))


def _l0_weight(w_in):
    hd = HEAD_DIM
    rng = [(hd * h, hd * h + hd) for h in _head_pair_order(A_HEADS)]
    rng += [(512, 640), (640, 768)]
    for h in range(IDX_HEADS):
        rng += [(768 + hd * h, 768 + hd * h + hd)] * 2
    rng += [(1024, 1088)] * 2
    rng += [(1088, 1092), LANE - IDX_HEADS]
    rng += [(1092, 1604), (1604, 2116), (2116, 2628)]
    return _cols(w_in, rng).astype(BF16)


def _l1_weight(w_in):
    hd = HEAD_DIM
    rng = [(hd * h, hd * h + hd) for h in _head_pair_order(C_HEADS)]
    rng += [(1024, 1152), (1152, 1280)]
    return _cols(w_in, rng).astype(BF16)


def _perm_rows(w, n_heads):
    return jnp.concatenate([w[HEAD_DIM * h:HEAD_DIM * (h + 1)] for h in _head_pair_order(n_heads)], axis=0)


def _front_pad(a, nb, rows_in, front, rows_out):
    a = a.reshape(nb, rows_in, a.shape[-1])
    return jnp.pad(a, ((0, 0), (front, rows_out - front - rows_in), (0, 0)))


def _layer0_attention(x_prompt, x_sample, cache_a_k, cache_a_v, cache_a_idx_k, cache_b_k, cache_b_v,
                      meta_tokens, rel_bias, l0_w_in):
    nb, seq, d = x_prompt.shape
    t = N_META + seq
    nq = -(-t // QBLK)
    tp = nq * QBLK
    db, ds, _ = x_sample.shape
    past = cache_a_k.shape[1]
    assert d == D_MODEL and past % LANE == 0 and ds % DSA_RB == 0 and ds <= 64
    np_, ns = nb * tp, db * ds
    tm = 512

    meta = jnp.broadcast_to(meta_tokens[None].astype(x_prompt.dtype), (nb, N_META, d))
    hp = jnp.concatenate([meta, x_prompt, jnp.zeros((nb, tp - t, d), x_prompt.dtype)], axis=1)
    x_all = jnp.concatenate([hp.reshape(np_, d), x_sample.reshape(ns, d)], axis=0)
    rel_bias = rel_bias.astype(F32)
    rb_far = rel_bias[NUM_BUCKETS // 2 - 1] * LOG2E

    (qa16, ka32, ka16, va32, va16, qi16, ki32, ki16, wi32, qb16, kb32, kb16, vb32, vb16) = _project(
        x_all, _l0_weight(l0_w_in), L0_SPECS, tm)

    n_sel_p = min(TOPK_MAX, (t - N_META) // 4)
    kr = QBLK * (nq + 1) + DSA_PADF
    ncm = -(-(QBLK * (nq + 1)) // DSA_CK)
    pk = [_front_pad(a[:np_], nb, tp, DSA_PADF, kr) for a in (ki16, ka16, va16)]
    bn_p = _bias_tile(rel_bias, A_KV_HEADS, A_HEADS // A_KV_HEADS, QBLK, DSA_CK, -(DSA_CK // 2)) * LOG2E
    oa_p = _dsa(rb_far, qi16, wi32, qa16, *pk, bn_p, nb=nb, nq=nq, tq=QBLK, row0=0, n_sel=n_sel_p,
                e_base=2 * QBLK, e_step=QBLK, q_base=0, q_step=QBLK, coff=PROMPT_COFF,
                n_valid=t, padf=DSA_PADF, n_chunks_max=ncm)

    n_keys = past + ds
    n_sel_s = min(TOPK_MAX, n_keys // 4)
    e_s = -(-n_keys // LANE) * LANE
    ncs = -(-e_s // DSA_CK)
    krs = ncs * DSA_CK
    fs = krs - e_s

    def cat_keys(cache, new, width_dup):
        c = cache.reshape(db, past, -1).astype(BF16)
        if width_dup:
            c = jnp.concatenate([c, c], axis=-1)
        a = jnp.concatenate([c, new[np_:].reshape(db, ds, -1)], axis=1)
        return jnp.pad(a, ((0, 0), (fs, krs - fs - n_keys), (0, 0)))

    sk = [cat_keys(cache_a_idx_k, ki16, True), cat_keys(cache_a_k, ka16, False),
          cat_keys(cache_a_v, va16, False)]
    bn_s = _bias_tile(rel_bias, A_KV_HEADS, A_HEADS // A_KV_HEADS, ds, DSA_CK, e_s - DSA_CK - past) * LOG2E
    oa_s = _dsa(rb_far, qi16, wi32, qa16, *sk, bn_s, nb=db, nq=1, tq=ds, row0=np_, n_sel=n_sel_s,
                e_base=e_s, e_step=0, q_base=past, q_step=0, coff=0, n_valid=n_keys, padf=fs,
                n_chunks_max=ncs)

    pfb = SB_BAND_PROMPT - QBLK
    kb_p = _front_pad(kb16[:np_], nb, tp, pfb, tp + pfb)
    vb_p = _front_pad(vb16[:np_], nb, tp, pfb, tp + pfb)
    ob_p = _stick_break(qb16, kb_p, vb_p, band=SB_BAND_PROMPT, nb=nb, nq=nq, tq=QBLK, row0=0, e_base=QBLK, e_step=QBLK,
                        q_base=0, q_step=QBLK, padf=pfb, n_valid=t)
    assert past % SB_BAND == 0 and ds <= SB_BAND

    def newest_band(new):
        return jnp.pad(new[np_:].reshape(db, ds, -1), ((0, 0), (0, SB_BAND - ds), (0, 0)))

    ob_s = _stick_break(qb16, cache_b_k.reshape(db, past, -1), cache_b_v.reshape(db, past, -1),
                        (newest_band(kb16), newest_band(vb16)), band=SB_BAND, nb=db, nq=1, tq=ds, row0=np_,
                        e_base=past + SB_BAND, e_step=0, q_base=past, q_step=0, padf=0, n_valid=n_keys)

    oa = jnp.concatenate([oa_p, oa_s], axis=0)
    ob = jnp.concatenate([ob_p, ob_s], axis=0)
    return dict(x_all=x_all, oa=oa, ob=ob, np=np_, tp=tp, t=t, nq=nq, rel_bias=rel_bias,
                ka32=ka32, va32=va32, ki32=ki32, kb32=kb32, vb32=vb32)


def kernel(x_prompt, x_sample, cache_a_k, cache_a_v, cache_a_idx_k, cache_b_k, cache_b_v, cache_c_k, cache_c_v, meta_tokens, rel_bias, l0_w_in, l0_w_out, l0_ln1_g, l0_ln1_b, l0_w_gate, l0_w_up, l0_w_down, l0_ln2_g, l0_ln2_b, l1_w_in, l1_sinks, l1_w_out, l1_ln1_g, l1_ln1_b, l1_router, l1_w_gate, l1_w_up, l1_w_down, l1_ln2_g, l1_ln2_b):
    a0 = _layer0_attention(x_prompt, x_sample, cache_a_k, cache_a_v, cache_a_idx_k, cache_b_k, cache_b_v,
                           meta_tokens, rel_bias, l0_w_in)
    x_all, oa, ob, np_, tp, t, nq, rel_bias = (a0[k] for k in ("x_all", "oa", "ob", "np", "tp", "t", "nq", "rel_bias"))
    ka32, va32, ki32, kb32, vb32 = (a0[k] for k in ("ka32", "va32", "ki32", "kb32", "vb32"))
    nb, _, d = x_prompt.shape
    db, ds, _ = x_sample.shape
    past = cache_a_k.shape[1]
    tm = 512
    w_out0 = l0_w_out.astype(BF16)
    half0 = A_HEADS * HEAD_DIM
    y0 = _mix_ln(x_all, [oa, ob], [_perm_rows(w_out0[:half0], A_HEADS), w_out0[half0:]],
                 l0_ln1_g, l0_ln1_b, tm)
    h1 = _ffn_ln(y0, l0_w_gate, l0_w_up, l0_w_down, l0_ln2_g, l0_ln2_b, tm, 256)

    oc, kc32, vc32 = _layer1_attention(h1, cache_c_k, cache_c_v, rel_bias, l1_w_in, l1_sinks,
                                       nb=nb, nq=nq, t=t, db=db, ds=ds, past=past, tm=tm)
    y1 = _mix_ln(h1, [oc], [_perm_rows(l1_w_out.astype(BF16), C_HEADS)], l1_ln1_g, l1_ln1_b, tm)
    n_all = y1.shape[0]
    tt = next(c for c in (256, 128, 64, 32, 16, 8) if n_all % c == 0)
    h2 = _moe(y1, l1_router, l1_w_gate, l1_w_up, l1_w_down, l1_ln2_g, l1_ln2_b, tm, 1024, 512, tt)
    return _assemble(h2, ka32, va32, ki32, kb32, vb32, kc32, vc32, cache_c_k, cache_c_v,
                     nb=nb, tp=tp, t=t, db=db, ds=ds)


def _layer1_attention(h1, cache_c_k, cache_c_v, rel_bias, l1_w_in, l1_sinks, *, nb, nq, t, db, ds, past, tm):
    tp = nq * QBLK
    np_ = nb * tp
    qc16, kc32, kc16, vc32, vc16 = _project(h1, _l1_weight(l1_w_in), L1_SPECS, tm)
    ckp = 4 * QBLK
    krc = QBLK * (nq - 1) + ckp
    kc_p = _front_pad(kc16[:np_], nb, tp, 2 * QBLK, max(krc, tp + 2 * QBLK))
    vc_p = _front_pad(vc16[:np_], nb, tp, 2 * QBLK, max(krc, tp + 2 * QBLK))
    bc_p = _bias_tile(rel_bias, C_KV_HEADS, C_HEADS // C_KV_HEADS, QBLK, ckp, -2 * QBLK) * LOG2E
    sinks = l1_sinks.astype(F32) * LOG2E
    oc_p = _swa(sinks, qc16, kc_p, vc_p, bc_p, nb=nb, nq=nq, tq=QBLK, row0=0, ck=ckp, r_base=0,
                r_step=QBLK, p_base=-2 * QBLK, p_step=QBLK, q_base=0, q_step=QBLK, coff=PROMPT_COFF, n_valid=t)

    buf = cache_c_k.shape[1]
    cks = -(-(buf + ds) // LANE) * LANE

    def cat_c(cache, new):
        a = jnp.concatenate([cache.reshape(db, buf, -1).astype(BF16), new[np_:].reshape(db, ds, -1)], axis=1)
        return jnp.pad(a, ((0, 0), (0, cks - buf - ds), (0, 0)))

    bc_s = _bias_tile(rel_bias, C_KV_HEADS, C_HEADS // C_KV_HEADS, ds, cks, -buf) * LOG2E
    oc_s = _swa(sinks, qc16, cat_c(cache_c_k, kc16), cat_c(cache_c_v, vc16), bc_s, nb=db, nq=1, tq=ds,
                row0=np_, ck=cks, r_base=0, r_step=0, p_base=past - buf, p_step=0, q_base=past,
                q_step=0, coff=0, n_valid=past + ds)
    return jnp.concatenate([oc_p, oc_s], axis=0), kc32, vc32


def _assemble(h2, ka32, va32, ki32, kb32, vb32, kc32, vc32, cache_c_k, cache_c_v, *, nb, tp, t, db, ds):
    np_ = nb * tp
    d = h2.shape[1]

    def pr(a, heads):
        a = a[:np_].reshape(nb, tp, -1)[:, :t]
        return a.reshape(nb, t, heads, HEAD_DIM) if heads else a[..., :HEAD_DIM]

    def sm(a, heads):
        a = a[np_:].reshape(db, ds, -1)
        return a.reshape(db, ds, heads, HEAD_DIM) if heads else a[..., :HEAD_DIM]

    y_prompt = h2[:np_].reshape(nb, tp, d)[:, N_META:t]
    y_sample = h2[np_:].reshape(db, ds, d)
    bufp = min(WINDOW, t)
    p_ck = pr(kc32, C_KV_HEADS)[:, t - bufp:]
    p_cv = pr(vc32, C_KV_HEADS)[:, t - bufp:]
    s_ck = jnp.concatenate([cache_c_k, sm(kc32, C_KV_HEADS)], axis=1)[:, ds:]
    s_cv = jnp.concatenate([cache_c_v, sm(vc32, C_KV_HEADS)], axis=1)[:, ds:]
    return (y_prompt, y_sample,
            pr(ka32, A_KV_HEADS), pr(va32, A_KV_HEADS), pr(ki32, 0), pr(kb32, B_HEADS), pr(vb32, B_HEADS),
            p_ck, p_cv,
            sm(ka32, A_KV_HEADS), sm(va32, A_KV_HEADS), sm(ki32, 0), sm(kb32, B_HEADS), sm(vb32, B_HEADS),
            s_ck, s_cv)
```
